```python
import math
import jax
import jax.numpy as jnp
from jax import lax
import numpy as np

D_MODEL = 1024
BATCH = 4
SEQ = 4096
DEPTH = 2

GRID_W = 64
CTX_LEN = 256
N_MIXERS = 2
N_ATTN_LAYERS = (DEPTH + 1) // 2
N_LRU_LAYERS = DEPTH // 2
HEAD_DIM = 64
N_HEADS = 16
N_KV_HEADS = 4
GROUP = N_HEADS // N_KV_HEADS
WINDOW = 128
ATTN_BLOCK = 128
ROPE_THETA = 10000.0
D_RNN = 1280
LRU_BLOCKS = 16
LRU_BLOCK_W = D_RNN // LRU_BLOCKS
CONV_W = 4
CONV_LEFT = 2
LRU_C = 8.0
N_EXPERTS = 64
TOP_K = 8
N_GROUPS = 8
TOPK_GROUPS = 4
D_EXPERT = 256
ROUTED_SCALE = 2.5
MOE_BLOCK = 128
RMS_EPS = 1e-6

kernel_name = 'hybrid_swa_sink_rglru_moe_dit'


def rms_norm(x, g):
    xf = x.astype(jnp.float32)
    y = xf * lax.rsqrt(jnp.mean(xf * xf, axis=-1, keepdims=True) + RMS_EPS)
    return (y * g.astype(jnp.float32)).astype(x.dtype)


def modulate(h, shift, scale):
    return h * (1 + scale) + shift


def swiglu(t, wg, wu, wd):
    return (jax.nn.silu(t @ wg) * (t @ wu)) @ wd


def rope_angles(n_tokens):
    rows = n_tokens // GRID_W
    row = jnp.repeat(jnp.arange(rows), GRID_W).astype(jnp.float32)
    col = jnp.tile(jnp.arange(GRID_W), rows).astype(jnp.float32)
    n_freq = HEAD_DIM // 4
    inv = jnp.exp(-math.log(ROPE_THETA) * jnp.arange(n_freq, dtype=jnp.float32) / n_freq)
    return row[:, None] * inv, col[:, None] * inv


def rotate_axis(u, ang):
    u1, u2 = jnp.split(u, 2, axis=-1)
    shape = (1, ang.shape[0]) + (1,) * (u.ndim - 3) + (ang.shape[-1],)
    cos = jnp.cos(ang).reshape(shape)
    sin = jnp.sin(ang).reshape(shape)
    return jnp.concatenate([u1 * cos - u2 * sin, u2 * cos + u1 * sin], axis=-1)


def rope_2d(u, ang_row, ang_col):
    ur, uc = jnp.split(u, 2, axis=-1)
    return jnp.concatenate([rotate_axis(ur, ang_row), rotate_axis(uc, ang_col)], axis=-1)


def softmax_with_sink(s, sink):
    full = jnp.concatenate([s, jnp.broadcast_to(sink, s.shape[:-1] + (1,))], axis=-1)
    return jax.nn.softmax(full, axis=-1)[..., :-1]


def window_gqa_mixer(h, hc, w_qkv, w_o, sink, ang_row, ang_col, need_ctx):
    f32 = jnp.float32
    B, S, _ = h.shape
    qd = N_HEADS * HEAD_DIM
    kd = N_KV_HEADS * HEAD_DIM
    scale = HEAD_DIM ** -0.5

    def project(t):
        L = t.shape[1]
        p = (t @ w_qkv).astype(f32)
        q = p[..., :qd].reshape(B, L, N_KV_HEADS, GROUP, HEAD_DIM)
        k = p[..., qd:qd + kd].reshape(B, L, N_KV_HEADS, HEAD_DIM)
        v = p[..., qd + kd:].reshape(B, L, N_KV_HEADS, HEAD_DIM)
        return q, k, v

    q, k, v = project(h)
    qc, kc, vc = project(hc)
    q = rope_2d(q, ang_row, ang_col) * scale
    k = rope_2d(k, ang_row, ang_col)
    qc = qc * scale
    sink_b = sink.astype(f32).reshape(1, N_KV_HEADS, GROUP, 1, 1)

    pad = ((0, 0), (WINDOW, WINDOW), (0, 0), (0, 0))
    k_pad = jnp.pad(k, pad)
    v_pad = jnp.pad(v, pad)
    span = ATTN_BLOCK + 2 * WINDOW
    n_blocks = S // ATTN_BLOCK
    q_off = jnp.arange(ATTN_BLOCK)
    k_off = jnp.arange(span) - WINDOW

    def block(n):
        start = n * ATTN_BLOCK
        qb = lax.dynamic_slice_in_dim(q, start, ATTN_BLOCK, axis=1)
        kb = lax.dynamic_slice_in_dim(k_pad, start, span, axis=1)
        vb = lax.dynamic_slice_in_dim(v_pad, start, span, axis=1)
        ipos = start + q_off
        jpos = start + k_off
        valid = (jnp.abs(jpos[None, :] - ipos[:, None]) <= WINDOW) & (jpos[None, :] >= 0) & (jpos[None, :] < S)
        s_loc = jnp.where(valid, jnp.einsum('bqkgd,bskd->bkgqs', qb, kb), -jnp.inf)
        s_ctx = jnp.einsum('bqkgd,bckd->bkgqc', qb, kc)
        p = softmax_with_sink(jnp.concatenate([s_loc, s_ctx], axis=-1), sink_b)
        return (jnp.einsum('bkgqs,bskd->bqkgd', p[..., :span], vb)
                + jnp.einsum('bkgqc,bckd->bqkgd', p[..., span:], vc))

    o = lax.map(block, jnp.arange(n_blocks))
    o = jnp.moveaxis(o, 0, 1).reshape(B, S, qd)
    y = o.astype(h.dtype) @ w_o
    yc = None
    if need_ctx:
        s = jnp.einsum('bqkgd,bskd->bkgqs', qc, kc)
        p = softmax_with_sink(s, sink_b)
        oc = jnp.einsum('bkgqs,bskd->bqkgd', p, vc).reshape(B, hc.shape[1], qd)
        yc = oc.astype(hc.dtype) @ w_o
    return y, yc


def depthwise_conv(t, w, b):
    L = t.shape[1]
    tp = jnp.pad(t, ((0, 0), (CONV_LEFT, CONV_W - 1 - CONV_LEFT), (0, 0)))
    out = tp[:, 0:L] * w[0] + b
    for j in range(1, CONV_W):
        out = out + tp[:, j:j + L] * w[j]
    return out


def rglru_gates(u, wa, ba, wx, bx, lam):
    f32 = jnp.float32
    uf = u.astype(f32)
    ub = uf.reshape(uf.shape[:-1] + (LRU_BLOCKS, LRU_BLOCK_W))
    r = jax.nn.sigmoid(jnp.einsum('blnc,ncd->blnd', ub, wa.astype(f32)).reshape(uf.shape) + ba.astype(f32))
    i_g = jax.nn.sigmoid(jnp.einsum('blnc,ncd->blnd', ub, wx.astype(f32)).reshape(uf.shape) + bx.astype(f32))
    log_a = -LRU_C * r * jax.nn.softplus(-lam.astype(f32))
    a = jnp.exp(log_a)
    b = jnp.sqrt(-jnp.expm1(2.0 * log_a)) * (i_g * uf)
    return a, b


def _lin_combine(left, right):
    a_l, b_l = left
    a_r, b_r = right
    return a_l * a_r, a_r * b_l + b_r


def linear_recurrence(a, b, h0, reverse):
    if reverse:
        b = b.at[:, -1].add(a[:, -1] * h0)
    else:
        b = b.at[:, 0].add(a[:, 0] * h0)
    return lax.associative_scan(_lin_combine, (a, b), reverse=reverse, axis=1)[1]


def rglru_mixer(h, hc, w_in, conv_w, conv_b, wa, ba, wx, bx, lam, w_out, need_ctx):
    f32 = jnp.float32
    B = h.shape[0]
    proj = h @ w_in
    gate = jax.nn.gelu(proj[..., :D_RNN]).astype(f32)
    u = depthwise_conv(proj[..., D_RNN:], conv_w, conv_b)
    proj_c = hc @ w_in
    u_c = depthwise_conv(proj_c[..., D_RNN:], conv_w, conv_b)
    zero = jnp.zeros((B, D_RNN), f32)
    h_lat = []
    h_ctx = []
    for d, reverse in ((0, False), (1, True)):
        a_c, b_c = rglru_gates(u_c, wa[d], ba[d], wx[d], bx[d], lam[d])
        hs_c = linear_recurrence(a_c, b_c, zero, reverse)
        final_c = hs_c[:, 0] if reverse else hs_c[:, -1]
        a, b = rglru_gates(u, wa[d], ba[d], wx[d], bx[d], lam[d])
        h_lat.append(linear_recurrence(a, b, final_c, reverse))
        h_ctx.append(hs_c)
    y = ((h_lat[0] + h_lat[1]) * gate).astype(h.dtype) @ w_out
    yc = None
    if need_ctx:
        gate_c = jax.nn.gelu(proj_c[..., :D_RNN]).astype(f32)
        yc = ((h_ctx[0] + h_ctx[1]) * gate_c).astype(hc.dtype) @ w_out
    return y, yc


def moe_ffn(t, router_w, router_bias, w_gate, w_up, w_down, s_gate, s_up, s_down):
    f32 = jnp.float32
    T, D = t.shape
    scores = jax.nn.sigmoid(t.astype(f32) @ router_w.astype(f32))
    sel = scores + router_bias.astype(f32)
    per_group = N_EXPERTS // N_GROUPS
    group_score = lax.top_k(sel.reshape(T, N_GROUPS, per_group), 2)[0].sum(-1)
    _, top_groups = lax.top_k(group_score, TOPK_GROUPS)
    group_mask = jax.nn.one_hot(top_groups, N_GROUPS, dtype=f32).sum(1) > 0
    expert_mask = jnp.repeat(group_mask, per_group, axis=1)
    _, idx = lax.top_k(jnp.where(expert_mask, sel, -jnp.inf), TOP_K)
    w = jnp.take_along_axis(scores, idx, axis=1)
    w = w / jnp.sum(w, axis=-1, keepdims=True) * ROUTED_SCALE

    A = T * TOP_K
    e_flat = idx.reshape(A)
    tok_flat = jnp.arange(A, dtype=jnp.int32) // TOP_K
    order = jnp.argsort(e_flat)
    e_sorted = e_flat[order]
    tok_sorted = tok_flat[order]
    w_sorted = w.reshape(A)[order]
    counts = jnp.bincount(e_flat, length=N_EXPERTS)
    padded = (counts + MOE_BLOCK - 1) // MOE_BLOCK * MOE_BLOCK
    pad_end = jnp.cumsum(padded)
    pad_start = pad_end - padded
    start = jnp.cumsum(counts) - counts
    dest = pad_start[e_sorted] + jnp.arange(A, dtype=jnp.int32) - start[e_sorted]
    n_blocks = -(-A // MOE_BLOCK) + N_EXPERTS
    P = n_blocks * MOE_BLOCK
    tok_buf = jnp.full((P,), T, jnp.int32).at[dest].set(tok_sorted)
    w_buf = jnp.zeros((P,), f32).at[dest].set(w_sorted)
    block_expert = jnp.minimum(
        jnp.searchsorted(pad_end, jnp.arange(n_blocks, dtype=jnp.int32) * MOE_BLOCK, side='right'),
        N_EXPERTS - 1)
    t_pad = jnp.concatenate([t, jnp.zeros((1, D), t.dtype)], axis=0)

    def expert_block(args):
        tok_b, w_b, e = args
        yb = swiglu(t_pad[tok_b], w_gate[e], w_up[e], w_down[e])
        return yb.astype(f32) * w_b[:, None]

    y = lax.map(expert_block, (tok_buf.reshape(n_blocks, MOE_BLOCK),
                               w_buf.reshape(n_blocks, MOE_BLOCK), block_expert))
    routed = jnp.zeros((T + 1, D), f32).at[tok_buf].add(y.reshape(P, D))[:T]
    shared = swiglu(t, s_gate, s_up, s_down).astype(f32)
    return (routed + shared).astype(t.dtype)


def setup_inputs(seed: int = 0) -> dict:
    key = jax.random.key(seed)
    ks = jax.random.split(key, 32)
    f32 = jnp.float32

    def nrm(k, shape, scale):
        return jax.random.normal(k, shape, f32) * scale

    qkv_cols = (N_HEADS + 2 * N_KV_HEADS) * HEAD_DIM
    u = jax.random.uniform(ks[18], (N_LRU_LAYERS, 2, D_RNN), f32, 0.9, 0.999)
    return {
        'x': nrm(ks[0], (BATCH, SEQ, D_MODEL), 1.0),
        'c': nrm(ks[1], (BATCH, D_MODEL), 1.0),
        'ctx': nrm(ks[2], (BATCH, CTX_LEN, D_MODEL), 1.0),
        'c_ctx': nrm(ks[3], (D_MODEL,), 1.0),
        'ada_w': nrm(ks[4], (DEPTH, D_MODEL, 6 * D_MODEL), 0.5 * D_MODEL ** -0.5),
        'ada_b': nrm(ks[5], (DEPTH, 6 * D_MODEL), 0.02),
        'norm1_g': 1.0 + nrm(ks[6], (DEPTH, D_MODEL), 0.05),
        'norm2_g': 1.0 + nrm(ks[7], (DEPTH, D_MODEL), 0.05),
        'attn_w_qkv': nrm(ks[8], (N_ATTN_LAYERS, D_MODEL, qkv_cols), D_MODEL ** -0.5),
        'attn_w_o': nrm(ks[9], (N_ATTN_LAYERS, N_HEADS * HEAD_DIM, D_MODEL), (N_HEADS * HEAD_DIM) ** -0.5),
        'attn_sink': nrm(ks[10], (N_ATTN_LAYERS, N_HEADS), 0.5),
        'lru_w_in': nrm(ks[11], (N_LRU_LAYERS, D_MODEL, 2 * D_RNN), D_MODEL ** -0.5),
        'lru_conv_w': nrm(ks[12], (N_LRU_LAYERS, CONV_W, D_RNN), CONV_W ** -0.5),
        'lru_conv_b': nrm(ks[13], (N_LRU_LAYERS, D_RNN), 0.02),
        'lru_wa': nrm(ks[14], (N_LRU_LAYERS, 2, LRU_BLOCKS, LRU_BLOCK_W, LRU_BLOCK_W), LRU_BLOCK_W ** -0.5),
        'lru_ba': nrm(ks[15], (N_LRU_LAYERS, 2, D_RNN), 0.02),
        'lru_wx': nrm(ks[16], (N_LRU_LAYERS, 2, LRU_BLOCKS, LRU_BLOCK_W, LRU_BLOCK_W), LRU_BLOCK_W ** -0.5),
        'lru_bx': nrm(ks[17], (N_LRU_LAYERS, 2, D_RNN), 0.02),
        'lru_lam': jnp.log(u) - jnp.log1p(-u),
        'lru_w_out': nrm(ks[19], (N_LRU_LAYERS, D_RNN, D_MODEL), D_RNN ** -0.5),
        'moe_router_w': nrm(ks[20], (DEPTH, D_MODEL, N_EXPERTS), D_MODEL ** -0.5),
        'moe_router_bias': nrm(ks[21], (DEPTH, N_EXPERTS), 0.01),
        'moe_w_gate': nrm(ks[22], (DEPTH, N_EXPERTS, D_MODEL, D_EXPERT), D_MODEL ** -0.5),
        'moe_w_up': nrm(ks[23], (DEPTH, N_EXPERTS, D_MODEL, D_EXPERT), D_MODEL ** -0.5),
        'moe_w_down': nrm(ks[24], (DEPTH, N_EXPERTS, D_EXPERT, D_MODEL), D_EXPERT ** -0.5),
        'shared_w_gate': nrm(ks[25], (DEPTH, D_MODEL, D_EXPERT), D_MODEL ** -0.5),
        'shared_w_up': nrm(ks[26], (DEPTH, D_MODEL, D_EXPERT), D_MODEL ** -0.5),
        'shared_w_down': nrm(ks[27], (DEPTH, D_EXPERT, D_MODEL), D_EXPERT ** -0.5),
        'final_g': 1.0 + nrm(ks[28], (D_MODEL,), 0.05),
    }


def reference(x, c, ctx, c_ctx, ada_w, ada_b, norm1_g, norm2_g, attn_w_qkv, attn_w_o, attn_sink,
              lru_w_in, lru_conv_w, lru_conv_b, lru_wa, lru_ba, lru_wx, lru_bx, lru_lam, lru_w_out,
              moe_router_w, moe_router_bias, moe_w_gate, moe_w_up, moe_w_down,
              shared_w_gate, shared_w_up, shared_w_down, final_g):
    B, S, D = x.shape
    C = ctx.shape[1]
    ang_row, ang_col = rope_angles(S)
    for i in range(DEPTH):
        mod = jax.nn.silu(c) @ ada_w[i] + ada_b[i]
        mod_c = jax.nn.silu(c_ctx) @ ada_w[i] + ada_b[i]
        sh1, sc1, g1, sh2, sc2, g2 = [m[:, None, :] for m in jnp.split(mod, 6, axis=-1)]
        csh1, csc1, cg1, csh2, csc2, cg2 = jnp.split(mod_c, 6, axis=-1)
        need_ctx = i < DEPTH - 1

        h = modulate(rms_norm(x, norm1_g[i]), sh1, sc1)
        hc = modulate(rms_norm(ctx, norm1_g[i]), csh1, csc1)
        j = i // N_MIXERS
        if i % N_MIXERS == 0:
            y, yc = window_gqa_mixer(h, hc, attn_w_qkv[j], attn_w_o[j], attn_sink[j], ang_row, ang_col, need_ctx)
        else:
            y, yc = rglru_mixer(h, hc, lru_w_in[j], lru_conv_w[j], lru_conv_b[j], lru_wa[j], lru_ba[j],
                                lru_wx[j], lru_bx[j], lru_lam[j], lru_w_out[j], need_ctx)
        x = x + g1 * y
        h2 = modulate(rms_norm(x, norm2_g[i]), sh2, sc2)
        moe_args = (moe_router_w[i], moe_router_bias[i], moe_w_gate[i], moe_w_up[i], moe_w_down[i],
                    shared_w_gate[i], shared_w_up[i], shared_w_down[i])
        if need_ctx:
            ctx = ctx + cg1 * yc
            h2c = modulate(rms_norm(ctx, norm2_g[i]), csh2, csc2)
            f = moe_ffn(jnp.concatenate([h2.reshape(B * S, D), h2c.reshape(B * C, D)], axis=0), *moe_args)
            x = x + g2 * f[:B * S].reshape(B, S, D)
            ctx = ctx + cg2 * f[B * S:].reshape(B, C, D)
        else:
            f = moe_ffn(h2.reshape(B * S, D), *moe_args)
            x = x + g2 * f.reshape(B, S, D)
    return rms_norm(x, final_g)
```

```python
import functools
import math

import jax
import jax.numpy as jnp
from jax import lax
from jax.experimental import pallas as pl
from jax.experimental.pallas import tpu as pltpu

D_MODEL = 1024
DEPTH = 2
GRID_W = 64
HEAD_DIM = 64
N_HEADS = 16
N_KV_HEADS = 4
GROUP = N_HEADS // N_KV_HEADS
WINDOW = 128
ATTN_BLOCK = 128
ROPE_THETA = 10000.0
D_RNN = 1280
LRU_BLOCKS = 16
LRU_BLOCK_W = D_RNN // LRU_BLOCKS
CONV_W = 4
CONV_LEFT = 2
LRU_C = 8.0
N_EXPERTS = 64
TOP_K = 8
N_GROUPS = 8
TOPK_GROUPS = 4
D_EXPERT = 256
ROUTED_SCALE = 2.5
RMS_EPS = 1e-6

F32 = jnp.float32
BF16 = jnp.bfloat16
I32 = jnp.int32
HIGHEST = lax.Precision.HIGHEST

TM = 256
CHUNK = 16
BM = 256
CPB = BM // CHUNK
RCAP = ((TOP_K * TM + (CHUNK - 1) * N_EXPERTS + TM - 1) // TM) * TM
NCH_CAP = RCAP // CHUNK
VMEM_LIMIT = 56 * 1024 * 1024


def _cparams(sem):
    return pltpu.CompilerParams(dimension_semantics=sem, vmem_limit_bytes=VMEM_LIMIT)


def _silu(x):
    return x * jax.nn.sigmoid(x)


def _dot(a, b):
    return jnp.dot(a, b, preferred_element_type=F32)


def _dot_nt(a, b, precision=None):
    return lax.dot_general(a, b, (((1,), (1,)), ((), ())), precision=precision,
                           preferred_element_type=F32)


def _adaln_kernel(c_ref, w_ref, b_ref, o_ref):
    s = _silu(c_ref[...])
    o_ref[0] = jnp.dot(s, w_ref[0], precision=HIGHEST, preferred_element_type=F32) + b_ref[0]


def _adaln(cvec, ada_w, ada_b):
    nb = 1536
    d6 = 6 * D_MODEL
    return pl.pallas_call(
        _adaln_kernel,
        grid=(DEPTH, d6 // nb),
        in_specs=[pl.BlockSpec((8, D_MODEL), lambda l, j: (0, 0)),
                  pl.BlockSpec((1, D_MODEL, nb), lambda l, j: (l, 0, j)),
                  pl.BlockSpec((1, 1, nb), lambda l, j: (l, 0, j))],
        out_specs=pl.BlockSpec((1, 8, nb), lambda l, j: (l, 0, j)),
        out_shape=jax.ShapeDtypeStruct((DEPTH, 8, d6), F32),
        compiler_params=_cparams(("arbitrary", "arbitrary")),
        name="adaln",
    )(cvec, ada_w, ada_b.reshape(DEPTH, 1, d6))


def _norm_mod(x, g, shift, scale):
    xn = x * lax.rsqrt(jnp.mean(x * x, axis=-1, keepdims=True) + RMS_EPS) * g
    return xn * (1.0 + scale) + shift


def _qkv_kernel(x_ref, mod_ref, g_ref, w_ref, cos_ref, sin_ref, q_ref, k_ref, v_ref):
    m = mod_ref[0]
    h = _norm_mod(x_ref[...], g_ref[...], m[0:1], m[1:2])
    p = _dot(h.astype(BF16), w_ref[...])
    cos = cos_ref[...]
    sin = sin_ref[...]
    lane = lax.broadcasted_iota(I32, (TM, 128), 1)
    second = (lane & 16) != 0
    scale = HEAD_DIM ** -0.5
    nq = N_HEADS * HEAD_DIM // 128
    nk = 2 * N_KV_HEADS * HEAD_DIM // 128
    for c in range(nq + nk):
        blk = p[:, 128 * c:128 * (c + 1)]
        partner = jnp.where(second, pltpu.roll(blk, 16, 1), pltpu.roll(blk, 112, 1))
        r = blk * cos + partner * sin
        if c < nq:
            q_ref[:, 128 * c:128 * (c + 1)] = (r * scale).astype(BF16)
        else:
            k_ref[:, 128 * (c - nq):128 * (c - nq + 1)] = r.astype(BF16)
    v_ref[...] = p[:, 128 * (nq + nk):].astype(BF16)


def _qkv(xall, mod, g, w, cos, sin, n_xt, tpb, nb):
    t = xall.shape[0]
    nt = t // TM
    kw = 2 * N_KV_HEADS * HEAD_DIM
    row = lambda i: jnp.where(i < n_xt, i // tpb, nb)
    pos = lambda i: jnp.where(i < n_xt, i % tpb, tpb)
    return pl.pallas_call(
        _qkv_kernel,
        grid=(nt,),
        in_specs=[pl.BlockSpec((TM, D_MODEL), lambda i: (i, 0)),
                  pl.BlockSpec((1, 6, D_MODEL), lambda i: (row(i), 0, 0)),
                  pl.BlockSpec((1, D_MODEL), lambda i: (0, 0)),
                  pl.BlockSpec(w.shape, lambda i: (0, 0)),
                  pl.BlockSpec((TM, 128), lambda i: (pos(i), 0)),
                  pl.BlockSpec((TM, 128), lambda i: (pos(i), 0))],
        out_specs=[pl.BlockSpec((TM, D_MODEL), lambda i: (i, 0)),
                   pl.BlockSpec((TM, kw), lambda i: (i, 0)),
                   pl.BlockSpec((TM, kw), lambda i: (i, 0))],
        out_shape=[jax.ShapeDtypeStruct((t, D_MODEL), BF16),
                   jax.ShapeDtypeStruct((t, kw), BF16),
                   jax.ShapeDtypeStruct((t, kw), BF16)],
        compiler_params=_cparams(("arbitrary",)),
        name="qkv_rope",
    )(xall, mod, g, w, cos, sin)


def _rope_tables(seq):
    s = jnp.arange(seq)
    row = (s // GRID_W).astype(F32)
    col = (s % GRID_W).astype(F32)
    n_freq = HEAD_DIM // 4
    inv = jnp.exp(-math.log(ROPE_THETA) * jnp.arange(n_freq, dtype=F32) / n_freq)
    ar = row[:, None] * inv
    ac = col[:, None] * inv
    cos = jnp.concatenate([jnp.cos(ar), jnp.cos(ar), jnp.cos(ac), jnp.cos(ac)], axis=-1)
    sin = jnp.concatenate([-jnp.sin(ar), jnp.sin(ar), -jnp.sin(ac), jnp.sin(ac)], axis=-1)
    cos = jnp.concatenate([jnp.tile(cos, (1, 2)), jnp.ones((TM, 128), F32)], axis=0)
    sin = jnp.concatenate([jnp.tile(sin, (1, 2)), jnp.zeros((TM, 128), F32)], axis=0)
    return cos, sin


def _attn_kernel(nqb, sink_ref, q_ref, kp_ref, kc_ref, kn_ref, kx_ref,
                 vp_ref, vc_ref, vn_ref, vx_ref, o_ref):
    n = pl.program_id(1)
    qb = ATTN_BLOCK
    nloc = 3 * qb
    nkeys = nloc + kx_ref.shape[0]
    rows = GROUP * qb
    r = lax.broadcasted_iota(I32, (rows, nkeys), 0) % qb
    c = lax.broadcasted_iota(I32, (rows, nkeys), 1)
    d = c - r
    ok = (c < nloc) & (d >= 0) & (d <= 2 * WINDOW) & (n < nqb)
    ok = ok & ((n > 0) | (c >= qb)) & ((n < nqb - 1) | (c < 2 * qb))
    valid = ok | (c >= nloc)
    lane = lax.broadcasted_iota(I32, (qb, 128), 1)
    rsub = lax.broadcasted_iota(I32, (rows, 1), 0) // qb
    for j in range(N_KV_HEADS):
        ks = slice(128 * j, 128 * (j + 1))
        kall = jnp.concatenate([kp_ref[:, ks], kc_ref[:, ks], kn_ref[:, ks], kx_ref[:, ks]], axis=0)
        vall = jnp.concatenate([vp_ref[:, ks], vc_ref[:, ks], vn_ref[:, ks], vx_ref[:, ks]], axis=0)
        qs = []
        sink = jnp.zeros((rows, 1), F32)
        for g in range(GROUP):
            pair, half = divmod(g, 2)
            qp = q_ref[:, 256 * j + 128 * pair:256 * j + 128 * (pair + 1)]
            keep = (lane >= 64) if half else (lane < 64)
            qs.append(jnp.where(keep, qp, jnp.zeros_like(qp)))
            sink = jnp.where(rsub == g, sink_ref[GROUP * j + g], sink)
        qst = jnp.concatenate(qs, axis=0)
        s = _dot_nt(qst, kall)
        s = jnp.where(valid, s, -jnp.inf)
        m = jnp.maximum(jnp.max(s, axis=-1, keepdims=True), sink)
        p = jnp.exp(s - m)
        den = jnp.sum(p, axis=-1, keepdims=True) + jnp.exp(sink - m)
        o = _dot(p.astype(BF16), vall) * (1.0 / den)
        for pair in range(GROUP // 2):
            o0 = o[(2 * pair) * qb:(2 * pair + 1) * qb]
            o1 = o[(2 * pair + 1) * qb:(2 * pair + 2) * qb]
            o_ref[:, 256 * j + 128 * pair:256 * j + 128 * (pair + 1)] = (
                jnp.where(lane < 64, o0, o1).astype(BF16))


def _attention(q, k, v, sink, nb, seq, ctx_len):
    t = q.shape[0]
    qb = ATTN_BLOCK
    nqb = seq // qb
    ncb = ctx_len // qb
    xq = nb * nqb
    kw = k.shape[1]
    qrow = lambda b, n: jnp.where(n < nqb, b * nqb + n, xq + b * ncb + (n - nqb))
    kprev = lambda b, n, s: (b * nqb + jnp.clip(n - 1, 0, nqb - 1), 0)
    kcur = lambda b, n, s: (b * nqb + jnp.minimum(n, nqb - 1), 0)
    knext = lambda b, n, s: (b * nqb + jnp.minimum(n + 1, nqb - 1), 0)
    kctx = lambda b, n, s: (nb * seq // ctx_len + b, 0)
    kspec = lambda f: pl.BlockSpec((qb, kw), f)
    xspec = pl.BlockSpec((ctx_len, kw), kctx)
    return pl.pallas_call(
        functools.partial(_attn_kernel, nqb),
        grid_spec=pltpu.PrefetchScalarGridSpec(
            num_scalar_prefetch=1,
            grid=(nb, nqb + ncb),
            in_specs=[pl.BlockSpec((qb, D_MODEL), lambda b, n, s: (qrow(b, n), 0)),
                      kspec(kprev), kspec(kcur), kspec(knext), xspec,
                      kspec(kprev), kspec(kcur), kspec(knext), xspec],
            out_specs=pl.BlockSpec((qb, D_MODEL), lambda b, n, s: (qrow(b, n), 0)),
        ),
        out_shape=jax.ShapeDtypeStruct((t, D_MODEL), BF16),
        compiler_params=_cparams(("arbitrary", "arbitrary")),
        name="window_attn",
    )(sink, q, k, k, k, k, v, v, v, v)


def _post_kernel(m_ref, w_ref, x_ref, mod_ref, g_ref, rw_ref, rb_ref,
                 x1_ref, h2_ref, pos_ref, gate_ref, cnt_ref):
    md = mod_ref[0]
    y = _dot(m_ref[...], w_ref[...])
    x1 = x_ref[...] + md[2:3] * y
    x1_ref[...] = x1
    h2 = _norm_mod(x1, g_ref[...], md[3:4], md[4:5])
    h2_ref[...] = h2.astype(BF16)

    ne = N_EXPERTS
    per = ne // N_GROUPS
    logit = _dot_nt(rw_ref[...], h2, precision=HIGHEST)
    sc = jax.nn.sigmoid(logit)
    sel = sc + rb_ref[...]
    sub8 = lax.broadcasted_iota(I32, (per, TM), 0)
    gs = jnp.zeros((N_GROUPS, TM), F32)
    gi = lax.broadcasted_iota(I32, (N_GROUPS, TM), 0)
    for g in range(N_GROUPS):
        blk = sel[per * g:per * (g + 1)]
        m1 = jnp.max(blk, axis=0, keepdims=True)
        i1 = jnp.min(jnp.where(blk == m1, sub8, per), axis=0, keepdims=True)
        m2 = jnp.max(jnp.where(sub8 == i1, -jnp.inf, blk), axis=0, keepdims=True)
        gs = jnp.where(gi == g, m1 + m2, gs)
    grank = jnp.zeros((N_GROUPS, TM), F32)
    for g in range(N_GROUPS):
        v = gs[g:g + 1]
        grank = grank + jnp.where(gi > g, jnp.where(v >= gs, 1.0, 0.0), jnp.where(v > gs, 1.0, 0.0))
    ei = lax.broadcasted_iota(I32, (ne, TM), 0)
    gsel = jnp.zeros((ne, TM), F32)
    for g in range(N_GROUPS):
        gsel = jnp.where(ei // per == g, grank[g:g + 1], gsel)
    selm = jnp.where(gsel < TOPK_GROUPS, sel, -jnp.inf)
    erank = jnp.zeros((ne, TM), F32)
    for e in range(ne):
        v = selm[e:e + 1]
        erank = erank + jnp.where(ei > e, jnp.where(v >= selm, 1.0, 0.0), jnp.where(v > selm, 1.0, 0.0))
    s8 = jnp.where(erank < TOP_K, 1.0, 0.0)
    ws = s8 * sc
    gate = ws / jnp.sum(ws, axis=0, keepdims=True) * ROUTED_SCALE

    s8b = s8.astype(BF16)
    er = lax.broadcasted_iota(I32, (ne, ne), 0)
    ec = lax.broadcasted_iota(I32, (ne, ne), 1)
    lower = jnp.where(ec < er, 1.0, 0.0).astype(BF16)
    tr = lax.broadcasted_iota(I32, (TM, TM), 0)
    tc = lax.broadcasted_iota(I32, (TM, TM), 1)
    upper = jnp.where(tr < tc, 1.0, 0.0).astype(BF16)
    slot = _dot(lower, s8b)
    rank = _dot(s8b, upper)
    cnt = jnp.sum(s8, axis=1, keepdims=True)
    pc = jnp.floor((cnt + (CHUNK - 1)) * (1.0 / CHUNK))
    pcb = jnp.broadcast_to(pc, (ne, 128))
    cnt_ref[0] = pcb
    loc = _dot(lower, pcb.astype(BF16))[:, 0:1] * CHUNK
    prow = loc + rank
    k8 = lax.broadcasted_iota(I32, (TOP_K, TM), 0)
    pos_t = jnp.zeros((TOP_K, TM), F32)
    gate_t = jnp.zeros((TOP_K, TM), F32)
    for k in range(TOP_K):
        mk = jnp.where(slot == k, s8, 0.0)
        pos_t = jnp.where(k8 == k, jnp.sum(mk * prow, axis=0, keepdims=True), pos_t)
        gate_t = jnp.where(k8 == k, jnp.sum(mk * gate, axis=0, keepdims=True), gate_t)
    pos_ref[...] = pos_t.astype(I32)
    gate_ref[...] = gate_t


def _post(m, w, xall, mod, g, rw_t, rb, nt, n_xt, tpb, nb):
    t = nt * TM
    kd = m.shape[1]
    row = lambda i: jnp.where(i < n_xt, i // tpb, nb)
    return pl.pallas_call(
        _post_kernel,
        grid=(nt,),
        in_specs=[pl.BlockSpec((TM, kd), lambda i: (i, 0)),
                  pl.BlockSpec(w.shape, lambda i: (0, 0)),
                  pl.BlockSpec((TM, D_MODEL), lambda i: (i, 0)),
                  pl.BlockSpec((1, 6, D_MODEL), lambda i: (row(i), 0, 0)),
                  pl.BlockSpec((1, D_MODEL), lambda i: (0, 0)),
                  pl.BlockSpec((N_EXPERTS, D_MODEL), lambda i: (0, 0)),
                  pl.BlockSpec((N_EXPERTS, 1), lambda i: (0, 0))],
        out_specs=[pl.BlockSpec((TM, D_MODEL), lambda i: (i, 0)),
                   pl.BlockSpec((TM, D_MODEL), lambda i: (i, 0)),
                   pl.BlockSpec((TOP_K, TM), lambda i: (0, i)),
                   pl.BlockSpec((TOP_K, TM), lambda i: (0, i)),
                   pl.BlockSpec((1, N_EXPERTS, 128), lambda i: (i, 0, 0))],
        out_shape=[jax.ShapeDtypeStruct((t, D_MODEL), F32),
                   jax.ShapeDtypeStruct((t, D_MODEL), BF16),
                   jax.ShapeDtypeStruct((TOP_K, t), I32),
                   jax.ShapeDtypeStruct((TOP_K, t), F32),
                   jax.ShapeDtypeStruct((nt, N_EXPERTS, 128), F32)],
        compiler_params=_cparams(("arbitrary",)),
        name="mix_out_route",
    )(m, w, xall, mod, g, rw_t, rb)


def _moe_tables(pc, nblk_cap):
    nt = pc.shape[0]
    seg = jnp.sum(pc, axis=0)
    segb = (seg + CPB - 1) // CPB
    seg_end_b = jnp.cumsum(segb)
    seg_start = (seg_end_b - segb) * CPB
    goff = seg_start[None, :] + jnp.cumsum(pc, axis=0) - pc
    loc_end = jnp.cumsum(pc, axis=1)
    loc = loc_end - pc
    nch = loc_end[:, -1]
    nrb = (nch * CHUNK + TM - 1) // TM
    j = jnp.arange(NCH_CAP, dtype=I32)
    eidx = jnp.sum((j[None, :, None] >= loc_end[:, None, :]).astype(I32), axis=-1)
    eidx = jnp.minimum(eidx, N_EXPERTS - 1)
    dst = (jnp.take_along_axis(goff, eidx, axis=1) + j[None, :]
           - jnp.take_along_axis(loc, eidx, axis=1))
    dst = jnp.where(j[None, :] < nch[:, None], dst, 0).astype(I32)
    tl = jnp.arange(CPB - 1, dtype=I32)
    tail_len = segb * CPB - seg
    tails = jnp.where(tl[None, :] < tail_len[:, None], (seg_start + seg)[:, None] + tl[None, :], -1)
    nused = seg_end_b[-1]
    blk = jnp.arange(nblk_cap, dtype=I32)
    bexp = jnp.sum((blk[:, None] >= seg_end_b[None, :]).astype(I32), axis=-1)
    last = jnp.sum((nused - 1 >= seg_end_b).astype(I32))
    bexp = jnp.where(blk < nused, bexp, last).astype(I32)
    return dict(nch=nch.astype(I32), nrb=nrb.astype(I32), dst=dst.reshape(-1),
                tails=tails.reshape(-1).astype(I32), nused=nused.reshape(1).astype(I32), bexp=bexp)


def _chunk_copy(src, dst, sem):
    return pltpu.make_async_copy(src, dst, sem)


def _dispatch_kernel(nt, nrb_ref, nch_ref, dst_ref, tails_ref, h2_ref, pos_ref, xs_hbm, buf, sem):
    i = pl.program_id(0)

    @pl.when(i < nt)
    def _():
        h2 = h2_ref[...]
        pos = pos_ref[...]
        riota = lax.broadcasted_iota(I32, (TM, TM), 0)

        def build(rb, carry):
            base = pl.multiple_of(rb * TM, TM)
            r = riota + base
            p = jnp.zeros((TM, TM), F32)
            for k in range(TOP_K):
                p = jnp.where(r == pos[k:k + 1], 1.0, p)
            buf[pl.ds(base, TM), :] = _dot(p.astype(BF16), h2).astype(BF16)
            return carry

        lax.fori_loop(0, nrb_ref[i], build, 0)
        nch = nch_ref[i]

        def issue(j, carry):
            d = dst_ref[i * NCH_CAP + j]
            _chunk_copy(buf.at[pl.ds(pl.multiple_of(j * CHUNK, CHUNK), CHUNK)],
                        xs_hbm.at[pl.ds(pl.multiple_of(d * CHUNK, CHUNK), CHUNK)], sem).start()
            return carry

        lax.fori_loop(0, nch, issue, 0)

        def drain(j, carry):
            _chunk_copy(buf.at[pl.ds(0, CHUNK)], xs_hbm.at[pl.ds(0, CHUNK)], sem).wait()
            return carry

        lax.fori_loop(0, nch, drain, 0)

    @pl.when(i == nt)
    def _():
        buf[pl.ds(0, CHUNK), :] = jnp.zeros((CHUNK, D_MODEL), BF16)
        ntail = N_EXPERTS * (CPB - 1)

        def issue(j, carry):
            d = tails_ref[j]

            @pl.when(d >= 0)
            def _():
                _chunk_copy(buf.at[pl.ds(0, CHUNK)],
                            xs_hbm.at[pl.ds(pl.multiple_of(d * CHUNK, CHUNK), CHUNK)], sem).start()
            return carry

        lax.fori_loop(0, ntail, issue, 0)

        def drain(j, carry):
            @pl.when(tails_ref[j] >= 0)
            def _():
                _chunk_copy(buf.at[pl.ds(0, CHUNK)], xs_hbm.at[pl.ds(0, CHUNK)], sem).wait()
            return carry

        lax.fori_loop(0, ntail, drain, 0)


def _dispatch(tb, h2, pos_t, nt, nblk_cap):
    last = nt - 1
    return pl.pallas_call(
        functools.partial(_dispatch_kernel, nt),
        grid_spec=pltpu.PrefetchScalarGridSpec(
            num_scalar_prefetch=4,
            grid=(nt + 1,),
            in_specs=[pl.BlockSpec((TM, D_MODEL), lambda i, *_: (jnp.minimum(i, last), 0)),
                      pl.BlockSpec((TOP_K, TM), lambda i, *_: (0, jnp.minimum(i, last)))],
            out_specs=pl.BlockSpec(memory_space=pl.ANY),
            scratch_shapes=[pltpu.VMEM((RCAP, D_MODEL), BF16), pltpu.SemaphoreType.DMA],
        ),
        out_shape=jax.ShapeDtypeStruct((nblk_cap * BM, D_MODEL), BF16),
        compiler_params=_cparams(("arbitrary",)),
        name="moe_dispatch",
    )(tb["nrb"], tb["nch"], tb["dst"], tb["tails"], h2, pos_t)


def _expert_kernel(bexp_ref, nused_ref, x_ref, wg_ref, wu_ref, wd_ref, y_ref, wgu, wdn):
    i = pl.program_id(0)
    e = bexp_ref[i]
    prev = bexp_ref[jnp.maximum(i - 1, 0)]

    @pl.when((i == 0) | (e != prev))
    def _():
        wgu[:, :D_EXPERT] = wg_ref[0].astype(BF16)
        wgu[:, D_EXPERT:] = wu_ref[0].astype(BF16)
        wdn[...] = wd_ref[0].astype(BF16)

    @pl.when(i < nused_ref[0])
    def _():
        gu = _dot(x_ref[...], wgu[...])
        h = _silu(gu[:, :D_EXPERT]) * gu[:, D_EXPERT:]
        y_ref[...] = _dot(h.astype(BF16), wdn[...]).astype(BF16)


def _experts(tb, xs, w_gate, w_up, w_down, nblk_cap):
    xmap = lambda i, be, nu: (jnp.minimum(i, nu[0] - 1), 0)
    return pl.pallas_call(
        _expert_kernel,
        grid_spec=pltpu.PrefetchScalarGridSpec(
            num_scalar_prefetch=2,
            grid=(nblk_cap,),
            in_specs=[pl.BlockSpec((BM, D_MODEL), xmap),
                      pl.BlockSpec((1, D_MODEL, D_EXPERT), lambda i, be, nu: (be[i], 0, 0)),
                      pl.BlockSpec((1, D_MODEL, D_EXPERT), lambda i, be, nu: (be[i], 0, 0)),
                      pl.BlockSpec((1, D_EXPERT, D_MODEL), lambda i, be, nu: (be[i], 0, 0))],
            out_specs=pl.BlockSpec((BM, D_MODEL), xmap),
            scratch_shapes=[pltpu.VMEM((D_MODEL, 2 * D_EXPERT), BF16),
                            pltpu.VMEM((D_EXPERT, D_MODEL), BF16)],
        ),
        out_shape=jax.ShapeDtypeStruct((nblk_cap * BM, D_MODEL), BF16),
        compiler_params=_cparams(("arbitrary",)),
        name="moe_experts",
    )(tb["bexp"], tb["nused"], xs, w_gate, w_up, w_down)


def _combine_kernel(final, nrb_ref, nch_ref, src_ref, ys_hbm, pos_ref, gate_ref, h2_ref, x1_ref,
                    mod_ref, sgu_ref, sdn_ref, fg_ref, o_ref, buf, acc, sem):
    i = pl.program_id(0)

    @pl.when(i == 0)
    def _():
        buf[...] = jnp.zeros(buf.shape, BF16)

    nch = nch_ref[i]

    def issue(j, carry):
        s = src_ref[i * NCH_CAP + j]
        _chunk_copy(ys_hbm.at[pl.ds(pl.multiple_of(s * CHUNK, CHUNK), CHUNK)],
                    buf.at[pl.ds(pl.multiple_of(j * CHUNK, CHUNK), CHUNK)], sem).start()
        return carry

    lax.fori_loop(0, nch, issue, 0)

    gu = _dot(h2_ref[...], sgu_ref[...])
    hs = _silu(gu[:, :D_EXPERT]) * gu[:, D_EXPERT:]
    acc[...] = _dot(hs.astype(BF16), sdn_ref[...])

    def drain(j, carry):
        _chunk_copy(ys_hbm.at[pl.ds(0, CHUNK)], buf.at[pl.ds(0, CHUNK)], sem).wait()
        return carry

    lax.fori_loop(0, nch, drain, 0)

    pos = pos_ref[...]
    gate = gate_ref[...]
    ciota = lax.broadcasted_iota(I32, (TM, TM), 1)
    posb = [jnp.broadcast_to(pos[:, k:k + 1], (TM, TM)) for k in range(TOP_K)]
    gateb = [jnp.broadcast_to(gate[:, k:k + 1], (TM, TM)) for k in range(TOP_K)]

    def fold(rb, carry):
        base = pl.multiple_of(rb * TM, TM)
        c = ciota + base
        p = jnp.zeros((TM, TM), F32)
        for k in range(TOP_K):
            p = jnp.where(c == posb[k], gateb[k], p)
        acc[...] += _dot(p.astype(BF16), buf[pl.ds(base, TM), :])
        return carry

    lax.fori_loop(0, nrb_ref[i], fold, 0)
    x2 = x1_ref[...] + mod_ref[0][5:6] * acc[...]
    if final:
        x2 = x2 * lax.rsqrt(jnp.mean(x2 * x2, axis=-1, keepdims=True) + RMS_EPS) * fg_ref[...]
    o_ref[...] = x2


def _combine(tb, ys, pos, gate, h2, x1, mod, sgu, sdn, fg, nt, n_xt, tpb, nb, final):
    t = nt * TM
    row = lambda i, *_: (jnp.where(i < n_xt, i // tpb, nb), 0, 0)
    tile = lambda i, *_: (i, 0)
    const = lambda i, *_: (0, 0)
    return pl.pallas_call(
        functools.partial(_combine_kernel, final),
        grid_spec=pltpu.PrefetchScalarGridSpec(
            num_scalar_prefetch=3,
            grid=(nt,),
            in_specs=[pl.BlockSpec(memory_space=pl.ANY),
                      pl.BlockSpec((TM, TOP_K), tile),
                      pl.BlockSpec((TM, TOP_K), tile),
                      pl.BlockSpec((TM, D_MODEL), tile),
                      pl.BlockSpec((TM, D_MODEL), tile),
                      pl.BlockSpec((1, 6, D_MODEL), row),
                      pl.BlockSpec(sgu.shape, const),
                      pl.BlockSpec(sdn.shape, const),
                      pl.BlockSpec((1, D_MODEL), const)],
            out_specs=pl.BlockSpec((TM, D_MODEL), tile),
            scratch_shapes=[pltpu.VMEM((RCAP, D_MODEL), BF16),
                            pltpu.VMEM((TM, D_MODEL), F32),
                            pltpu.SemaphoreType.DMA],
        ),
        out_shape=jax.ShapeDtypeStruct((t, D_MODEL), F32),
        compiler_params=_cparams(("arbitrary",)),
        name="moe_combine",
    )(tb["nrb"], tb["nch"], tb["dst"], ys, pos, gate, h2, x1, mod, sgu, sdn, fg)


def _moe(h2, pos_t, gate_t, cnt, x1, mod, w_gate, w_up, w_down, sgu, sdn, fg, nt, n_xt, tpb, nb, final):
    nblk_cap = (TOP_K * nt * TM + (CHUNK - 1) * N_EXPERTS * nt) // BM + N_EXPERTS
    tb = _moe_tables(cnt[:, :, 0].astype(I32), nblk_cap)
    xs = _dispatch(tb, h2, pos_t, nt, nblk_cap)
    ys = _experts(tb, xs, w_gate, w_up, w_down, nblk_cap)
    return _combine(tb, ys, pos_t.T, gate_t.T, h2, x1, mod, sgu, sdn, fg, nt, n_xt, tpb, nb, final)


def _lru_in_kernel(x_ref, mod_ref, g_ref, w_ref, gate_ref, u_ref):
    m = mod_ref[0]
    h = _norm_mod(x_ref[...], g_ref[...], m[0:1], m[1:2])
    p = _dot(h.astype(BF16), w_ref[...])
    z = p[:, :D_RNN]
    cdf = 0.5 * (1.0 + jnp.tanh(math.sqrt(2.0 / math.pi) * (z + 0.044715 * (z * z * z))))
    gate_ref[...] = (z * cdf).astype(BF16)
    u_ref[...] = p[:, D_RNN:]


def _lru_in(xall, mod, g, w, n_xt, tpb, nb):
    t = xall.shape[0]
    nt = t // TM
    row = lambda i: jnp.where(i < n_xt, i // tpb, nb)
    return pl.pallas_call(
        _lru_in_kernel,
        grid=(nt,),
        in_specs=[pl.BlockSpec((TM, D_MODEL), lambda i: (i, 0)),
                  pl.BlockSpec((1, 6, D_MODEL), lambda i: (row(i), 0, 0)),
                  pl.BlockSpec((1, D_MODEL), lambda i: (0, 0)),
                  pl.BlockSpec(w.shape, lambda i: (0, 0))],
        out_specs=[pl.BlockSpec((TM, D_RNN), lambda i: (i, 0)),
                   pl.BlockSpec((TM, D_RNN), lambda i: (i, 0))],
        out_shape=[jax.ShapeDtypeStruct((t, D_RNN), BF16),
                   jax.ShapeDtypeStruct((t, D_RNN), F32)],
        compiler_params=_cparams(("arbitrary",)),
        name="lru_in",
    )(xall, mod, g, w)


def _expm1(x):
    e = jnp.exp(x)
    em = e - 1.0
    return jnp.where(em == 0.0, x, jnp.where(em == -1.0, -1.0, em * x / jnp.log(e)))


def _lru_sweep_kernel(rev, mix, ns, u_ref, up_ref, un_ref, cw_ref, cb_ref, wa_ref, wx_ref,
                      ba_ref, bx_ref, lam_ref, h0_ref, *rest):
    if mix:
        hf_ref, gate_ref, o_ref, a_s, b_s, carry = rest
    else:
        o_ref, a_s, b_s, carry = rest
    s = pl.program_id(1)
    ss = (ns - 1 - s) if rev else s

    @pl.when(s == 0)
    def _():
        carry[...] = jnp.broadcast_to(h0_ref[0], (8, D_RNN))

    prev = jnp.where(ss > 0, up_ref[...], 0.0)
    nxt = jnp.where(ss < ns - 1, un_ref[...], 0.0)
    ext = jnp.concatenate([prev, u_ref[...], nxt], axis=0)
    n_ext = TM + 16
    cw = cw_ref[...]
    u = cb_ref[...] + cw[0:1] * pltpu.roll(ext, CONV_LEFT, 0)[8:8 + TM]
    u = u + cw[1:2] * pltpu.roll(ext, 1, 0)[8:8 + TM]
    u = u + cw[2:3] * ext[8:8 + TM]
    u = u + cw[3:4] * pltpu.roll(ext, n_ext - 1, 0)[8:8 + TM]

    ub = u.astype(BF16)
    r = jax.nn.sigmoid(_dot(ub, wa_ref[...]) + ba_ref[...])
    ig = jax.nn.sigmoid(_dot(ub, wx_ref[...]) + bx_ref[...])
    nl = -lam_ref[...]
    softplus = jnp.maximum(nl, 0.0) + jnp.log1p(jnp.exp(-jnp.abs(nl)))
    log_a = (-LRU_C) * r * softplus
    a_s[...] = jnp.exp(log_a)
    b_s[...] = jnp.sqrt(-_expm1(2.0 * log_a)) * (ig * u)

    sub = lax.broadcasted_iota(I32, (8, D_RNN), 0)
    h = carry[...]
    ng = TM // 8
    for gi in range(ng):
        r0 = 8 * ((ng - 1 - gi) if rev else gi)
        a = a_s[r0:r0 + 8, :]
        b = b_s[r0:r0 + 8, :]
        for sh in (1, 2, 4):
            amt = (8 - sh) if rev else sh
            keep = (sub < 8 - sh) if rev else (sub >= sh)
            a_sh = pltpu.roll(a, amt, 0)
            b_sh = pltpu.roll(b, amt, 0)
            b = jnp.where(keep, a * b_sh + b, b)
            a = jnp.where(keep, a * a_sh, a)
        hg = a * h + b
        edge = hg[0:1] if rev else hg[7:8]
        h = jnp.broadcast_to(edge, (8, D_RNN))
        if mix:
            o_ref[r0:r0 + 8, :] = ((hf_ref[r0:r0 + 8, :] + hg)
                                   * gate_ref[r0:r0 + 8, :].astype(F32)).astype(BF16)
        else:
            o_ref[r0:r0 + 8, :] = hg
    carry[...] = h


def _lru_sweep(u_all, row0, nb, seq, cw, cb, wa, wx, ba, bx, lam, h0, rev, hf=None, gate=None):
    ns = seq // TM
    blk0 = row0 // TM
    mix = hf is not None
    sidx = (lambda s: ns - 1 - s) if rev else (lambda s: s)
    cur = lambda b, s: (blk0 + b * ns + sidx(s), 0)
    out = lambda b, s: (b * ns + sidx(s), 0)
    h8 = TM // 8
    nblk8 = u_all.shape[0] // 8
    prv = lambda b, s: (jnp.maximum((blk0 + b * ns + sidx(s)) * h8 - 1, 0), 0)
    nxt = lambda b, s: (jnp.minimum((blk0 + b * ns + sidx(s) + 1) * h8, nblk8 - 1), 0)
    vec = lambda b, s: (0, 0)
    in_specs = [pl.BlockSpec((TM, D_RNN), cur),
                pl.BlockSpec((8, D_RNN), prv),
                pl.BlockSpec((8, D_RNN), nxt),
                pl.BlockSpec((CONV_W, D_RNN), vec),
                pl.BlockSpec((1, D_RNN), vec),
                pl.BlockSpec((D_RNN, D_RNN), vec),
                pl.BlockSpec((D_RNN, D_RNN), vec),
                pl.BlockSpec((1, D_RNN), vec),
                pl.BlockSpec((1, D_RNN), vec),
                pl.BlockSpec((1, D_RNN), vec),
                pl.BlockSpec((1, 1, D_RNN), lambda b, s: (b, 0, 0))]
    args = [u_all, u_all, u_all, cw, cb, wa, wx, ba, bx, lam, h0]
    if mix:
        in_specs += [pl.BlockSpec((TM, D_RNN), out), pl.BlockSpec((TM, D_RNN), cur)]
        args += [hf, gate]
    return pl.pallas_call(
        functools.partial(_lru_sweep_kernel, rev, mix, ns),
        grid=(nb, ns),
        in_specs=in_specs,
        out_specs=pl.BlockSpec((TM, D_RNN), out),
        out_shape=jax.ShapeDtypeStruct((nb * seq, D_RNN), BF16 if mix else F32),
        scratch_shapes=[pltpu.VMEM((TM, D_RNN), F32), pltpu.VMEM((TM, D_RNN), F32),
                        pltpu.VMEM((8, D_RNN), F32)],
        compiler_params=_cparams(("arbitrary", "arbitrary")),
        name="lru_sweep_rev" if rev else "lru_sweep_fwd",
    )(*args)


def _block_diag(w):
    eye = jnp.eye(LRU_BLOCKS, dtype=w.dtype)
    return jnp.einsum("ncd,nm->ncmd", w, eye).reshape(D_RNN, D_RNN)


def kernel(x, c, ctx, c_ctx, ada_w, ada_b, norm1_g, norm2_g, attn_w_qkv, attn_w_o, attn_sink,
           lru_w_in, lru_conv_w, lru_conv_b, lru_wa, lru_ba, lru_wx, lru_bx, lru_lam, lru_w_out,
           moe_router_w, moe_router_bias, moe_w_gate, moe_w_up, moe_w_down,
           shared_w_gate, shared_w_up, shared_w_down, final_g):
    nb, seq, d = x.shape
    ctx_len = ctx.shape[1]
    assert d == D_MODEL and seq % TM == 0 and ctx_len == TM and nb < 8
    tx = nb * seq
    tall = tx + nb * ctx_len
    tpb = seq // TM
    n_xt = tx // TM
    nt_all = tall // TM

    cvec = jnp.zeros((8, d), F32).at[:nb].set(c).at[nb].set(c_ctx)
    mod = _adaln(cvec, ada_w, ada_b).reshape(DEPTH, 8, 6, d)
    xall = jnp.concatenate([x.reshape(tx, d), ctx.reshape(nb * ctx_len, d)], axis=0)
    row = lambda v: v.reshape(1, -1)
    shared = lambda i: (jnp.concatenate([shared_w_gate[i], shared_w_up[i]], axis=1).astype(BF16),
                        shared_w_down[i].astype(BF16))

    qd = N_HEADS * HEAD_DIM
    kd = N_KV_HEADS * HEAD_DIM
    wq = attn_w_qkv[0][:, :qd]
    wk = attn_w_qkv[0][:, qd:qd + kd].reshape(d, N_KV_HEADS, 1, HEAD_DIM)
    wv = attn_w_qkv[0][:, qd + kd:].reshape(d, N_KV_HEADS, 1, HEAD_DIM)
    dup = lambda w: jnp.broadcast_to(w, (d, N_KV_HEADS, 2, HEAD_DIM)).reshape(d, 2 * kd)
    wqkv = jnp.concatenate([wq, dup(wk), dup(wv)], axis=1).astype(BF16)
    cos, sin = _rope_tables(seq)
    q, k, v = _qkv(xall, mod[0], row(norm1_g[0]), wqkv, cos, sin, n_xt, tpb, nb)
    o = _attention(q, k, v, attn_sink[0], nb, seq, ctx_len)
    x1, h2, pos_t, gate_t, cnt = _post(
        o, attn_w_o[0].astype(BF16), xall, mod[0], row(norm2_g[0]),
        moe_router_w[0].T, moe_router_bias[0].reshape(-1, 1), nt_all, n_xt, tpb, nb)
    sgu, sdn = shared(0)
    xall = _moe(h2, pos_t, gate_t, cnt, x1, mod[0], moe_w_gate[0], moe_w_up[0], moe_w_down[0],
                sgu, sdn, row(final_g), nt_all, n_xt, tpb, nb, False)

    gate, u_pre = _lru_in(xall, mod[1], row(norm1_g[1]), lru_w_in[0].astype(BF16), n_xt, tpb, nb)
    cw, cb = lru_conv_w[0], row(lru_conv_b[0])
    hdir = []
    for dr, rev in ((0, False), (1, True)):
        wa = _block_diag(lru_wa[0, dr]).astype(BF16)
        wx = _block_diag(lru_wx[0, dr]).astype(BF16)
        prm = (cw, cb, wa, wx, row(lru_ba[0, dr]), row(lru_bx[0, dr]), row(lru_lam[0, dr]))
        zero = jnp.zeros((nb, 1, D_RNN), F32)
        hc = _lru_sweep(u_pre, tx, nb, ctx_len, *prm, zero, rev).reshape(nb, ctx_len, D_RNN)
        h0 = hc[:, 0:1] if rev else hc[:, ctx_len - 1:ctx_len]
        if not rev:
            hdir.append(_lru_sweep(u_pre, 0, nb, seq, *prm, h0, rev))
        else:
            mixed = _lru_sweep(u_pre, 0, nb, seq, *prm, h0, rev, hf=hdir[0], gate=gate)
    x1, h2, pos_t, gate_t, cnt = _post(
        mixed, lru_w_out[0].astype(BF16), xall, mod[1], row(norm2_g[1]),
        moe_router_w[1].T, moe_router_bias[1].reshape(-1, 1), n_xt, n_xt, tpb, nb)
    sgu, sdn = shared(1)
    out = _moe(h2, pos_t, gate_t, cnt, x1, mod[1], moe_w_gate[1], moe_w_up[1], moe_w_down[1],
               sgu, sdn, row(final_g), n_xt, n_xt, tpb, nb, True)
    return out.reshape(nb, seq, d)
```

```python
import functools
import math

import jax
import jax.numpy as jnp
from jax import lax
from jax.experimental import pallas as pl
from jax.experimental.pallas import tpu as pltpu

D_MODEL = 1024
DEPTH = 2
GRID_W = 64
HEAD_DIM = 64
N_HEADS = 16
N_KV_HEADS = 4
GROUP = N_HEADS // N_KV_HEADS
WINDOW = 128
ATTN_BLOCK = 128
ROPE_THETA = 10000.0
D_RNN = 1280
LRU_BLOCKS = 16
LRU_BLOCK_W = D_RNN // LRU_BLOCKS
CONV_W = 4
CONV_LEFT = 2
LRU_C = 8.0
N_EXPERTS = 64
TOP_K = 8
N_GROUPS = 8
TOPK_GROUPS = 4
D_EXPERT = 256
ROUTED_SCALE = 2.5
RMS_EPS = 1e-6

F32 = jnp.float32
BF16 = jnp.bfloat16
I32 = jnp.int32
HIGHEST = lax.Precision.HIGHEST

TM = 256
CHUNK = 16
HB = 256
BM = 2 * HB
CPB = BM // CHUNK
CPH = HB // CHUNK
RCAP = ((TOP_K * TM + (CHUNK - 1) * N_EXPERTS + TM - 1) // TM) * TM
NCH_CAP = RCAP // CHUNK
NRB_CAP = RCAP // TM
VMEM_LIMIT = 56 * 1024 * 1024


def _cparams(sem):
    return pltpu.CompilerParams(dimension_semantics=sem, vmem_limit_bytes=VMEM_LIMIT)


def _silu(x):
    return x * jax.nn.sigmoid(x)


def _dot(a, b):
    return jnp.dot(a, b, preferred_element_type=F32)


def _dot_nt(a, b, precision=None):
    return lax.dot_general(a, b, (((1,), (1,)), ((), ())), precision=precision,
                           preferred_element_type=F32)


def _adaln_kernel(c_ref, w_ref, b_ref, o_ref):
    s = _silu(c_ref[...])
    o_ref[0] = jnp.dot(s, w_ref[0], precision=HIGHEST, preferred_element_type=F32) + b_ref[0]


def _adaln(cvec, ada_w, ada_b):
    nb = 1536
    d6 = 6 * D_MODEL
    return pl.pallas_call(
        _adaln_kernel,
        grid=(DEPTH, d6 // nb),
        in_specs=[pl.BlockSpec((8, D_MODEL), lambda l, j: (0, 0)),
                  pl.BlockSpec((1, D_MODEL, nb), lambda l, j: (l, 0, j)),
                  pl.BlockSpec((1, 1, nb), lambda l, j: (l, 0, j))],
        out_specs=pl.BlockSpec((1, 8, nb), lambda l, j: (l, 0, j)),
        out_shape=jax.ShapeDtypeStruct((DEPTH, 8, d6), F32),
        compiler_params=_cparams(("arbitrary", "arbitrary")),
        name="adaln",
    )(cvec, ada_w, ada_b.reshape(DEPTH, 1, d6))


def _norm_mod(x, g, shift, scale):
    xn = x * lax.rsqrt(jnp.mean(x * x, axis=-1, keepdims=True) + RMS_EPS) * g
    return xn * (1.0 + scale) + shift


def _qkv_kernel(x_ref, mod_ref, g_ref, w_ref, cos_ref, sin_ref, q_ref, k_ref, v_ref):
    m = mod_ref[0]
    h = _norm_mod(x_ref[...], g_ref[...], m[0:1], m[1:2])
    p = _dot(h.astype(BF16), w_ref[...])
    cos = cos_ref[...]
    sin = sin_ref[...]
    lane = lax.broadcasted_iota(I32, (TM, 128), 1)
    second = (lane & 16) != 0
    scale = HEAD_DIM ** -0.5
    nq = N_HEADS * HEAD_DIM // 128
    nk = 2 * N_KV_HEADS * HEAD_DIM // 128
    for c in range(nq + nk):
        blk = p[:, 128 * c:128 * (c + 1)]
        partner = jnp.where(second, pltpu.roll(blk, 16, 1), pltpu.roll(blk, 112, 1))
        r = blk * cos + partner * sin
        if c < nq:
            q_ref[:, 128 * c:128 * (c + 1)] = (r * scale).astype(BF16)
        else:
            k_ref[:, 128 * (c - nq):128 * (c - nq + 1)] = r.astype(BF16)
    v_ref[...] = p[:, 128 * (nq + nk):].astype(BF16)


def _qkv(xall, mod, g, w, cos, sin, n_xt, tpb, nb):
    t = xall.shape[0]
    nt = t // TM
    kw = 2 * N_KV_HEADS * HEAD_DIM
    row = lambda i: jnp.where(i < n_xt, i // tpb, nb)
    pos = lambda i: jnp.where(i < n_xt, i % tpb, tpb)
    return pl.pallas_call(
        _qkv_kernel,
        grid=(nt,),
        in_specs=[pl.BlockSpec((TM, D_MODEL), lambda i: (i, 0)),
                  pl.BlockSpec((1, 6, D_MODEL), lambda i: (row(i), 0, 0)),
                  pl.BlockSpec((1, D_MODEL), lambda i: (0, 0)),
                  pl.BlockSpec(w.shape, lambda i: (0, 0)),
                  pl.BlockSpec((TM, 128), lambda i: (pos(i), 0)),
                  pl.BlockSpec((TM, 128), lambda i: (pos(i), 0))],
        out_specs=[pl.BlockSpec((TM, D_MODEL), lambda i: (i, 0)),
                   pl.BlockSpec((TM, kw), lambda i: (i, 0)),
                   pl.BlockSpec((TM, kw), lambda i: (i, 0))],
        out_shape=[jax.ShapeDtypeStruct((t, D_MODEL), BF16),
                   jax.ShapeDtypeStruct((t, kw), BF16),
                   jax.ShapeDtypeStruct((t, kw), BF16)],
        compiler_params=_cparams(("arbitrary",)),
        name="qkv_rope",
    )(xall, mod, g, w, cos, sin)


def _rope_tables(seq):
    s = jnp.arange(seq)
    row = (s // GRID_W).astype(F32)
    col = (s % GRID_W).astype(F32)
    n_freq = HEAD_DIM // 4
    inv = jnp.exp(-math.log(ROPE_THETA) * jnp.arange(n_freq, dtype=F32) / n_freq)
    ar = row[:, None] * inv
    ac = col[:, None] * inv
    cos = jnp.concatenate([jnp.cos(ar), jnp.cos(ar), jnp.cos(ac), jnp.cos(ac)], axis=-1)
    sin = jnp.concatenate([-jnp.sin(ar), jnp.sin(ar), -jnp.sin(ac), jnp.sin(ac)], axis=-1)
    cos = jnp.concatenate([jnp.tile(cos, (1, 2)), jnp.ones((TM, 128), F32)], axis=0)
    sin = jnp.concatenate([jnp.tile(sin, (1, 2)), jnp.zeros((TM, 128), F32)], axis=0)
    return cos, sin


def _attn_kernel(nqb, sink_ref, q_ref, kp_ref, kc_ref, kn_ref, kx_ref,
                 vp_ref, vc_ref, vn_ref, vx_ref, o_ref):
    n = pl.program_id(1)
    qb = ATTN_BLOCK
    nloc = 3 * qb
    nkeys = nloc + kx_ref.shape[0]
    rows = GROUP * qb
    r = lax.broadcasted_iota(I32, (rows, nkeys), 0) % qb
    c = lax.broadcasted_iota(I32, (rows, nkeys), 1)
    d = c - r
    ok = (c < nloc) & (d >= 0) & (d <= 2 * WINDOW) & (n < nqb)
    ok = ok & ((n > 0) | (c >= qb)) & ((n < nqb - 1) | (c < 2 * qb))
    valid = ok | (c >= nloc)
    lane = lax.broadcasted_iota(I32, (qb, 128), 1)
    rsub = lax.broadcasted_iota(I32, (rows, 1), 0) // qb
    for j in range(N_KV_HEADS):
        ks = slice(128 * j, 128 * (j + 1))
        kall = jnp.concatenate([kp_ref[:, ks], kc_ref[:, ks], kn_ref[:, ks], kx_ref[:, ks]], axis=0)
        vall = jnp.concatenate([vp_ref[:, ks], vc_ref[:, ks], vn_ref[:, ks], vx_ref[:, ks]], axis=0)
        qs = []
        sink = jnp.zeros((rows, 1), F32)
        for g in range(GROUP):
            pair, half = divmod(g, 2)
            qp = q_ref[:, 256 * j + 128 * pair:256 * j + 128 * (pair + 1)]
            keep = (lane >= 64) if half else (lane < 64)
            qs.append(jnp.where(keep, qp, jnp.zeros_like(qp)))
            sink = jnp.where(rsub == g, sink_ref[GROUP * j + g], sink)
        qst = jnp.concatenate(qs, axis=0)
        s = _dot_nt(qst, kall)
        s = jnp.where(valid, s, -jnp.inf)
        m = jnp.maximum(jnp.max(s, axis=-1, keepdims=True), sink)
        p = jnp.exp(s - m)
        den = jnp.sum(p, axis=-1, keepdims=True) + jnp.exp(sink - m)
        o = _dot(p.astype(BF16), vall) * (1.0 / den)
        for pair in range(GROUP // 2):
            o0 = o[(2 * pair) * qb:(2 * pair + 1) * qb]
            o1 = o[(2 * pair + 1) * qb:(2 * pair + 2) * qb]
            o_ref[:, 256 * j + 128 * pair:256 * j + 128 * (pair + 1)] = (
                jnp.where(lane < 64, o0, o1).astype(BF16))


def _attention(q, k, v, sink, nb, seq, ctx_len):
    t = q.shape[0]
    qb = ATTN_BLOCK
    nqb = seq // qb
    ncb = ctx_len // qb
    xq = nb * nqb
    kw = k.shape[1]
    qrow = lambda b, n: jnp.where(n < nqb, b * nqb + n, xq + b * ncb + (n - nqb))
    kprev = lambda b, n, s: (b * nqb + jnp.clip(n - 1, 0, nqb - 1), 0)
    kcur = lambda b, n, s: (b * nqb + jnp.minimum(n, nqb - 1), 0)
    knext = lambda b, n, s: (b * nqb + jnp.minimum(n + 1, nqb - 1), 0)
    kctx = lambda b, n, s: (nb * seq // ctx_len + b, 0)
    kspec = lambda f: pl.BlockSpec((qb, kw), f)
    xspec = pl.BlockSpec((ctx_len, kw), kctx)
    return pl.pallas_call(
        functools.partial(_attn_kernel, nqb),
        grid_spec=pltpu.PrefetchScalarGridSpec(
            num_scalar_prefetch=1,
            grid=(nb, nqb + ncb),
            in_specs=[pl.BlockSpec((qb, D_MODEL), lambda b, n, s: (qrow(b, n), 0)),
                      kspec(kprev), kspec(kcur), kspec(knext), xspec,
                      kspec(kprev), kspec(kcur), kspec(knext), xspec],
            out_specs=pl.BlockSpec((qb, D_MODEL), lambda b, n, s: (qrow(b, n), 0)),
        ),
        out_shape=jax.ShapeDtypeStruct((t, D_MODEL), BF16),
        compiler_params=_cparams(("arbitrary", "arbitrary")),
        name="window_attn",
    )(sink, q, k, k, k, k, v, v, v, v)


def _post_kernel(m_ref, w_ref, x_ref, mod_ref, g_ref, rw_ref, rb_ref,
                 x1_ref, h2_ref, pos_ref, gate_ref, cnt_ref):
    md = mod_ref[0]
    y = _dot(m_ref[...], w_ref[...])
    x1 = x_ref[...] + md[2:3] * y
    x1_ref[...] = x1
    h2 = _norm_mod(x1, g_ref[...], md[3:4], md[4:5])
    h2b = h2.astype(BF16)
    h2_ref[...] = h2b

    ne = N_EXPERTS
    per = ne // N_GROUPS
    h2lo = (h2 - h2b.astype(F32)).astype(BF16)
    lg = _dot(h2b, rw_ref[...]) + _dot(h2lo, rw_ref[...])
    lg = lg + pltpu.roll(lg, ne, 1)
    logit = lg.T[0:ne]
    sc = jax.nn.sigmoid(logit)
    sel = sc + rb_ref[...]
    sub8 = lax.broadcasted_iota(I32, (per, TM), 0)
    gs = jnp.zeros((N_GROUPS, TM), F32)
    gi = lax.broadcasted_iota(I32, (N_GROUPS, TM), 0)
    for g in range(N_GROUPS):
        blk = sel[per * g:per * (g + 1)]
        m1 = jnp.max(blk, axis=0, keepdims=True)
        i1 = jnp.min(jnp.where(blk == m1, sub8, per), axis=0, keepdims=True)
        m2 = jnp.max(jnp.where(sub8 == i1, -jnp.inf, blk), axis=0, keepdims=True)
        gs = jnp.where(gi == g, m1 + m2, gs)
    grank = jnp.zeros((N_GROUPS, TM), F32)
    for g in range(N_GROUPS):
        v = gs[g:g + 1]
        grank = grank + jnp.where(gi > g, jnp.where(v >= gs, 1.0, 0.0), jnp.where(v > gs, 1.0, 0.0))
    ei = lax.broadcasted_iota(I32, (ne, TM), 0)
    gsel = jnp.zeros((ne, TM), F32)
    for g in range(N_GROUPS):
        gsel = jnp.where(ei // per == g, grank[g:g + 1], gsel)
    selm = jnp.where(gsel < TOPK_GROUPS, sel, -jnp.inf)
    eif = ei.astype(F32)
    s8 = jnp.zeros((ne, TM), F32)
    for _ in range(TOP_K):
        best = jnp.max(selm, axis=0, keepdims=True)
        first = jnp.min(jnp.where(selm == best, eif, float(ne)), axis=0, keepdims=True)
        hit = eif == first
        s8 = jnp.where(hit, 1.0, s8)
        selm = jnp.where(hit, -jnp.inf, selm)
    ws = s8 * sc
    gate = ws / jnp.sum(ws, axis=0, keepdims=True) * ROUTED_SCALE

    s8b = s8.astype(BF16)
    er = lax.broadcasted_iota(I32, (ne, ne), 0)
    ec = lax.broadcasted_iota(I32, (ne, ne), 1)
    lower = jnp.where(ec < er, 1.0, 0.0).astype(BF16)
    tr = lax.broadcasted_iota(I32, (TM, TM), 0)
    tc = lax.broadcasted_iota(I32, (TM, TM), 1)
    upper = jnp.where(tr < tc, 1.0, 0.0).astype(BF16)
    slot = _dot(lower, s8b)
    rank = _dot(s8b, upper)
    cnt = jnp.sum(s8, axis=1, keepdims=True)
    pc = jnp.floor((cnt + (CHUNK - 1)) * (1.0 / CHUNK))
    pcb = jnp.broadcast_to(pc, (ne, 128))
    cnt_ref[0] = pcb
    loc = _dot(lower, pcb.astype(BF16))[:, 0:1] * CHUNK
    prow = loc + rank
    k8 = lax.broadcasted_iota(I32, (TOP_K, TM), 0)
    pos_t = jnp.zeros((TOP_K, TM), F32)
    gate_t = jnp.zeros((TOP_K, TM), F32)
    for k in range(TOP_K):
        mk = jnp.where(slot == k, s8, 0.0)
        pos_t = jnp.where(k8 == k, jnp.sum(mk * prow, axis=0, keepdims=True), pos_t)
        gate_t = jnp.where(k8 == k, jnp.sum(mk * gate, axis=0, keepdims=True), gate_t)
    pos_ref[...] = pos_t.astype(I32)
    gate_ref[...] = gate_t


def _post(m, w, xall, mod, g, rw_t, rb, nt, n_xt, tpb, nb):
    t = nt * TM
    kd = m.shape[1]
    row = lambda i: jnp.where(i < n_xt, i // tpb, nb)
    return pl.pallas_call(
        _post_kernel,
        grid=(nt,),
        in_specs=[pl.BlockSpec((TM, kd), lambda i: (i, 0)),
                  pl.BlockSpec(w.shape, lambda i: (0, 0)),
                  pl.BlockSpec((TM, D_MODEL), lambda i: (i, 0)),
                  pl.BlockSpec((1, 6, D_MODEL), lambda i: (row(i), 0, 0)),
                  pl.BlockSpec((1, D_MODEL), lambda i: (0, 0)),
                  pl.BlockSpec((D_MODEL, 2 * N_EXPERTS), lambda i: (0, 0)),
                  pl.BlockSpec((N_EXPERTS, 1), lambda i: (0, 0))],
        out_specs=[pl.BlockSpec((TM, D_MODEL), lambda i: (i, 0)),
                   pl.BlockSpec((TM, D_MODEL), lambda i: (i, 0)),
                   pl.BlockSpec((TOP_K, TM), lambda i: (0, i)),
                   pl.BlockSpec((TOP_K, TM), lambda i: (0, i)),
                   pl.BlockSpec((1, N_EXPERTS, 128), lambda i: (i, 0, 0))],
        out_shape=[jax.ShapeDtypeStruct((t, D_MODEL), F32),
                   jax.ShapeDtypeStruct((t, D_MODEL), BF16),
                   jax.ShapeDtypeStruct((TOP_K, t), I32),
                   jax.ShapeDtypeStruct((TOP_K, t), F32),
                   jax.ShapeDtypeStruct((nt, N_EXPERTS, 128), F32)],
        compiler_params=_cparams(("arbitrary",)),
        name="mix_out_route",
    )(m, w, xall, mod, g, rw_t, rb)


def _moe_tables(pc, nblk_cap):
    nt = pc.shape[0]
    seg = jnp.sum(pc, axis=0)
    segb = (seg + CPB - 1) // CPB
    seg_end_b = jnp.cumsum(segb)
    seg_start = (seg_end_b - segb) * CPB
    goff = seg_start[None, :] + jnp.cumsum(pc, axis=0) - pc
    loc_end = jnp.cumsum(pc, axis=1)
    loc = loc_end - pc
    nch = loc_end[:, -1]
    nrb = (nch + CPH - 1) // CPH
    j = jnp.arange(NCH_CAP, dtype=I32)
    inrun = (j[None, :, None] >= loc[:, None, :]) & (j[None, :, None] < loc_end[:, None, :])
    shift = jnp.sum(jnp.where(inrun, (goff - loc)[:, None, :], 0), axis=-1)
    live = j[None, :] < nch[:, None]
    trash = nblk_cap * CPB + jnp.arange(nt, dtype=I32)[:, None] * CPH + j[None, :] % CPH
    dst = jnp.where(live, shift + j[None, :], trash).astype(I32)
    src = jnp.where(live, shift + j[None, :], 0).astype(I32)
    halves = (seg + CPH - 1) // CPH
    tl = jnp.arange(CPH - 1, dtype=I32)
    tail_len = halves * CPH - seg
    tails = jnp.where(tl[None, :] < tail_len[:, None], (seg_start + seg)[:, None] + tl[None, :], -1)
    hzero = jnp.where(halves % 2 == 1, seg_start + halves * CPH, -1)
    nused = seg_end_b[-1]
    blk = jnp.arange(nblk_cap, dtype=I32)
    bexp = jnp.sum((blk[:, None] >= seg_end_b[None, :]).astype(I32), axis=-1)
    last = jnp.sum((nused - 1 >= seg_end_b).astype(I32))
    bexp = jnp.where(blk < nused, bexp, last).astype(I32)
    onehot = (bexp[:, None] == jnp.arange(N_EXPERTS, dtype=I32)[None, :]).astype(I32)
    first_blk = seg_end_b - segb
    bhalf = jnp.sum(onehot * (halves - 2 * (blk[:, None] - first_blk[None, :])), axis=-1)
    bhalf = jnp.where(blk < nused, jnp.clip(bhalf, 1, 2), 0).astype(I32)
    return dict(nrb=nrb.astype(I32), dst=dst.reshape(-1), src=src.reshape(-1),
                tails=tails.reshape(-1).astype(I32), hzero=hzero.astype(I32),
                nused=nused.reshape(1).astype(I32), bexp=bexp, bhalf=bhalf)


def _rows_copy(src, dst, sem):
    return pltpu.make_async_copy(src, dst, sem)


def _row_onehot(riota, pos, base, value):
    p = jnp.zeros((TM, TM), BF16)
    for k in range(TOP_K):
        d = jnp.clip(pos[k:k + 1] - base, -1, TM).astype(F32).astype(BF16)
        v = 1.0 if value is None else value[k:k + 1]
        p = jnp.where(riota == d, v, p)
    return p


def _dispatch_kernel(nt, nrb_ref, dst_ref, tails_ref, hzero_ref, h2_ref, pos_ref, xs_hbm, buf, sem):
    i = pl.program_id(0)
    slot = i % 2

    def drain(tile, sl):
        def body(rb, carry):
            _rows_copy(buf.at[sl, pl.ds(0, TM)], xs_hbm.at[pl.ds(0, TM)], sem.at[sl]).wait()
            return carry
        lax.fori_loop(0, nrb_ref[tile], body, 0)

    @pl.when((i >= 2) & (i < nt))
    def _():
        drain(i - 2, slot)

    @pl.when(i < nt)
    def _():
        h2 = h2_ref[...]
        pos = pos_ref[...]
        riota = lax.broadcasted_iota(I32, (TM, TM), 0).astype(F32).astype(BF16)
        nrb = nrb_ref[i]

        def build(rb):
            base = pl.multiple_of(rb * TM, TM)
            p = _row_onehot(riota, pos, base, None)
            buf[slot, pl.ds(base, TM), :] = _dot(p, h2).astype(BF16)

        def issue(rb):
            for c in range(CPH):
                d = dst_ref[i * NCH_CAP + rb * CPH + c]
                _rows_copy(buf.at[slot, pl.ds(pl.multiple_of(rb * TM + c * CHUNK, CHUNK), CHUNK)],
                           xs_hbm.at[pl.ds(pl.multiple_of(d * CHUNK, CHUNK), CHUNK)],
                           sem.at[slot]).start()

        build(0)

        def step(rb, carry):
            issue(rb - 1)
            build(rb)
            return carry

        lax.fori_loop(1, nrb, step, 0)
        issue(nrb - 1)

    @pl.when(i == nt)
    def _():
        drain(nt - 2, nt % 2)
        drain(nt - 1, (nt - 1) % 2)
        buf[0, pl.ds(0, HB), :] = jnp.zeros((HB, D_MODEL), BF16)
        ntail = N_EXPERTS * (CPH - 1)

        def tail(j, carry):
            d = tails_ref[j]

            @pl.when(d >= 0)
            def _():
                _rows_copy(buf.at[0, pl.ds(0, CHUNK)],
                           xs_hbm.at[pl.ds(pl.multiple_of(d * CHUNK, CHUNK), CHUNK)], sem.at[0]).start()
            return carry

        lax.fori_loop(0, ntail, tail, 0)

        def half(e, carry):
            d = hzero_ref[e]

            @pl.when(d >= 0)
            def _():
                _rows_copy(buf.at[0, pl.ds(0, HB)],
                           xs_hbm.at[pl.ds(pl.multiple_of(d * CHUNK, CHUNK), HB)], sem.at[1]).start()
            return carry

        lax.fori_loop(0, N_EXPERTS, half, 0)

        def tail_wait(j, carry):
            @pl.when(tails_ref[j] >= 0)
            def _():
                _rows_copy(buf.at[0, pl.ds(0, CHUNK)], xs_hbm.at[pl.ds(0, CHUNK)], sem.at[0]).wait()
            return carry

        lax.fori_loop(0, ntail, tail_wait, 0)

        def half_wait(e, carry):
            @pl.when(hzero_ref[e] >= 0)
            def _():
                _rows_copy(buf.at[0, pl.ds(0, HB)], xs_hbm.at[pl.ds(0, HB)], sem.at[1]).wait()
            return carry

        lax.fori_loop(0, N_EXPERTS, half_wait, 0)


def _dispatch(tb, h2, pos_t, nt, nblk_cap):
    last = nt - 1
    return pl.pallas_call(
        functools.partial(_dispatch_kernel, nt),
        grid_spec=pltpu.PrefetchScalarGridSpec(
            num_scalar_prefetch=4,
            grid=(nt + 1,),
            in_specs=[pl.BlockSpec((TM, D_MODEL), lambda i, *_: (jnp.minimum(i, last), 0)),
                      pl.BlockSpec((TOP_K, TM), lambda i, *_: (0, jnp.minimum(i, last)))],
            out_specs=pl.BlockSpec(memory_space=pl.ANY),
            scratch_shapes=[pltpu.VMEM((2, RCAP, D_MODEL), BF16), pltpu.SemaphoreType.DMA((2,))],
        ),
        out_shape=jax.ShapeDtypeStruct((nblk_cap * BM + nt * HB, D_MODEL), BF16),
        compiler_params=_cparams(("arbitrary",)),
        name="moe_dispatch",
    )(tb["nrb"], tb["dst"], tb["tails"], tb["hzero"], h2, pos_t)


def _expert_kernel(bexp_ref, bhalf_ref, nused_ref, x_ref, wg_ref, wu_ref, wd_ref, y_ref, wgu, wdn):
    i = pl.program_id(0)
    e = bexp_ref[i]
    prev = bexp_ref[jnp.maximum(i - 1, 0)]

    @pl.when((i == 0) | (e != prev))
    def _():
        wgu[:, :D_EXPERT] = wg_ref[0, 0].astype(BF16)
        wgu[:, D_EXPERT:] = wu_ref[0, 0].astype(BF16)
        wdn[...] = wd_ref[0, 0].astype(BF16)

    def chain(r0):
        gu = _dot(x_ref[r0:r0 + HB, :], wgu[...])
        h = _silu(gu[:, :D_EXPERT]) * gu[:, D_EXPERT:]
        y_ref[r0:r0 + HB, :] = _dot(h.astype(BF16), wdn[...]).astype(BF16)

    nh = bhalf_ref[i]

    @pl.when(nh == 2)
    def _():
        chain(0)
        chain(HB)

    @pl.when(nh == 1)
    def _():
        chain(0)
        y_ref[HB:, :] = jnp.zeros((HB, D_MODEL), BF16)


def _experts(tb, xs, w_gate, w_up, w_down, layer, nblk_cap):
    xmap = lambda i, be, bh, nu: (jnp.minimum(i, nu[0] - 1), 0)
    wmap = lambda i, be, bh, nu: (layer, be[i], 0, 0)
    return pl.pallas_call(
        _expert_kernel,
        grid_spec=pltpu.PrefetchScalarGridSpec(
            num_scalar_prefetch=3,
            grid=(nblk_cap,),
            in_specs=[pl.BlockSpec((BM, D_MODEL), xmap),
                      pl.BlockSpec((1, 1, D_MODEL, D_EXPERT), wmap),
                      pl.BlockSpec((1, 1, D_MODEL, D_EXPERT), wmap),
                      pl.BlockSpec((1, 1, D_EXPERT, D_MODEL), wmap)],
            out_specs=pl.BlockSpec((BM, D_MODEL), xmap),
            scratch_shapes=[pltpu.VMEM((D_MODEL, 2 * D_EXPERT), BF16),
                            pltpu.VMEM((D_EXPERT, D_MODEL), BF16)],
        ),
        out_shape=jax.ShapeDtypeStruct((nblk_cap * BM, D_MODEL), BF16),
        compiler_params=_cparams(("arbitrary",)),
        name="moe_experts",
    )(tb["bexp"], tb["bhalf"], tb["nused"], xs, w_gate, w_up, w_down)


def _combine_kernel(final, nt, nrb_ref, src_ref, ys_hbm, pos_ref, gate_ref, h2_ref, x1_ref,
                    mod_ref, sgu_ref, sdn_ref, fg_ref, o_ref, buf, acc, sem):
    i = pl.program_id(0)
    slot = i % 2
    nxt = jnp.minimum(i + 1, nt - 1)
    nrb = nrb_ref[i]
    nrb_next = nrb_ref[nxt]

    def issue(tile, sl, rb):
        for c in range(CPH):
            s = src_ref[tile * NCH_CAP + rb * CPH + c]
            _rows_copy(ys_hbm.at[pl.ds(pl.multiple_of(s * CHUNK, CHUNK), CHUNK)],
                       buf.at[sl, pl.ds(pl.multiple_of(rb * TM + c * CHUNK, CHUNK), CHUNK)],
                       sem.at[sl]).start()

    def wait_blocks(sl, n):
        def body(rb, carry):
            _rows_copy(ys_hbm.at[pl.ds(0, TM)], buf.at[sl, pl.ds(0, TM)], sem.at[sl]).wait()
            return carry
        lax.fori_loop(0, n, body, 0)

    @pl.when(i == 0)
    def _():
        def first(rb, carry):
            issue(0, 0, rb)
            return carry
        lax.fori_loop(0, nrb, first, 0)

    gu = _dot(h2_ref[...], sgu_ref[...])
    hs = _silu(gu[:, :D_EXPERT]) * gu[:, D_EXPERT:]
    acc[...] = _dot(hs.astype(BF16), sdn_ref[...])

    fetched = jnp.where(i == 0, nrb, jnp.maximum(nrb_ref[jnp.maximum(i - 1, 0)], nrb))
    wait_blocks(slot, fetched)

    pos = pos_ref[...]
    gate = gate_ref[...].astype(BF16)
    riota = lax.broadcasted_iota(I32, (TM, TM), 0).astype(F32).astype(BF16)

    def fold(rb, carry):
        issue(nxt, 1 - slot, rb)
        base = pl.multiple_of(rb * TM, TM)
        p = _row_onehot(riota, pos, base, gate)
        acc[...] += lax.dot_general(p, buf[slot, pl.ds(base, TM), :], (((0,), (0,)), ((), ())),
                                    preferred_element_type=F32)
        return carry

    lax.fori_loop(0, nrb, fold, 0)

    def rest(rb, carry):
        issue(nxt, 1 - slot, rb)
        return carry

    lax.fori_loop(nrb, nrb_next, rest, 0)

    @pl.when(i == nt - 1)
    def _():
        wait_blocks(1 - slot, jnp.maximum(nrb, nrb_next))

    x2 = x1_ref[...] + mod_ref[0][5:6] * acc[...]
    if final:
        x2 = x2 * lax.rsqrt(jnp.mean(x2 * x2, axis=-1, keepdims=True) + RMS_EPS) * fg_ref[...]
    o_ref[...] = x2


def _combine(tb, ys, pos, gate, h2, x1, mod, sgu, sdn, fg, nt, n_xt, tpb, nb, final):
    t = nt * TM
    row = lambda i, *_: (jnp.where(i < n_xt, i // tpb, nb), 0, 0)
    tile = lambda i, *_: (i, 0)
    slots = lambda i, *_: (0, i)
    const = lambda i, *_: (0, 0)
    return pl.pallas_call(
        functools.partial(_combine_kernel, final, nt),
        grid_spec=pltpu.PrefetchScalarGridSpec(
            num_scalar_prefetch=2,
            grid=(nt,),
            in_specs=[pl.BlockSpec(memory_space=pl.ANY),
                      pl.BlockSpec((TOP_K, TM), slots),
                      pl.BlockSpec((TOP_K, TM), slots),
                      pl.BlockSpec((TM, D_MODEL), tile),
                      pl.BlockSpec((TM, D_MODEL), tile),
                      pl.BlockSpec((1, 6, D_MODEL), row),
                      pl.BlockSpec(sgu.shape, const),
                      pl.BlockSpec(sdn.shape, const),
                      pl.BlockSpec((1, D_MODEL), const)],
            out_specs=pl.BlockSpec((TM, D_MODEL), tile),
            scratch_shapes=[pltpu.VMEM((2, RCAP, D_MODEL), BF16),
                            pltpu.VMEM((TM, D_MODEL), F32),
                            pltpu.SemaphoreType.DMA((2,))],
        ),
        out_shape=jax.ShapeDtypeStruct((t, D_MODEL), F32),
        compiler_params=_cparams(("arbitrary",)),
        name="moe_combine",
    )(tb["nrb"], tb["src"], ys, pos, gate, h2, x1, mod, sgu, sdn, fg)


def _moe(h2, pos_t, gate_t, cnt, x1, mod, w_gate, w_up, w_down, layer, sgu, sdn, fg,
         nt, n_xt, tpb, nb, final):
    nblk_cap = (TOP_K * nt * TM + (CHUNK - 1) * N_EXPERTS * nt) // BM + N_EXPERTS
    tb = _moe_tables(cnt[:, :, 0].astype(I32), nblk_cap)
    xs = _dispatch(tb, h2, pos_t, nt, nblk_cap)
    ys = _experts(tb, xs, w_gate, w_up, w_down, layer, nblk_cap)
    return _combine(tb, ys, pos_t, gate_t, h2, x1, mod, sgu, sdn, fg, nt, n_xt, tpb, nb, final)


def _lru_in_kernel(x_ref, mod_ref, g_ref, w_ref, gate_ref, u_ref):
    m = mod_ref[0]
    h = _norm_mod(x_ref[...], g_ref[...], m[0:1], m[1:2])
    p = _dot(h.astype(BF16), w_ref[...])
    z = p[:, :D_RNN]
    cdf = 0.5 * (1.0 + jnp.tanh(math.sqrt(2.0 / math.pi) * (z + 0.044715 * (z * z * z))))
    gate_ref[...] = (z * cdf).astype(BF16)
    u_ref[...] = p[:, D_RNN:]


def _lru_in(xall, mod, g, w, n_xt, tpb, nb):
    t = xall.shape[0]
    nt = t // TM
    row = lambda i: jnp.where(i < n_xt, i // tpb, nb)
    return pl.pallas_call(
        _lru_in_kernel,
        grid=(nt,),
        in_specs=[pl.BlockSpec((TM, D_MODEL), lambda i: (i, 0)),
                  pl.BlockSpec((1, 6, D_MODEL), lambda i: (row(i), 0, 0)),
                  pl.BlockSpec((1, D_MODEL), lambda i: (0, 0)),
                  pl.BlockSpec(w.shape, lambda i: (0, 0))],
        out_specs=[pl.BlockSpec((TM, D_RNN), lambda i: (i, 0)),
                   pl.BlockSpec((TM, D_RNN), lambda i: (i, 0))],
        out_shape=[jax.ShapeDtypeStruct((t, D_RNN), BF16),
                   jax.ShapeDtypeStruct((t, D_RNN), F32)],
        compiler_params=_cparams(("arbitrary",)),
        name="lru_in",
    )(xall, mod, g, w)


def _sigmoid(x):
    return 0.5 * (1.0 + jnp.tanh(0.5 * x))


def _gate_windows():
    wins = []
    for c0 in range(0, D_RNN, 256):
        c1 = min(c0 + 256, D_RNN)
        k0 = (c0 // LRU_BLOCK_W) * LRU_BLOCK_W
        k1 = ((c1 - 1) // LRU_BLOCK_W + 1) * LRU_BLOCK_W
        wins.append((c0, c1, (k0 // 128) * 128, min(-(-k1 // 128) * 128, D_RNN)))
    return wins


def _block_diag_dot(ub, w_ref):
    return jnp.concatenate([_dot(ub[:, k0:k1], w_ref[k0:k1, c0:c1]) for c0, c1, k0, k1 in _gate_windows()],
                           axis=1)


def _lru_sweep_kernel(rev, mix, ns, u_ref, up_ref, un_ref, cw_ref, cb_ref, wa_ref, wx_ref,
                      ba_ref, bx_ref, lam_ref, h0_ref, *rest):
    if mix:
        hf_ref, gate_ref, o_ref, a_s, b_s, carry = rest
    else:
        o_ref, a_s, b_s, carry = rest
    s = pl.program_id(1)
    ss = (ns - 1 - s) if rev else s

    @pl.when(s == 0)
    def _():
        carry[...] = jnp.broadcast_to(h0_ref[0], (8, D_RNN))

    prev = jnp.where(ss > 0, up_ref[...], 0.0)
    nxt = jnp.where(ss < ns - 1, un_ref[...], 0.0)
    ext = jnp.concatenate([prev, u_ref[...], nxt], axis=0)
    n_ext = TM + 16
    cw = cw_ref[...]
    u = cb_ref[...]
    for j in range(CONV_W):
        shift = (CONV_LEFT - j) % n_ext
        tap = ext if shift == 0 else pltpu.roll(ext, shift, 0)
        u = u + cw[j:j + 1] * tap[8:8 + TM]

    ub = u.astype(BF16)
    r = _sigmoid(_block_diag_dot(ub, wa_ref) + ba_ref[...])
    ig = _sigmoid(_block_diag_dot(ub, wx_ref) + bx_ref[...])
    nl = -lam_ref[...]
    softplus = jnp.maximum(nl, 0.0) + jnp.log1p(jnp.exp(-jnp.abs(nl)))
    y = LRU_C * r * softplus
    a = jnp.exp(-y)
    a_s[...] = a
    b_s[...] = jnp.sqrt(jnp.tanh(y) * (1.0 + a * a)) * (ig * u)

    sub = lax.broadcasted_iota(I32, (8, D_RNN), 0)
    h = carry[...]
    ng = TM // 8
    for gi in range(ng):
        r0 = 8 * ((ng - 1 - gi) if rev else gi)
        a = a_s[r0:r0 + 8, :]
        b = b_s[r0:r0 + 8, :]
        for sh in (1, 2, 4):
            amt = (8 - sh) if rev else sh
            keep = (sub < 8 - sh) if rev else (sub >= sh)
            a_sh = pltpu.roll(a, amt, 0)
            b_sh = pltpu.roll(b, amt, 0)
            b = jnp.where(keep, a * b_sh + b, b)
            a = jnp.where(keep, a * a_sh, a)
        hg = a * h + b
        edge = hg[0:1] if rev else hg[7:8]
        h = jnp.broadcast_to(edge, (8, D_RNN))
        if mix:
            o_ref[r0:r0 + 8, :] = ((hf_ref[r0:r0 + 8, :] + hg)
                                   * gate_ref[r0:r0 + 8, :].astype(F32)).astype(BF16)
        else:
            o_ref[r0:r0 + 8, :] = hg
    carry[...] = h


def _lru_sweep(u_all, row0, nb, seq, cw, cb, wa, wx, ba, bx, lam, h0, rev, hf=None, gate=None):
    ns = seq // TM
    blk0 = row0 // TM
    mix = hf is not None
    sidx = (lambda s: ns - 1 - s) if rev else (lambda s: s)
    cur = lambda b, s: (blk0 + b * ns + sidx(s), 0)
    out = lambda b, s: (b * ns + sidx(s), 0)
    h8 = TM // 8
    nblk8 = u_all.shape[0] // 8
    prv = lambda b, s: (jnp.maximum((blk0 + b * ns + sidx(s)) * h8 - 1, 0), 0)
    nxt = lambda b, s: (jnp.minimum((blk0 + b * ns + sidx(s) + 1) * h8, nblk8 - 1), 0)
    vec = lambda b, s: (0, 0)
    in_specs = [pl.BlockSpec((TM, D_RNN), cur),
                pl.BlockSpec((8, D_RNN), prv),
                pl.BlockSpec((8, D_RNN), nxt),
                pl.BlockSpec((CONV_W, D_RNN), vec),
                pl.BlockSpec((1, D_RNN), vec),
                pl.BlockSpec((D_RNN, D_RNN), vec),
                pl.BlockSpec((D_RNN, D_RNN), vec),
                pl.BlockSpec((1, D_RNN), vec),
                pl.BlockSpec((1, D_RNN), vec),
                pl.BlockSpec((1, D_RNN), vec),
                pl.BlockSpec((1, 1, D_RNN), lambda b, s: (b, 0, 0))]
    args = [u_all, u_all, u_all, cw, cb, wa, wx, ba, bx, lam, h0]
    if mix:
        in_specs += [pl.BlockSpec((TM, D_RNN), out), pl.BlockSpec((TM, D_RNN), cur)]
        args += [hf, gate]
    return pl.pallas_call(
        functools.partial(_lru_sweep_kernel, rev, mix, ns),
        grid=(nb, ns),
        in_specs=in_specs,
        out_specs=pl.BlockSpec((TM, D_RNN), out),
        out_shape=jax.ShapeDtypeStruct((nb * seq, D_RNN), BF16 if mix else F32),
        scratch_shapes=[pltpu.VMEM((TM, D_RNN), F32), pltpu.VMEM((TM, D_RNN), F32),
                        pltpu.VMEM((8, D_RNN), F32)],
        compiler_params=_cparams(("arbitrary", "arbitrary")),
        name="lru_sweep_rev" if rev else "lru_sweep_fwd",
    )(*args)


def _block_diag(w):
    eye = jnp.eye(LRU_BLOCKS, dtype=w.dtype)
    return jnp.einsum("ncd,nm->ncmd", w, eye).reshape(D_RNN, D_RNN)


def kernel(x, c, ctx, c_ctx, ada_w, ada_b, norm1_g, norm2_g, attn_w_qkv, attn_w_o, attn_sink,
           lru_w_in, lru_conv_w, lru_conv_b, lru_wa, lru_ba, lru_wx, lru_bx, lru_lam, lru_w_out,
           moe_router_w, moe_router_bias, moe_w_gate, moe_w_up, moe_w_down,
           shared_w_gate, shared_w_up, shared_w_down, final_g):
    nb, seq, d = x.shape
    ctx_len = ctx.shape[1]
    assert d == D_MODEL and seq % TM == 0 and ctx_len == TM and nb < 8
    tx = nb * seq
    tall = tx + nb * ctx_len
    tpb = seq // TM
    n_xt = tx // TM
    nt_all = tall // TM

    cvec = jnp.zeros((8, d), F32).at[:nb].set(c).at[nb].set(c_ctx)
    mod = _adaln(cvec, ada_w, ada_b).reshape(DEPTH, 8, 6, d)
    xall = jnp.concatenate([x.reshape(tx, d), ctx.reshape(nb * ctx_len, d)], axis=0)
    row = lambda v: v.reshape(1, -1)
    shared = lambda i: (jnp.concatenate([shared_w_gate[i], shared_w_up[i]], axis=1).astype(BF16),
                        shared_w_down[i].astype(BF16))

    def router(i):
        hi = moe_router_w[i].astype(BF16)
        lo = (moe_router_w[i] - hi.astype(F32)).astype(BF16)
        return jnp.concatenate([hi, lo], axis=1)

    qd = N_HEADS * HEAD_DIM
    kd = N_KV_HEADS * HEAD_DIM
    wq = attn_w_qkv[0][:, :qd]
    wk = attn_w_qkv[0][:, qd:qd + kd].reshape(d, N_KV_HEADS, 1, HEAD_DIM)
    wv = attn_w_qkv[0][:, qd + kd:].reshape(d, N_KV_HEADS, 1, HEAD_DIM)
    dup = lambda w: jnp.broadcast_to(w, (d, N_KV_HEADS, 2, HEAD_DIM)).reshape(d, 2 * kd)
    wqkv = jnp.concatenate([wq, dup(wk), dup(wv)], axis=1).astype(BF16)
    cos, sin = _rope_tables(seq)
    q, k, v = _qkv(xall, mod[0], row(norm1_g[0]), wqkv, cos, sin, n_xt, tpb, nb)
    o = _attention(q, k, v, attn_sink[0], nb, seq, ctx_len)
    x1, h2, pos_t, gate_t, cnt = _post(
        o, attn_w_o[0].astype(BF16), xall, mod[0], row(norm2_g[0]),
        router(0), moe_router_bias[0].reshape(-1, 1), nt_all, n_xt, tpb, nb)
    sgu, sdn = shared(0)
    xall = _moe(h2, pos_t, gate_t, cnt, x1, mod[0], moe_w_gate, moe_w_up, moe_w_down, 0,
                sgu, sdn, row(final_g), nt_all, n_xt, tpb, nb, False)

    gate, u_pre = _lru_in(xall, mod[1], row(norm1_g[1]), lru_w_in[0].astype(BF16), n_xt, tpb, nb)
    cw, cb = lru_conv_w[0], row(lru_conv_b[0])
    hdir = []
    for dr, rev in ((0, False), (1, True)):
        wa = _block_diag(lru_wa[0, dr]).astype(BF16)
        wx = _block_diag(lru_wx[0, dr]).astype(BF16)
        prm = (cw, cb, wa, wx, row(lru_ba[0, dr]), row(lru_bx[0, dr]), row(lru_lam[0, dr]))
        zero = jnp.zeros((nb, 1, D_RNN), F32)
        hc = _lru_sweep(u_pre, tx, nb, ctx_len, *prm, zero, rev).reshape(nb, ctx_len, D_RNN)
        h0 = hc[:, 0:1] if rev else hc[:, ctx_len - 1:ctx_len]
        if not rev:
            hdir.append(_lru_sweep(u_pre, 0, nb, seq, *prm, h0, rev))
        else:
            mixed = _lru_sweep(u_pre, 0, nb, seq, *prm, h0, rev, hf=hdir[0], gate=gate)
    x1, h2, pos_t, gate_t, cnt = _post(
        mixed, lru_w_out[0].astype(BF16), xall, mod[1], row(norm2_g[1]),
        router(1), moe_router_bias[1].reshape(-1, 1), n_xt, n_xt, tpb, nb)
    sgu, sdn = shared(1)
    out = _moe(h2, pos_t, gate_t, cnt, x1, mod[1], moe_w_gate, moe_w_up, moe_w_down, 1,
               sgu, sdn, row(final_g), n_xt, n_xt, tpb, nb, True)
    return out.reshape(nb, seq, d)
```

```python
import functools
import math

import jax
import jax.numpy as jnp
from jax import lax
from jax.experimental import pallas as pl
from jax.experimental.pallas import tpu as pltpu

D_MODEL = 1024
DEPTH = 2
GRID_W = 64
HEAD_DIM = 64
N_HEADS = 16
N_KV_HEADS = 4
GROUP = N_HEADS // N_KV_HEADS
WINDOW = 128
ATTN_BLOCK = 128
ROPE_THETA = 10000.0
D_RNN = 1280
LRU_BLOCKS = 16
LRU_BLOCK_W = D_RNN // LRU_BLOCKS
CONV_W = 4
CONV_LEFT = 2
LRU_C = 8.0
N_EXPERTS = 64
TOP_K = 8
N_GROUPS = 8
TOPK_GROUPS = 4
D_EXPERT = 256
ROUTED_SCALE = 2.5
RMS_EPS = 1e-6

F32 = jnp.float32
BF16 = jnp.bfloat16
I32 = jnp.int32
HIGHEST = lax.Precision.HIGHEST
LOG2E = math.log2(math.e)

TM = 256
CHUNK = 16
HB = 256
BM = 2 * HB
CPB = BM // CHUNK
CPH = HB // CHUNK
RCAP = ((TOP_K * TM + (CHUNK - 1) * N_EXPERTS + TM - 1) // TM) * TM
NCH_CAP = RCAP // CHUNK
NRB_CAP = RCAP // TM
MIN_RB = TOP_K
EAGER_RB = 10
SPARE = (NRB_CAP - MIN_RB) * CPH
VMEM_LIMIT = 56 * 1024 * 1024


def _cparams(sem):
    return pltpu.CompilerParams(dimension_semantics=sem, vmem_limit_bytes=VMEM_LIMIT)


def _silu(x):
    return x * jax.nn.sigmoid(x)


def _dot(a, b):
    return jnp.dot(a, b, preferred_element_type=F32)


def _dot_nt(a, b, precision=None):
    return lax.dot_general(a, b, (((1,), (1,)), ((), ())), precision=precision,
                           preferred_element_type=F32)


def _adaln_kernel(c_ref, w_ref, b_ref, o_ref):
    s = _silu(c_ref[...])
    o_ref[0] = jnp.dot(s, w_ref[0], precision=HIGHEST, preferred_element_type=F32) + b_ref[0]


def _adaln(cvec, ada_w, ada_b):
    nb = 1536
    d6 = 6 * D_MODEL
    return pl.pallas_call(
        _adaln_kernel,
        grid=(DEPTH, d6 // nb),
        in_specs=[pl.BlockSpec((8, D_MODEL), lambda l, j: (0, 0)),
                  pl.BlockSpec((1, D_MODEL, nb), lambda l, j: (l, 0, j)),
                  pl.BlockSpec((1, 1, nb), lambda l, j: (l, 0, j))],
        out_specs=pl.BlockSpec((1, 8, nb), lambda l, j: (l, 0, j)),
        out_shape=jax.ShapeDtypeStruct((DEPTH, 8, d6), F32),
        compiler_params=_cparams(("arbitrary", "arbitrary")),
        name="adaln",
    )(cvec, ada_w, ada_b.reshape(DEPTH, 1, d6))


def _norm_mod(x, g, shift, scale):
    xn = x * lax.rsqrt(jnp.mean(x * x, axis=-1, keepdims=True) + RMS_EPS) * g
    return xn * (1.0 + scale) + shift


def _qkv_kernel(n_xt, x_ref, c_ref, mod_ref, g_ref, w_ref, cos_ref, sin_ref, q_ref, k_ref, v_ref):
    m = mod_ref[0]
    x = jnp.where(pl.program_id(0) < n_xt, x_ref[...], c_ref[...])
    h = _norm_mod(x, g_ref[...], m[0:1], m[1:2])
    p = _dot(h.astype(BF16), w_ref[...])
    cos = cos_ref[...]
    sin = sin_ref[...]
    lane = lax.broadcasted_iota(I32, (TM, 128), 1)
    second = (lane & 16) != 0
    scale = HEAD_DIM ** -0.5 * LOG2E
    nq = N_HEADS * HEAD_DIM // 128
    nk = 2 * N_KV_HEADS * HEAD_DIM // 128
    for c in range(nq + nk):
        blk = p[:, 128 * c:128 * (c + 1)]
        partner = jnp.where(second, pltpu.roll(blk, 16, 1), pltpu.roll(blk, 112, 1))
        r = blk * cos + partner * sin
        if c < nq:
            q_ref[:, 128 * c:128 * (c + 1)] = (r * scale).astype(BF16)
        else:
            k_ref[:, 128 * (c - nq):128 * (c - nq + 1)] = r.astype(BF16)
    v_ref[...] = p[:, 128 * (nq + nk):].astype(BF16)


def _qkv(x2d, c2d, mod, g, w, cos, sin, n_xt, tpb, nb):
    t = x2d.shape[0] + c2d.shape[0]
    nt = t // TM
    kw = 2 * N_KV_HEADS * HEAD_DIM
    row = lambda i: jnp.where(i < n_xt, i // tpb, nb)
    pos = lambda i: jnp.where(i < n_xt, i % tpb, tpb)
    return pl.pallas_call(
        functools.partial(_qkv_kernel, n_xt),
        grid=(nt,),
        in_specs=[pl.BlockSpec((TM, D_MODEL), lambda i: (jnp.minimum(i, n_xt - 1), 0)),
                  pl.BlockSpec((TM, D_MODEL), lambda i: (jnp.maximum(i - n_xt, 0), 0)),
                  pl.BlockSpec((1, 6, D_MODEL), lambda i: (row(i), 0, 0)),
                  pl.BlockSpec((1, D_MODEL), lambda i: (0, 0)),
                  pl.BlockSpec(w.shape, lambda i: (0, 0)),
                  pl.BlockSpec((TM, 128), lambda i: (pos(i), 0)),
                  pl.BlockSpec((TM, 128), lambda i: (pos(i), 0))],
        out_specs=[pl.BlockSpec((TM, D_MODEL), lambda i: (i, 0)),
                   pl.BlockSpec((TM, kw), lambda i: (i, 0)),
                   pl.BlockSpec((TM, kw), lambda i: (i, 0))],
        out_shape=[jax.ShapeDtypeStruct((t, D_MODEL), BF16),
                   jax.ShapeDtypeStruct((t, kw), BF16),
                   jax.ShapeDtypeStruct((t, kw), BF16)],
        compiler_params=_cparams(("arbitrary",)),
        name="qkv_rope",
    )(x2d, c2d, mod, g, w, cos, sin)


def _rope_tables(seq):
    s = jnp.arange(seq)
    row = (s // GRID_W).astype(F32)
    col = (s % GRID_W).astype(F32)
    n_freq = HEAD_DIM // 4
    inv = jnp.exp(-math.log(ROPE_THETA) * jnp.arange(n_freq, dtype=F32) / n_freq)
    ar = row[:, None] * inv
    ac = col[:, None] * inv
    cos = jnp.concatenate([jnp.cos(ar), jnp.cos(ar), jnp.cos(ac), jnp.cos(ac)], axis=-1)
    sin = jnp.concatenate([-jnp.sin(ar), jnp.sin(ar), -jnp.sin(ac), jnp.sin(ac)], axis=-1)
    cos = jnp.concatenate([jnp.tile(cos, (1, 2)), jnp.ones((TM, 128), F32)], axis=0)
    sin = jnp.concatenate([jnp.tile(sin, (1, 2)), jnp.zeros((TM, 128), F32)], axis=0)
    return cos, sin


def _attn_kernel(nqb, sink_ref, q_ref, kp_ref, kc_ref, kn_ref, kx_ref,
                 vp_ref, vc_ref, vn_ref, vx_ref, o_ref):
    n = pl.program_id(1)
    qb = ATTN_BLOCK
    nloc = 3 * qb
    nkeys = nloc + kx_ref.shape[0]
    rows = GROUP * qb
    r = lax.broadcasted_iota(I32, (rows, nkeys), 0) % qb
    c = lax.broadcasted_iota(I32, (rows, nkeys), 1)
    d = c - r
    ok = (c < nloc) & (d >= 0) & (d <= 2 * WINDOW) & (n < nqb)
    ok = ok & ((n > 0) | (c >= qb)) & ((n < nqb - 1) | (c < 2 * qb))
    valid = ok | (c >= nloc)
    lane = lax.broadcasted_iota(I32, (qb, 128), 1)
    rsub = lax.broadcasted_iota(I32, (rows, 1), 0) // qb
    for j in range(N_KV_HEADS):
        ks = slice(128 * j, 128 * (j + 1))
        kall = jnp.concatenate([kp_ref[:, ks], kc_ref[:, ks], kn_ref[:, ks], kx_ref[:, ks]], axis=0)
        vall = jnp.concatenate([vp_ref[:, ks], vc_ref[:, ks], vn_ref[:, ks], vx_ref[:, ks]], axis=0)
        qs = []
        sink = jnp.zeros((rows, 1), F32)
        for g in range(GROUP):
            pair, half = divmod(g, 2)
            qp = q_ref[:, 256 * j + 128 * pair:256 * j + 128 * (pair + 1)]
            keep = (lane >= 64) if half else (lane < 64)
            qs.append(jnp.where(keep, qp, jnp.zeros_like(qp)))
            sink = jnp.where(rsub == g, sink_ref[GROUP * j + g] * LOG2E, sink)
        qst = jnp.concatenate(qs, axis=0)
        s = _dot_nt(qst, kall)
        s = jnp.where(valid, s, -jnp.inf)
        m = jnp.maximum(jnp.max(s, axis=-1, keepdims=True), sink)
        p = jnp.exp2(s - m)
        den = jnp.sum(p, axis=-1, keepdims=True) + jnp.exp2(sink - m)
        o = _dot(p.astype(BF16), vall) * (1.0 / den)
        for pair in range(GROUP // 2):
            o0 = o[(2 * pair) * qb:(2 * pair + 1) * qb]
            o1 = o[(2 * pair + 1) * qb:(2 * pair + 2) * qb]
            o_ref[:, 256 * j + 128 * pair:256 * j + 128 * (pair + 1)] = (
                jnp.where(lane < 64, o0, o1).astype(BF16))


def _attention(q, k, v, sink, nb, seq, ctx_len):
    t = q.shape[0]
    qb = ATTN_BLOCK
    nqb = seq // qb
    ncb = ctx_len // qb
    xq = nb * nqb
    kw = k.shape[1]
    qrow = lambda b, n: jnp.where(n < nqb, b * nqb + n, xq + b * ncb + (n - nqb))
    kprev = lambda b, n, s: (b * nqb + jnp.clip(n - 1, 0, nqb - 1), 0)
    kcur = lambda b, n, s: (b * nqb + jnp.minimum(n, nqb - 1), 0)
    knext = lambda b, n, s: (b * nqb + jnp.minimum(n + 1, nqb - 1), 0)
    kctx = lambda b, n, s: (nb * seq // ctx_len + b, 0)
    kspec = lambda f: pl.BlockSpec((qb, kw), f)
    xspec = pl.BlockSpec((ctx_len, kw), kctx)
    return pl.pallas_call(
        functools.partial(_attn_kernel, nqb),
        grid_spec=pltpu.PrefetchScalarGridSpec(
            num_scalar_prefetch=1,
            grid=(nb, nqb + ncb),
            in_specs=[pl.BlockSpec((qb, D_MODEL), lambda b, n, s: (qrow(b, n), 0)),
                      kspec(kprev), kspec(kcur), kspec(knext), xspec,
                      kspec(kprev), kspec(kcur), kspec(knext), xspec],
            out_specs=pl.BlockSpec((qb, D_MODEL), lambda b, n, s: (qrow(b, n), 0)),
        ),
        out_shape=jax.ShapeDtypeStruct((t, D_MODEL), BF16),
        compiler_params=_cparams(("arbitrary", "arbitrary")),
        name="window_attn",
    )(sink, q, k, k, k, k, v, v, v, v)


def _post_kernel(n_xt, m_ref, w_ref, x_ref, c_ref, mod_ref, g_ref, rw_ref, rb_ref,
                 x1_ref, h2_ref, pos_ref, gate_ref, cnt_ref):
    md = mod_ref[0]
    y = _dot(m_ref[...], w_ref[...])
    x = jnp.where(pl.program_id(0) < n_xt, x_ref[...], c_ref[...])
    x1 = x + md[2:3] * y
    x1_ref[...] = x1
    h2 = _norm_mod(x1, g_ref[...], md[3:4], md[4:5])
    h2b = h2.astype(BF16)
    h2_ref[...] = h2b

    ne = N_EXPERTS
    per = ne // N_GROUPS
    h2lo = (h2 - h2b.astype(F32)).astype(BF16)
    lg = _dot(h2b, rw_ref[...]) + _dot(h2lo, rw_ref[...])
    lg = lg + pltpu.roll(lg, ne, 1)
    logit = lg.T[0:ne]
    sc = jax.nn.sigmoid(logit)
    sel = sc + rb_ref[...]
    sub8 = lax.broadcasted_iota(I32, (per, TM), 0)
    gs = jnp.zeros((N_GROUPS, TM), F32)
    gi = lax.broadcasted_iota(I32, (N_GROUPS, TM), 0)
    for g in range(N_GROUPS):
        blk = sel[per * g:per * (g + 1)]
        m1 = jnp.max(blk, axis=0, keepdims=True)
        i1 = jnp.min(jnp.where(blk == m1, sub8, per), axis=0, keepdims=True)
        m2 = jnp.max(jnp.where(sub8 == i1, -jnp.inf, blk), axis=0, keepdims=True)
        gs = jnp.where(gi == g, m1 + m2, gs)
    grank = jnp.zeros((N_GROUPS, TM), F32)
    for g in range(N_GROUPS):
        v = gs[g:g + 1]
        grank = grank + jnp.where(gi > g, jnp.where(v >= gs, 1.0, 0.0), jnp.where(v > gs, 1.0, 0.0))
    ei = lax.broadcasted_iota(I32, (ne, TM), 0)
    gsel = jnp.zeros((ne, TM), F32)
    for g in range(N_GROUPS):
        gsel = jnp.where(ei // per == g, grank[g:g + 1], gsel)
    selm = jnp.where(gsel < TOPK_GROUPS, sel, -jnp.inf)
    eif = ei.astype(F32)
    s8 = jnp.zeros((ne, TM), F32)
    for _ in range(TOP_K):
        best = jnp.max(selm, axis=0, keepdims=True)
        first = jnp.min(jnp.where(selm == best, eif, float(ne)), axis=0, keepdims=True)
        hit = eif == first
        s8 = jnp.where(hit, 1.0, s8)
        selm = jnp.where(hit, -jnp.inf, selm)
    ws = s8 * sc
    gate = ws / jnp.sum(ws, axis=0, keepdims=True) * ROUTED_SCALE

    s8b = s8.astype(BF16)
    er = lax.broadcasted_iota(I32, (ne, ne), 0)
    ec = lax.broadcasted_iota(I32, (ne, ne), 1)
    lower = jnp.where(ec < er, 1.0, 0.0).astype(BF16)
    tr = lax.broadcasted_iota(I32, (TM, TM), 0)
    tc = lax.broadcasted_iota(I32, (TM, TM), 1)
    upper = jnp.where(tr < tc, 1.0, 0.0).astype(BF16)
    slot = _dot(lower, s8b)
    rank = _dot(s8b, upper)
    cnt = jnp.sum(s8, axis=1, keepdims=True)
    pc = jnp.floor((cnt + (CHUNK - 1)) * (1.0 / CHUNK))
    pcb = jnp.broadcast_to(pc, (ne, 128))
    cnt_ref[0] = pcb
    loc = _dot(lower, pcb.astype(BF16))[:, 0:1] * CHUNK
    prow = loc + rank
    k8 = lax.broadcasted_iota(I32, (TOP_K, TM), 0)
    pos_t = jnp.zeros((TOP_K, TM), F32)
    gate_t = jnp.zeros((TOP_K, TM), F32)
    for k in range(TOP_K):
        mk = jnp.where(slot == k, s8, 0.0)
        pos_t = jnp.where(k8 == k, jnp.sum(mk * prow, axis=0, keepdims=True), pos_t)
        gate_t = jnp.where(k8 == k, jnp.sum(mk * gate, axis=0, keepdims=True), gate_t)
    pos_ref[...] = pos_t.astype(I32)
    gate_ref[...] = gate_t


def _post(m, w, x2d, c2d, mod, g, rw_t, rb, nt, n_xt, tpb, nb):
    t = nt * TM
    kd = m.shape[1]
    row = lambda i: jnp.where(i < n_xt, i // tpb, nb)
    return pl.pallas_call(
        functools.partial(_post_kernel, n_xt),
        grid=(nt,),
        in_specs=[pl.BlockSpec((TM, kd), lambda i: (i, 0)),
                  pl.BlockSpec(w.shape, lambda i: (0, 0)),
                  pl.BlockSpec((TM, D_MODEL), lambda i: (jnp.minimum(i, n_xt - 1), 0)),
                  pl.BlockSpec((TM, D_MODEL), lambda i: (jnp.maximum(i - n_xt, 0), 0)),
                  pl.BlockSpec((1, 6, D_MODEL), lambda i: (row(i), 0, 0)),
                  pl.BlockSpec((1, D_MODEL), lambda i: (0, 0)),
                  pl.BlockSpec((D_MODEL, 2 * N_EXPERTS), lambda i: (0, 0)),
                  pl.BlockSpec((N_EXPERTS, 1), lambda i: (0, 0))],
        out_specs=[pl.BlockSpec((TM, D_MODEL), lambda i: (i, 0)),
                   pl.BlockSpec((TM, D_MODEL), lambda i: (i, 0)),
                   pl.BlockSpec((TOP_K, TM), lambda i: (0, i)),
                   pl.BlockSpec((TOP_K, TM), lambda i: (0, i)),
                   pl.BlockSpec((1, N_EXPERTS, 128), lambda i: (i, 0, 0))],
        out_shape=[jax.ShapeDtypeStruct((t, D_MODEL), F32),
                   jax.ShapeDtypeStruct((t, D_MODEL), BF16),
                   jax.ShapeDtypeStruct((TOP_K, t), I32),
                   jax.ShapeDtypeStruct((TOP_K, t), F32),
                   jax.ShapeDtypeStruct((nt, N_EXPERTS, 128), F32)],
        compiler_params=_cparams(("arbitrary",)),
        name="mix_out_route",
    )(m, w, x2d, c2d, mod, g, rw_t, rb)


def _moe_tables(pc, nblk_cap):
    nt = pc.shape[0]
    seg = jnp.sum(pc, axis=0)
    segb = (seg + CPB - 1) // CPB
    seg_end_b = jnp.cumsum(segb)
    seg_start = (seg_end_b - segb) * CPB
    goff = seg_start[None, :] + jnp.cumsum(pc, axis=0) - pc
    loc_end = jnp.cumsum(pc, axis=1)
    loc = loc_end - pc
    nch = loc_end[:, -1]
    nrb = (nch + CPH - 1) // CPH
    j = jnp.arange(NCH_CAP, dtype=I32)
    inrun = (j[None, :, None] >= loc[:, None, :]) & (j[None, :, None] < loc_end[:, None, :])
    shift = jnp.sum(jnp.where(inrun, (goff - loc)[:, None, :], 0), axis=-1)
    live = j[None, :] < nch[:, None]
    trash = nblk_cap * CPB + jnp.arange(nt, dtype=I32)[:, None] * SPARE + j[None, :] - MIN_RB * CPH
    dst = jnp.where(live, shift + j[None, :], trash).astype(I32)
    src = jnp.where(live, shift + j[None, :], 0).astype(I32)
    halves = (seg + CPH - 1) // CPH
    tl = jnp.arange(CPH - 1, dtype=I32)
    tail_len = halves * CPH - seg
    tails = jnp.where(tl[None, :] < tail_len[:, None], (seg_start + seg)[:, None] + tl[None, :], -1)
    hzero = jnp.where(halves % 2 == 1, seg_start + halves * CPH, -1)
    nused = seg_end_b[-1]
    blk = jnp.arange(nblk_cap, dtype=I32)
    bexp = jnp.sum((blk[:, None] >= seg_end_b[None, :]).astype(I32), axis=-1)
    last = jnp.sum((nused - 1 >= seg_end_b).astype(I32))
    bexp = jnp.where(blk < nused, bexp, last).astype(I32)
    onehot = (bexp[:, None] == jnp.arange(N_EXPERTS, dtype=I32)[None, :]).astype(I32)
    first_blk = seg_end_b - segb
    bhalf = jnp.sum(onehot * (halves - 2 * (blk[:, None] - first_blk[None, :])), axis=-1)
    bhalf = jnp.where(blk < nused, jnp.clip(bhalf, 1, 2), 0).astype(I32)
    eid = jnp.arange(N_EXPERTS, dtype=I32)
    owns = segb > 0
    ordinal = jnp.cumsum(owns.astype(I32)) - 1
    later = jnp.where(owns[None, :] & (eid[None, :] > eid[:, None]), eid[None, :], N_EXPERTS)
    nxt_e = jnp.min(later, axis=1)
    nxt_e = jnp.where(nxt_e < N_EXPERTS, nxt_e, -1)
    bpar = jnp.sum(onehot * (ordinal % 2)[None, :], axis=-1).astype(I32)
    bnext = jnp.sum(onehot * nxt_e[None, :], axis=-1).astype(I32)
    return dict(nrb=nrb.astype(I32), dst=dst.reshape(-1), src=src.reshape(-1),
                tails=tails.reshape(-1).astype(I32), hzero=hzero.astype(I32),
                nused=nused.reshape(1).astype(I32), bexp=bexp, bhalf=bhalf, bpar=bpar, bnext=bnext)


def _rows_copy(src, dst, sem):
    return pltpu.make_async_copy(src, dst, sem)


def _row_onehot(riota, pos, base, value):
    p = jnp.zeros((TM, TM), BF16)
    for k in range(TOP_K):
        d = jnp.clip(pos[k:k + 1] - base, -1, TM).astype(F32).astype(BF16)
        v = 1.0 if value is None else value[k:k + 1]
        p = jnp.where(riota == d, v, p)
    return p


def _dispatch_kernel(nt, nrb_ref, dst_ref, tails_ref, hzero_ref, h2_ref, pos_ref, xs_hbm, buf, sem):
    i = pl.program_id(0)
    slot = i % 2

    def drain(tile, sl):
        def body(rb, carry):
            _rows_copy(buf.at[sl, pl.ds(0, TM)], xs_hbm.at[pl.ds(0, TM)], sem.at[sl]).wait()
            return carry
        lax.fori_loop(0, jnp.maximum(nrb_ref[tile], EAGER_RB), body, 0)

    @pl.when((i >= 2) & (i < nt))
    def _():
        drain(i - 2, slot)

    @pl.when(i < nt)
    def _():
        h2 = h2_ref[...]
        pos = pos_ref[...]
        riota = lax.broadcasted_iota(I32, (TM, TM), 0).astype(F32).astype(BF16)
        nrb = nrb_ref[i]

        def block(b):
            p = _row_onehot(riota, pos, b * TM, None)
            buf[slot, b * TM:(b + 1) * TM, :] = _dot(p, h2).astype(BF16)
            for c in range(CPH):
                d = dst_ref[i * NCH_CAP + b * CPH + c]
                _rows_copy(buf.at[slot, pl.ds(b * TM + c * CHUNK, CHUNK)],
                           xs_hbm.at[pl.ds(pl.multiple_of(d * CHUNK, CHUNK), CHUNK)],
                           sem.at[slot]).start()

        for b in range(EAGER_RB):
            block(b)
        for b in range(EAGER_RB, NRB_CAP):
            pl.when(b < nrb)(functools.partial(block, b))

    @pl.when(i == nt)
    def _():
        drain(nt - 2, nt % 2)
        drain(nt - 1, (nt - 1) % 2)
        buf[0, pl.ds(0, HB), :] = jnp.zeros((HB, D_MODEL), BF16)
        ntail = N_EXPERTS * (CPH - 1)

        def tail(j, carry):
            d = tails_ref[j]

            @pl.when(d >= 0)
            def _():
                _rows_copy(buf.at[0, pl.ds(0, CHUNK)],
                           xs_hbm.at[pl.ds(pl.multiple_of(d * CHUNK, CHUNK), CHUNK)], sem.at[0]).start()
            return carry

        lax.fori_loop(0, ntail, tail, 0)

        def half(e, carry):
            d = hzero_ref[e]

            @pl.when(d >= 0)
            def _():
                _rows_copy(buf.at[0, pl.ds(0, HB)],
                           xs_hbm.at[pl.ds(pl.multiple_of(d * CHUNK, CHUNK), HB)], sem.at[1]).start()
            return carry

        lax.fori_loop(0, N_EXPERTS, half, 0)

        def tail_wait(j, carry):
            @pl.when(tails_ref[j] >= 0)
            def _():
                _rows_copy(buf.at[0, pl.ds(0, CHUNK)], xs_hbm.at[pl.ds(0, CHUNK)], sem.at[0]).wait()
            return carry

        lax.fori_loop(0, ntail, tail_wait, 0)

        def half_wait(e, carry):
            @pl.when(hzero_ref[e] >= 0)
            def _():
                _rows_copy(buf.at[0, pl.ds(0, HB)], xs_hbm.at[pl.ds(0, HB)], sem.at[1]).wait()
            return carry

        lax.fori_loop(0, N_EXPERTS, half_wait, 0)


def _dispatch(tb, h2, pos_t, nt, nblk_cap):
    last = nt - 1
    return pl.pallas_call(
        functools.partial(_dispatch_kernel, nt),
        grid_spec=pltpu.PrefetchScalarGridSpec(
            num_scalar_prefetch=4,
            grid=(nt + 1,),
            in_specs=[pl.BlockSpec((TM, D_MODEL), lambda i, *_: (jnp.minimum(i, last), 0)),
                      pl.BlockSpec((TOP_K, TM), lambda i, *_: (0, jnp.minimum(i, last)))],
            out_specs=pl.BlockSpec(memory_space=pl.ANY),
            scratch_shapes=[pltpu.VMEM((2, RCAP, D_MODEL), BF16), pltpu.SemaphoreType.DMA((2,))],
        ),
        out_shape=jax.ShapeDtypeStruct((nblk_cap * BM + nt * SPARE * CHUNK, D_MODEL), BF16),
        compiler_params=_cparams(("arbitrary",)),
        name="moe_dispatch",
    )(tb["nrb"], tb["dst"], tb["tails"], tb["hzero"], h2, pos_t)


RING = 3


def _expert_kernel(layer, bexp_ref, bhalf_ref, bpar_ref, bnext_ref, nused_ref,
                   xs_hbm, wg_hbm, wu_hbm, wd_hbm, ys_hbm,
                   xb, yb, stg_g, stg_u, stg_d, wgu, wdn, xsem, ysem, wsem):
    i = pl.program_id(0)
    nused = nused_ref[0]

    def x_copy(blk, sl):
        return _rows_copy(xs_hbm.at[pl.ds(pl.multiple_of(blk * BM, BM), BM)], xb.at[sl], xsem.at[sl])

    def y_copy(blk, sl):
        return _rows_copy(yb.at[sl], ys_hbm.at[pl.ds(pl.multiple_of(blk * BM, BM), BM)], ysem.at[sl])

    def w_copies(ex, sl):
        return (_rows_copy(wg_hbm.at[layer, ex], stg_g.at[sl], wsem.at[sl]),
                _rows_copy(wu_hbm.at[layer, ex], stg_u.at[sl], wsem.at[sl]),
                _rows_copy(wd_hbm.at[layer, ex], stg_d.at[sl], wsem.at[sl]))

    @pl.when(i == 0)
    def _():
        x_copy(0, 0).start()

        @pl.when(nused > 1)
        def _():
            x_copy(1, 1).start()
        for c in w_copies(bexp_ref[0], bpar_ref[0]):
            c.start()

    @pl.when(i < nused)
    def _():
        sl = i % RING

        @pl.when(i + 2 < nused)
        def _():
            x_copy(i + 2, (i + 2) % RING).start()

        e = bexp_ref[i]
        par = bpar_ref[i]

        @pl.when((i == 0) | (e != bexp_ref[jnp.maximum(i - 1, 0)]))
        def _():
            for c in w_copies(e, par):
                c.wait()
            wgu[:, :D_EXPERT] = stg_g[par].astype(BF16)
            wgu[:, D_EXPERT:] = stg_u[par].astype(BF16)
            wdn[...] = stg_d[par].astype(BF16)
            nxt = bnext_ref[i]

            @pl.when(nxt >= 0)
            def _():
                for c in w_copies(nxt, 1 - par):
                    c.start()

        x_copy(i, sl).wait()

        @pl.when(i >= RING)
        def _():
            y_copy(i - RING, sl).wait()

        def chain(r0):
            gu = _dot(xb[sl, pl.ds(r0, HB), :], wgu[...])
            h = _silu(gu[:, :D_EXPERT]) * gu[:, D_EXPERT:]
            yb[sl, pl.ds(r0, HB), :] = _dot(h.astype(BF16), wdn[...]).astype(BF16)

        nh = bhalf_ref[i]

        @pl.when(nh == 2)
        def _():
            chain(0)
            chain(HB)

        @pl.when(nh == 1)
        def _():
            chain(0)
            yb[sl, pl.ds(HB, HB), :] = jnp.zeros((HB, D_MODEL), BF16)

        y_copy(i, sl).start()

        @pl.when(i == nused - 1)
        def _():
            for back in range(RING):
                @pl.when(i - back >= 0)
                def _():
                    y_copy(i - back, (i - back) % RING).wait()


def _experts(tb, xs, w_gate, w_up, w_down, layer, nblk_cap):
    hbm = pl.BlockSpec(memory_space=pl.ANY)
    return pl.pallas_call(
        functools.partial(_expert_kernel, layer),
        grid_spec=pltpu.PrefetchScalarGridSpec(
            num_scalar_prefetch=5,
            grid=(nblk_cap,),
            in_specs=[hbm, hbm, hbm, hbm],
            out_specs=hbm,
            scratch_shapes=[pltpu.VMEM((RING, BM, D_MODEL), BF16),
                            pltpu.VMEM((RING, BM, D_MODEL), BF16),
                            pltpu.VMEM((2, D_MODEL, D_EXPERT), F32),
                            pltpu.VMEM((2, D_MODEL, D_EXPERT), F32),
                            pltpu.VMEM((2, D_EXPERT, D_MODEL), F32),
                            pltpu.VMEM((D_MODEL, 2 * D_EXPERT), BF16),
                            pltpu.VMEM((D_EXPERT, D_MODEL), BF16),
                            pltpu.SemaphoreType.DMA((RING,)),
                            pltpu.SemaphoreType.DMA((RING,)),
                            pltpu.SemaphoreType.DMA((2,))],
        ),
        out_shape=jax.ShapeDtypeStruct((nblk_cap * BM, D_MODEL), BF16),
        compiler_params=_cparams(("arbitrary",)),
        name="moe_experts",
    )(tb["bexp"], tb["bhalf"], tb["bpar"], tb["bnext"], tb["nused"], xs, w_gate, w_up, w_down)


def _combine_kernel(final, nt, nrb_ref, src_ref, ys_hbm, pos_ref, gate_ref, h2_ref, x1_ref,
                    mod_ref, sgu_ref, sdn_ref, fg_ref, o_ref, buf, acc, sem):
    i = pl.program_id(0)
    slot = i % 2
    nxt = jnp.minimum(i + 1, nt - 1)
    nrb = nrb_ref[i]
    nrb_next = nrb_ref[nxt]

    def fetch(tile, sl, b):
        for c in range(CPH):
            s = src_ref[tile * NCH_CAP + b * CPH + c]
            _rows_copy(ys_hbm.at[pl.ds(pl.multiple_of(s * CHUNK, CHUNK), CHUNK)],
                       buf.at[sl, pl.ds(b * TM + c * CHUNK, CHUNK)], sem.at[sl]).start()

    def wait_blocks(sl, n):
        def body(rb, carry):
            _rows_copy(ys_hbm.at[pl.ds(0, TM)], buf.at[sl, pl.ds(0, TM)], sem.at[sl]).wait()
            return carry
        lax.fori_loop(0, jnp.maximum(n, EAGER_RB), body, 0)

    @pl.when(i == 0)
    def _():
        for b in range(EAGER_RB):
            fetch(0, 0, b)
        for b in range(EAGER_RB, NRB_CAP):
            pl.when(b < nrb)(functools.partial(fetch, 0, 0, b))

    wait_blocks(slot, nrb)

    pos = pos_ref[...]
    gate = gate_ref[...].astype(BF16)
    riota = lax.broadcasted_iota(I32, (TM, TM), 0).astype(F32).astype(BF16)
    tlhs = (((0,), (0,)), ((), ()))

    for b in range(EAGER_RB):
        fetch(nxt, 1 - slot, b)
    gu = _dot(h2_ref[...], sgu_ref[...])
    hs = _silu(gu[:, :D_EXPERT]) * gu[:, D_EXPERT:]
    shared = _dot(hs.astype(BF16), sdn_ref[...])
    p = jnp.concatenate([_row_onehot(riota, pos, b * TM, gate) for b in range(EAGER_RB)], axis=0)
    acc[...] = shared + lax.dot_general(p, buf[slot, 0:EAGER_RB * TM, :], tlhs,
                                        preferred_element_type=F32)
    for b in range(EAGER_RB, NRB_CAP):
        pl.when(b < nrb_next)(functools.partial(fetch, nxt, 1 - slot, b))

        @pl.when(b < nrb)
        def _():
            pb = _row_onehot(riota, pos, b * TM, gate)
            acc[...] += lax.dot_general(pb, buf[slot, b * TM:(b + 1) * TM, :], tlhs,
                                        preferred_element_type=F32)

    @pl.when(i == nt - 1)
    def _():
        wait_blocks(1 - slot, nrb_next)

    x2 = x1_ref[...] + mod_ref[0][5:6] * acc[...]
    if final:
        x2 = x2 * lax.rsqrt(jnp.mean(x2 * x2, axis=-1, keepdims=True) + RMS_EPS) * fg_ref[...]
    o_ref[...] = x2


def _combine(tb, ys, pos, gate, h2, x1, mod, sgu, sdn, fg, nt, n_xt, tpb, nb, final):
    t = nt * TM
    row = lambda i, *_: (jnp.where(i < n_xt, i // tpb, nb), 0, 0)
    tile = lambda i, *_: (i, 0)
    slots = lambda i, *_: (0, i)
    const = lambda i, *_: (0, 0)
    return pl.pallas_call(
        functools.partial(_combine_kernel, final, nt),
        grid_spec=pltpu.PrefetchScalarGridSpec(
            num_scalar_prefetch=2,
            grid=(nt,),
            in_specs=[pl.BlockSpec(memory_space=pl.ANY),
                      pl.BlockSpec((TOP_K, TM), slots),
                      pl.BlockSpec((TOP_K, TM), slots),
                      pl.BlockSpec((TM, D_MODEL), tile),
                      pl.BlockSpec((TM, D_MODEL), tile),
                      pl.BlockSpec((1, 6, D_MODEL), row),
                      pl.BlockSpec(sgu.shape, const),
                      pl.BlockSpec(sdn.shape, const),
                      pl.BlockSpec((1, D_MODEL), const)],
            out_specs=pl.BlockSpec((TM, D_MODEL), tile),
            scratch_shapes=[pltpu.VMEM((2, RCAP, D_MODEL), BF16),
                            pltpu.VMEM((TM, D_MODEL), F32),
                            pltpu.SemaphoreType.DMA((2,))],
        ),
        out_shape=jax.ShapeDtypeStruct((t, D_MODEL), F32),
        compiler_params=_cparams(("arbitrary",)),
        name="moe_combine",
    )(tb["nrb"], tb["src"], ys, pos, gate, h2, x1, mod, sgu, sdn, fg)


def _moe(h2, pos_t, gate_t, cnt, x1, mod, w_gate, w_up, w_down, layer, sgu, sdn, fg,
         nt, n_xt, tpb, nb, final):
    nblk_cap = (TOP_K * nt * TM + (CHUNK - 1) * N_EXPERTS * nt) // BM + N_EXPERTS
    tb = _moe_tables(cnt[:, :, 0].astype(I32), nblk_cap)
    xs = _dispatch(tb, h2, pos_t, nt, nblk_cap)
    ys = _experts(tb, xs, w_gate, w_up, w_down, layer, nblk_cap)
    return _combine(tb, ys, pos_t, gate_t, h2, x1, mod, sgu, sdn, fg, nt, n_xt, tpb, nb, final)


def _lru_in_kernel(x_ref, mod_ref, g_ref, w_ref, gate_ref, u_ref):
    m = mod_ref[0]
    h = _norm_mod(x_ref[...], g_ref[...], m[0:1], m[1:2])
    p = _dot(h.astype(BF16), w_ref[...])
    z = p[:, :D_RNN]
    cdf = 0.5 * (1.0 + jnp.tanh(math.sqrt(2.0 / math.pi) * (z + 0.044715 * (z * z * z))))
    gate_ref[...] = (z * cdf).astype(BF16)
    u_ref[...] = p[:, D_RNN:]


def _lru_in(xall, mod, g, w, n_xt, tpb, nb):
    t = xall.shape[0]
    nt = t // TM
    row = lambda i: jnp.where(i < n_xt, i // tpb, nb)
    return pl.pallas_call(
        _lru_in_kernel,
        grid=(nt,),
        in_specs=[pl.BlockSpec((TM, D_MODEL), lambda i: (i, 0)),
                  pl.BlockSpec((1, 6, D_MODEL), lambda i: (row(i), 0, 0)),
                  pl.BlockSpec((1, D_MODEL), lambda i: (0, 0)),
                  pl.BlockSpec(w.shape, lambda i: (0, 0))],
        out_specs=[pl.BlockSpec((TM, D_RNN), lambda i: (i, 0)),
                   pl.BlockSpec((TM, D_RNN), lambda i: (i, 0))],
        out_shape=[jax.ShapeDtypeStruct((t, D_RNN), BF16),
                   jax.ShapeDtypeStruct((t, D_RNN), F32)],
        compiler_params=_cparams(("arbitrary",)),
        name="lru_in",
    )(xall, mod, g, w)


def _sigmoid(x):
    return 0.5 * (1.0 + jnp.tanh(0.5 * x))


def _gate_windows():
    wins = []
    for c0 in range(0, D_RNN, 256):
        c1 = min(c0 + 256, D_RNN)
        k0 = (c0 // LRU_BLOCK_W) * LRU_BLOCK_W
        k1 = ((c1 - 1) // LRU_BLOCK_W + 1) * LRU_BLOCK_W
        wins.append((c0, c1, (k0 // 128) * 128, min(-(-k1 // 128) * 128, D_RNN)))
    return wins


def _block_diag_dot(ub, w_ref):
    return jnp.concatenate([_dot(ub[:, k0:k1], w_ref[k0:k1, c0:c1]) for c0, c1, k0, k1 in _gate_windows()],
                           axis=1)


def _lru_sweep_kernel(rev, mix, ns, u_ref, up_ref, un_ref, cw_ref, cb_ref, wa_ref, wx_ref,
                      ba_ref, bx_ref, lam_ref, h0_ref, *rest):
    if mix:
        hf_ref, gate_ref, o_ref, a_s, b_s, carry = rest
    else:
        o_ref, a_s, b_s, carry = rest
    s = pl.program_id(1)
    ss = (ns - 1 - s) if rev else s

    @pl.when(s == 0)
    def _():
        carry[...] = jnp.broadcast_to(h0_ref[0], (8, D_RNN))

    prev = jnp.where(ss > 0, up_ref[...], 0.0)
    nxt = jnp.where(ss < ns - 1, un_ref[...], 0.0)
    ext = jnp.concatenate([prev, u_ref[...], nxt], axis=0)
    n_ext = TM + 16
    cw = cw_ref[...]
    u = cb_ref[...]
    for j in range(CONV_W):
        shift = (CONV_LEFT - j) % n_ext
        tap = ext if shift == 0 else pltpu.roll(ext, shift, 0)
        u = u + cw[j:j + 1] * tap[8:8 + TM]

    ub = u.astype(BF16)
    r = _sigmoid(_block_diag_dot(ub, wa_ref) + ba_ref[...])
    ig = _sigmoid(_block_diag_dot(ub, wx_ref) + bx_ref[...])
    nl = -lam_ref[...]
    softplus = jnp.maximum(nl, 0.0) + jnp.log1p(jnp.exp(-jnp.abs(nl)))
    y = LRU_C * r * softplus
    a = jnp.exp(-y)
    a_s[...] = a
    b_s[...] = jnp.sqrt(jnp.tanh(y) * (1.0 + a * a)) * (ig * u)

    sub = lax.broadcasted_iota(I32, (8, D_RNN), 0)
    h = carry[...]
    ng = TM // 8
    for gi in range(ng):
        r0 = 8 * ((ng - 1 - gi) if rev else gi)
        a = a_s[r0:r0 + 8, :]
        b = b_s[r0:r0 + 8, :]
        for sh in (1, 2, 4):
            amt = (8 - sh) if rev else sh
            keep = (sub < 8 - sh) if rev else (sub >= sh)
            a_sh = pltpu.roll(a, amt, 0)
            b_sh = pltpu.roll(b, amt, 0)
            b = jnp.where(keep, a * b_sh + b, b)
            a = jnp.where(keep, a * a_sh, a)
        hg = a * h + b
        edge = hg[0:1] if rev else hg[7:8]
        h = jnp.broadcast_to(edge, (8, D_RNN))
        if mix:
            o_ref[r0:r0 + 8, :] = ((hf_ref[r0:r0 + 8, :] + hg)
                                   * gate_ref[r0:r0 + 8, :].astype(F32)).astype(BF16)
        else:
            o_ref[r0:r0 + 8, :] = hg
    carry[...] = h


def _lru_sweep(u_all, row0, nb, seq, cw, cb, wa, wx, ba, bx, lam, h0, rev, hf=None, gate=None):
    ns = seq // TM
    blk0 = row0 // TM
    mix = hf is not None
    sidx = (lambda s: ns - 1 - s) if rev else (lambda s: s)
    cur = lambda b, s: (blk0 + b * ns + sidx(s), 0)
    out = lambda b, s: (b * ns + sidx(s), 0)
    h8 = TM // 8
    nblk8 = u_all.shape[0] // 8
    prv = lambda b, s: (jnp.maximum((blk0 + b * ns + sidx(s)) * h8 - 1, 0), 0)
    nxt = lambda b, s: (jnp.minimum((blk0 + b * ns + sidx(s) + 1) * h8, nblk8 - 1), 0)
    vec = lambda b, s: (0, 0)
    in_specs = [pl.BlockSpec((TM, D_RNN), cur),
                pl.BlockSpec((8, D_RNN), prv),
                pl.BlockSpec((8, D_RNN), nxt),
                pl.BlockSpec((CONV_W, D_RNN), vec),
                pl.BlockSpec((1, D_RNN), vec),
                pl.BlockSpec((D_RNN, D_RNN), vec),
                pl.BlockSpec((D_RNN, D_RNN), vec),
                pl.BlockSpec((1, D_RNN), vec),
                pl.BlockSpec((1, D_RNN), vec),
                pl.BlockSpec((1, D_RNN), vec),
                pl.BlockSpec((1, 1, D_RNN), lambda b, s: (b, 0, 0))]
    args = [u_all, u_all, u_all, cw, cb, wa, wx, ba, bx, lam, h0]
    if mix:
        in_specs += [pl.BlockSpec((TM, D_RNN), out), pl.BlockSpec((TM, D_RNN), cur)]
        args += [hf, gate]
    return pl.pallas_call(
        functools.partial(_lru_sweep_kernel, rev, mix, ns),
        grid=(nb, ns),
        in_specs=in_specs,
        out_specs=pl.BlockSpec((TM, D_RNN), out),
        out_shape=jax.ShapeDtypeStruct((nb * seq, D_RNN), BF16 if mix else F32),
        scratch_shapes=[pltpu.VMEM((TM, D_RNN), F32), pltpu.VMEM((TM, D_RNN), F32),
                        pltpu.VMEM((8, D_RNN), F32)],
        compiler_params=_cparams(("arbitrary", "arbitrary")),
        name="lru_sweep_rev" if rev else "lru_sweep_fwd",
    )(*args)


def _block_diag(w):
    rows = jnp.tile(w.reshape(D_RNN, LRU_BLOCK_W), (1, LRU_BLOCKS))
    blk = jnp.arange(D_RNN, dtype=I32) // LRU_BLOCK_W
    return jnp.where(blk[:, None] == blk[None, :], rows, 0.0).astype(BF16)


def kernel(x, c, ctx, c_ctx, ada_w, ada_b, norm1_g, norm2_g, attn_w_qkv, attn_w_o, attn_sink,
           lru_w_in, lru_conv_w, lru_conv_b, lru_wa, lru_ba, lru_wx, lru_bx, lru_lam, lru_w_out,
           moe_router_w, moe_router_bias, moe_w_gate, moe_w_up, moe_w_down,
           shared_w_gate, shared_w_up, shared_w_down, final_g):
    nb, seq, d = x.shape
    ctx_len = ctx.shape[1]
    assert d == D_MODEL and seq % TM == 0 and ctx_len == TM and nb < 8
    tx = nb * seq
    tall = tx + nb * ctx_len
    tpb = seq // TM
    n_xt = tx // TM
    nt_all = tall // TM

    cvec = jnp.zeros((8, d), F32).at[:nb].set(c).at[nb].set(c_ctx)
    mod = _adaln(cvec, ada_w, ada_b).reshape(DEPTH, 8, 6, d)
    x2d = x.reshape(tx, d)
    c2d = ctx.reshape(nb * ctx_len, d)
    row = lambda v: v.reshape(1, -1)
    shared = lambda i: (jnp.concatenate([shared_w_gate[i], shared_w_up[i]], axis=1).astype(BF16),
                        shared_w_down[i].astype(BF16))

    def router(i):
        hi = moe_router_w[i].astype(BF16)
        lo = (moe_router_w[i] - hi.astype(F32)).astype(BF16)
        return jnp.concatenate([hi, lo], axis=1)

    qd = N_HEADS * HEAD_DIM
    kd = N_KV_HEADS * HEAD_DIM
    wq = attn_w_qkv[0][:, :qd]
    wk = attn_w_qkv[0][:, qd:qd + kd].reshape(d, N_KV_HEADS, 1, HEAD_DIM)
    wv = attn_w_qkv[0][:, qd + kd:].reshape(d, N_KV_HEADS, 1, HEAD_DIM)
    dup = lambda w: jnp.broadcast_to(w, (d, N_KV_HEADS, 2, HEAD_DIM)).reshape(d, 2 * kd)
    wqkv = jnp.concatenate([wq, dup(wk), dup(wv)], axis=1).astype(BF16)
    cos, sin = _rope_tables(seq)
    q, k, v = _qkv(x2d, c2d, mod[0], row(norm1_g[0]), wqkv, cos, sin, n_xt, tpb, nb)
    o = _attention(q, k, v, attn_sink[0], nb, seq, ctx_len)
    x1, h2, pos_t, gate_t, cnt = _post(
        o, attn_w_o[0].astype(BF16), x2d, c2d, mod[0], row(norm2_g[0]),
        router(0), moe_router_bias[0].reshape(-1, 1), nt_all, n_xt, tpb, nb)
    sgu, sdn = shared(0)
    xall = _moe(h2, pos_t, gate_t, cnt, x1, mod[0], moe_w_gate, moe_w_up, moe_w_down, 0,
                sgu, sdn, row(final_g), nt_all, n_xt, tpb, nb, False)

    gate, u_pre = _lru_in(xall, mod[1], row(norm1_g[1]), lru_w_in[0].astype(BF16), n_xt, tpb, nb)
    cw, cb = lru_conv_w[0], row(lru_conv_b[0])
    hdir = []
    for dr, rev in ((0, False), (1, True)):
        wa = _block_diag(lru_wa[0, dr])
        wx = _block_diag(lru_wx[0, dr])
        prm = (cw, cb, wa, wx, row(lru_ba[0, dr]), row(lru_bx[0, dr]), row(lru_lam[0, dr]))
        zero = jnp.zeros((nb, 1, D_RNN), F32)
        hc = _lru_sweep(u_pre, tx, nb, ctx_len, *prm, zero, rev).reshape(nb, ctx_len, D_RNN)
        h0 = hc[:, 0:1] if rev else hc[:, ctx_len - 1:ctx_len]
        if not rev:
            hdir.append(_lru_sweep(u_pre, 0, nb, seq, *prm, h0, rev))
        else:
            mixed = _lru_sweep(u_pre, 0, nb, seq, *prm, h0, rev, hf=hdir[0], gate=gate)
    x1, h2, pos_t, gate_t, cnt = _post(
        mixed, lru_w_out[0].astype(BF16), xall, xall, mod[1], row(norm2_g[1]),
        router(1), moe_router_bias[1].reshape(-1, 1), n_xt, n_xt, tpb, nb)
    sgu, sdn = shared(1)
    out = _moe(h2, pos_t, gate_t, cnt, x1, mod[1], moe_w_gate, moe_w_up, moe_w_down, 1,
               sgu, sdn, row(final_g), n_xt, n_xt, tpb, nb, True)
    return out.reshape(nb, seq, d)
```

```python
import functools
import math

import jax
import jax.numpy as jnp
from jax import lax
from jax.experimental import pallas as pl
from jax.experimental.pallas import tpu as pltpu

D_MODEL = 1024
DEPTH = 2
GRID_W = 64
HEAD_DIM = 64
N_HEADS = 16
N_KV_HEADS = 4
GROUP = N_HEADS // N_KV_HEADS
WINDOW = 128
ATTN_BLOCK = 128
ROPE_THETA = 10000.0
D_RNN = 1280
LRU_BLOCKS = 16
LRU_BLOCK_W = D_RNN // LRU_BLOCKS
CONV_W = 4
CONV_LEFT = 2
LRU_C = 8.0
N_EXPERTS = 64
TOP_K = 8
N_GROUPS = 8
TOPK_GROUPS = 4
D_EXPERT = 256
ROUTED_SCALE = 2.5
RMS_EPS = 1e-6

F32 = jnp.float32
BF16 = jnp.bfloat16
I32 = jnp.int32
HIGHEST = lax.Precision.HIGHEST
LOG2E = math.log2(math.e)

TM = 256
RT = 2
CHUNK = 16
HB = 256
NSUB = 4
BM = NSUB * HB
CPB = BM // CHUNK
CPH = HB // CHUNK
RCAP = ((TOP_K * TM + (CHUNK - 1) * N_EXPERTS + TM - 1) // TM) * TM
NCH_CAP = RCAP // CHUNK
NRB_CAP = RCAP // TM
MIN_RB = TOP_K
EAGER_RB = 10
SPARE = (NRB_CAP - MIN_RB) * CPH
VMEM_LIMIT = 56 * 1024 * 1024


def _cparams(sem):
    return pltpu.CompilerParams(dimension_semantics=sem, vmem_limit_bytes=VMEM_LIMIT)


def _silu(x):
    return x * jax.nn.sigmoid(x)


def _dot(a, b):
    return jnp.dot(a, b, preferred_element_type=F32)


def _dot_nt(a, b, precision=None):
    return lax.dot_general(a, b, (((1,), (1,)), ((), ())), precision=precision,
                           preferred_element_type=F32)


def _adaln_kernel(c_ref, w_ref, b_ref, o_ref):
    s = _silu(c_ref[...])
    o_ref[0] = jnp.dot(s, w_ref[0], precision=HIGHEST, preferred_element_type=F32) + b_ref[0]


def _adaln(cvec, ada_w, ada_b):
    nb = 1536
    d6 = 6 * D_MODEL
    return pl.pallas_call(
        _adaln_kernel,
        grid=(DEPTH, d6 // nb),
        in_specs=[pl.BlockSpec((8, D_MODEL), lambda l, j: (0, 0)),
                  pl.BlockSpec((1, D_MODEL, nb), lambda l, j: (l, 0, j)),
                  pl.BlockSpec((1, 1, nb), lambda l, j: (l, 0, j))],
        out_specs=pl.BlockSpec((1, 8, nb), lambda l, j: (l, 0, j)),
        out_shape=jax.ShapeDtypeStruct((DEPTH, 8, d6), F32),
        compiler_params=_cparams(("arbitrary", "arbitrary")),
        name="adaln",
    )(cvec, ada_w, ada_b.reshape(DEPTH, 1, d6))


def _norm_mod(x, g, shift, scale):
    xn = x * lax.rsqrt(jnp.mean(x * x, axis=-1, keepdims=True) + RMS_EPS) * g
    return xn * (1.0 + scale) + shift


def _qkv_kernel(n_xt, x_ref, c_ref, mod_ref, g_ref, w_ref, cos_ref, sin_ref, q_ref, k_ref, v_ref):
    m = mod_ref[0]
    x = jnp.where(pl.program_id(0) < n_xt, x_ref[...], c_ref[...])
    h = _norm_mod(x, g_ref[...], m[0:1], m[1:2])
    p = _dot(h.astype(BF16), w_ref[...])
    cos = cos_ref[...]
    sin = sin_ref[...]
    lane = lax.broadcasted_iota(I32, (TM, 128), 1)
    second = (lane & 16) != 0
    scale = HEAD_DIM ** -0.5 * LOG2E
    nq = N_HEADS * HEAD_DIM // 128
    nk = 2 * N_KV_HEADS * HEAD_DIM // 128
    for c in range(nq + nk):
        blk = p[:, 128 * c:128 * (c + 1)]
        partner = jnp.where(second, pltpu.roll(blk, 16, 1), pltpu.roll(blk, 112, 1))
        r = blk * cos + partner * sin
        if c < nq:
            q_ref[:, 128 * c:128 * (c + 1)] = (r * scale).astype(BF16)
        else:
            k_ref[:, 128 * (c - nq):128 * (c - nq + 1)] = r.astype(BF16)
    v_ref[...] = p[:, 128 * (nq + nk):].astype(BF16)


def _qkv(x2d, c2d, mod, g, w, cos, sin, n_xt, tpb, nb):
    t = x2d.shape[0] + c2d.shape[0]
    nt = t // TM
    kw = 2 * N_KV_HEADS * HEAD_DIM
    row = lambda i: jnp.where(i < n_xt, i // tpb, nb)
    pos = lambda i: jnp.where(i < n_xt, i % tpb, tpb)
    return pl.pallas_call(
        functools.partial(_qkv_kernel, n_xt),
        grid=(nt,),
        in_specs=[pl.BlockSpec((TM, D_MODEL), lambda i: (jnp.minimum(i, n_xt - 1), 0)),
                  pl.BlockSpec((TM, D_MODEL), lambda i: (jnp.maximum(i - n_xt, 0), 0)),
                  pl.BlockSpec((1, 6, D_MODEL), lambda i: (row(i), 0, 0)),
                  pl.BlockSpec((1, D_MODEL), lambda i: (0, 0)),
                  pl.BlockSpec(w.shape, lambda i: (0, 0)),
                  pl.BlockSpec((TM, 128), lambda i: (pos(i), 0)),
                  pl.BlockSpec((TM, 128), lambda i: (pos(i), 0))],
        out_specs=[pl.BlockSpec((TM, D_MODEL), lambda i: (i, 0)),
                   pl.BlockSpec((TM, kw), lambda i: (i, 0)),
                   pl.BlockSpec((TM, kw), lambda i: (i, 0))],
        out_shape=[jax.ShapeDtypeStruct((t, D_MODEL), BF16),
                   jax.ShapeDtypeStruct((t, kw), BF16),
                   jax.ShapeDtypeStruct((t, kw), BF16)],
        compiler_params=_cparams(("arbitrary",)),
        name="qkv_rope",
    )(x2d, c2d, mod, g, w, cos, sin)


def _rope_tables(seq):
    s = jnp.arange(seq)
    row = (s // GRID_W).astype(F32)
    col = (s % GRID_W).astype(F32)
    n_freq = HEAD_DIM // 4
    inv = jnp.exp(-math.log(ROPE_THETA) * jnp.arange(n_freq, dtype=F32) / n_freq)
    ar = row[:, None] * inv
    ac = col[:, None] * inv
    cos = jnp.concatenate([jnp.cos(ar), jnp.cos(ar), jnp.cos(ac), jnp.cos(ac)], axis=-1)
    sin = jnp.concatenate([-jnp.sin(ar), jnp.sin(ar), -jnp.sin(ac), jnp.sin(ac)], axis=-1)
    cos = jnp.concatenate([jnp.tile(cos, (1, 2)), jnp.ones((TM, 128), F32)], axis=0)
    sin = jnp.concatenate([jnp.tile(sin, (1, 2)), jnp.zeros((TM, 128), F32)], axis=0)
    return cos, sin


def _attn_kernel(nqb, sink_ref, q_ref, kp_ref, kc_ref, kn_ref, kx_ref,
                 vp_ref, vc_ref, vn_ref, vx_ref, o_ref):
    n = pl.program_id(1)
    qb = ATTN_BLOCK
    nloc = 3 * qb
    nkeys = nloc + kx_ref.shape[0]
    rows = GROUP * qb
    r = lax.broadcasted_iota(I32, (rows, nkeys), 0) % qb
    c = lax.broadcasted_iota(I32, (rows, nkeys), 1)
    d = c - r
    ok = (c < nloc) & (d >= 0) & (d <= 2 * WINDOW) & (n < nqb)
    ok = ok & ((n > 0) | (c >= qb)) & ((n < nqb - 1) | (c < 2 * qb))
    valid = ok | (c >= nloc)
    lane = lax.broadcasted_iota(I32, (qb, 128), 1)
    rsub = lax.broadcasted_iota(I32, (rows, 1), 0) // qb
    for j in range(N_KV_HEADS):
        ks = slice(128 * j, 128 * (j + 1))
        kall = jnp.concatenate([kp_ref[:, ks], kc_ref[:, ks], kn_ref[:, ks], kx_ref[:, ks]], axis=0)
        vall = jnp.concatenate([vp_ref[:, ks], vc_ref[:, ks], vn_ref[:, ks], vx_ref[:, ks]], axis=0)
        qs = []
        sink = jnp.zeros((rows, 1), F32)
        for g in range(GROUP):
            pair, half = divmod(g, 2)
            qp = q_ref[:, 256 * j + 128 * pair:256 * j + 128 * (pair + 1)]
            keep = (lane >= 64) if half else (lane < 64)
            qs.append(jnp.where(keep, qp, jnp.zeros_like(qp)))
            sink = jnp.where(rsub == g, sink_ref[GROUP * j + g] * LOG2E, sink)
        qst = jnp.concatenate(qs, axis=0)
        s = _dot_nt(qst, kall)
        s = jnp.where(valid, s, -jnp.inf)
        m = jnp.maximum(jnp.max(s, axis=-1, keepdims=True), sink)
        p = jnp.exp2(s - m)
        den = jnp.sum(p, axis=-1, keepdims=True) + jnp.exp2(sink - m)
        o = _dot(p.astype(BF16), vall) * (1.0 / den)
        for pair in range(GROUP // 2):
            o0 = o[(2 * pair) * qb:(2 * pair + 1) * qb]
            o1 = o[(2 * pair + 1) * qb:(2 * pair + 2) * qb]
            o_ref[:, 256 * j + 128 * pair:256 * j + 128 * (pair + 1)] = (
                jnp.where(lane < 64, o0, o1).astype(BF16))


def _attention(q, k, v, sink, nb, seq, ctx_len):
    t = q.shape[0]
    qb = ATTN_BLOCK
    nqb = seq // qb
    ncb = ctx_len // qb
    xq = nb * nqb
    kw = k.shape[1]
    qrow = lambda b, n: jnp.where(n < nqb, b * nqb + n, xq + b * ncb + (n - nqb))
    kprev = lambda b, n, s: (b * nqb + jnp.clip(n - 1, 0, nqb - 1), 0)
    kcur = lambda b, n, s: (b * nqb + jnp.minimum(n, nqb - 1), 0)
    knext = lambda b, n, s: (b * nqb + jnp.minimum(n + 1, nqb - 1), 0)
    kctx = lambda b, n, s: (nb * seq // ctx_len + b, 0)
    kspec = lambda f: pl.BlockSpec((qb, kw), f)
    xspec = pl.BlockSpec((ctx_len, kw), kctx)
    return pl.pallas_call(
        functools.partial(_attn_kernel, nqb),
        grid_spec=pltpu.PrefetchScalarGridSpec(
            num_scalar_prefetch=1,
            grid=(nb, nqb + ncb),
            in_specs=[pl.BlockSpec((qb, D_MODEL), lambda b, n, s: (qrow(b, n), 0)),
                      kspec(kprev), kspec(kcur), kspec(knext), xspec,
                      kspec(kprev), kspec(kcur), kspec(knext), xspec],
            out_specs=pl.BlockSpec((qb, D_MODEL), lambda b, n, s: (qrow(b, n), 0)),
        ),
        out_shape=jax.ShapeDtypeStruct((t, D_MODEL), BF16),
        compiler_params=_cparams(("arbitrary", "arbitrary")),
        name="window_attn",
    )(sink, q, k, k, k, k, v, v, v, v)


def _post_kernel(n_xt, m_ref, w_ref, x_ref, c_ref, mod_ref, g_ref, rw_ref, rb_ref,
                 x1_ref, h2_ref, pos_ref, gate_ref, cnt_ref):
    latent = pl.program_id(0) * RT < n_xt
    for s in range(RT):
        _post_tile(s, latent, m_ref, w_ref, x_ref, c_ref, mod_ref, g_ref, rw_ref, rb_ref,
                   x1_ref, h2_ref, pos_ref, gate_ref, cnt_ref)


def _post_tile(s, latent, m_ref, w_ref, x_ref, c_ref, mod_ref, g_ref, rw_ref, rb_ref,
               x1_ref, h2_ref, pos_ref, gate_ref, cnt_ref):
    rows = slice(s * TM, (s + 1) * TM)
    md = mod_ref[0]
    y = _dot(m_ref[rows, :], w_ref[...])
    x = jnp.where(latent, x_ref[rows, :], c_ref[rows, :])
    x1 = x + md[2:3] * y
    x1_ref[rows, :] = x1
    h2 = _norm_mod(x1, g_ref[...], md[3:4], md[4:5])
    h2b = h2.astype(BF16)
    h2_ref[rows, :] = h2b

    ne = N_EXPERTS
    per = ne // N_GROUPS
    h2lo = (h2 - h2b.astype(F32)).astype(BF16)
    lg = _dot(h2b, rw_ref[...]) + _dot(h2lo, rw_ref[...])
    lg = lg + pltpu.roll(lg, ne, 1)
    logit = lg.T[0:ne]
    sc = jax.nn.sigmoid(logit)
    sel = sc + rb_ref[...]
    sub8 = lax.broadcasted_iota(I32, (per, TM), 0)
    gs = jnp.zeros((N_GROUPS, TM), F32)
    gi = lax.broadcasted_iota(I32, (N_GROUPS, TM), 0)
    for g in range(N_GROUPS):
        blk = sel[per * g:per * (g + 1)]
        m1 = jnp.max(blk, axis=0, keepdims=True)
        i1 = jnp.min(jnp.where(blk == m1, sub8, per), axis=0, keepdims=True)
        m2 = jnp.max(jnp.where(sub8 == i1, -jnp.inf, blk), axis=0, keepdims=True)
        gs = jnp.where(gi == g, m1 + m2, gs)
    grank = jnp.zeros((N_GROUPS, TM), F32)
    for g in range(N_GROUPS):
        v = gs[g:g + 1]
        grank = grank + jnp.where(gi > g, jnp.where(v >= gs, 1.0, 0.0), jnp.where(v > gs, 1.0, 0.0))
    ei = lax.broadcasted_iota(I32, (ne, TM), 0)
    gsel = jnp.zeros((ne, TM), F32)
    for g in range(N_GROUPS):
        gsel = jnp.where(ei // per == g, grank[g:g + 1], gsel)
    selm = jnp.where(gsel < TOPK_GROUPS, sel, -jnp.inf)
    eif = ei.astype(F32)
    s8 = jnp.zeros((ne, TM), F32)
    for _ in range(TOP_K):
        best = jnp.max(selm, axis=0, keepdims=True)
        first = jnp.min(jnp.where(selm == best, eif, float(ne)), axis=0, keepdims=True)
        hit = eif == first
        s8 = jnp.where(hit, 1.0, s8)
        selm = jnp.where(hit, -jnp.inf, selm)
    ws = s8 * sc
    gate = ws / jnp.sum(ws, axis=0, keepdims=True) * ROUTED_SCALE

    s8b = s8.astype(BF16)
    er = lax.broadcasted_iota(I32, (ne, ne), 0)
    ec = lax.broadcasted_iota(I32, (ne, ne), 1)
    lower = jnp.where(ec < er, 1.0, 0.0).astype(BF16)
    tr = lax.broadcasted_iota(I32, (TM, TM), 0)
    tc = lax.broadcasted_iota(I32, (TM, TM), 1)
    upper = jnp.where(tr < tc, 1.0, 0.0).astype(BF16)
    slot = _dot(lower, s8b)
    rank = _dot(s8b, upper)
    cnt = jnp.sum(s8, axis=1, keepdims=True)
    pc = jnp.floor((cnt + (CHUNK - 1)) * (1.0 / CHUNK))
    pcb = jnp.broadcast_to(pc, (ne, 128))
    cnt_ref[s] = pcb
    loc = _dot(lower, pcb.astype(BF16))[:, 0:1] * CHUNK
    prow = loc + rank
    k8 = lax.broadcasted_iota(I32, (TOP_K, TM), 0)
    pos_t = jnp.zeros((TOP_K, TM), F32)
    gate_t = jnp.zeros((TOP_K, TM), F32)
    for k in range(TOP_K):
        mk = jnp.where(slot == k, s8, 0.0)
        pos_t = jnp.where(k8 == k, jnp.sum(mk * prow, axis=0, keepdims=True), pos_t)
        gate_t = jnp.where(k8 == k, jnp.sum(mk * gate, axis=0, keepdims=True), gate_t)
    pos_ref[:, rows] = pos_t.astype(I32)
    gate_ref[:, rows] = gate_t


def _post(m, w, x2d, c2d, mod, g, rw_t, rb, nt, n_xt, tpb, nb):
    assert nt % RT == 0 and n_xt % RT == 0 and tpb % RT == 0
    t = nt * TM
    kd = m.shape[1]
    rm = RT * TM
    nxs = n_xt // RT
    row = lambda i: jnp.where(i < nxs, i * RT // tpb, nb)
    return pl.pallas_call(
        functools.partial(_post_kernel, n_xt),
        grid=(nt // RT,),
        in_specs=[pl.BlockSpec((rm, kd), lambda i: (i, 0)),
                  pl.BlockSpec(w.shape, lambda i: (0, 0)),
                  pl.BlockSpec((rm, D_MODEL), lambda i: (jnp.minimum(i, nxs - 1), 0)),
                  pl.BlockSpec((rm, D_MODEL), lambda i: (jnp.maximum(i - nxs, 0), 0)),
                  pl.BlockSpec((1, 6, D_MODEL), lambda i: (row(i), 0, 0)),
                  pl.BlockSpec((1, D_MODEL), lambda i: (0, 0)),
                  pl.BlockSpec((D_MODEL, 2 * N_EXPERTS), lambda i: (0, 0)),
                  pl.BlockSpec((N_EXPERTS, 1), lambda i: (0, 0))],
        out_specs=[pl.BlockSpec((rm, D_MODEL), lambda i: (i, 0)),
                   pl.BlockSpec((rm, D_MODEL), lambda i: (i, 0)),
                   pl.BlockSpec((TOP_K, rm), lambda i: (0, i)),
                   pl.BlockSpec((TOP_K, rm), lambda i: (0, i)),
                   pl.BlockSpec((RT, N_EXPERTS, 128), lambda i: (i, 0, 0))],
        out_shape=[jax.ShapeDtypeStruct((t, D_MODEL), F32),
                   jax.ShapeDtypeStruct((t, D_MODEL), BF16),
                   jax.ShapeDtypeStruct((TOP_K, t), I32),
                   jax.ShapeDtypeStruct((TOP_K, t), F32),
                   jax.ShapeDtypeStruct((nt, N_EXPERTS, 128), F32)],
        compiler_params=_cparams(("arbitrary",)),
        name="mix_out_route",
    )(m, w, x2d, c2d, mod, g, rw_t, rb)


def _moe_tables(pc, nblk_cap):
    nt = pc.shape[0]
    seg = jnp.sum(pc, axis=0)
    segb = (seg + CPB - 1) // CPB
    seg_end_b = jnp.cumsum(segb)
    seg_start = (seg_end_b - segb) * CPB
    goff = seg_start[None, :] + jnp.cumsum(pc, axis=0) - pc
    loc_end = jnp.cumsum(pc, axis=1)
    loc = loc_end - pc
    nch = loc_end[:, -1]
    nrb = (nch + CPH - 1) // CPH
    j = jnp.arange(NCH_CAP, dtype=I32)
    inrun = (j[None, :, None] >= loc[:, None, :]) & (j[None, :, None] < loc_end[:, None, :])
    shift = jnp.sum(jnp.where(inrun, (goff - loc)[:, None, :], 0), axis=-1)
    live = j[None, :] < nch[:, None]
    trash = nblk_cap * CPB + jnp.arange(nt, dtype=I32)[:, None] * SPARE + j[None, :] - MIN_RB * CPH
    dst = jnp.where(live, shift + j[None, :], trash).astype(I32)
    src = jnp.where(live, shift + j[None, :], 0).astype(I32)
    subs = (seg + CPH - 1) // CPH
    tl = jnp.arange(CPH - 1, dtype=I32)
    tail_len = subs * CPH - seg
    tails = jnp.where(tl[None, :] < tail_len[:, None], (seg_start + seg)[:, None] + tl[None, :], -1)
    nused = seg_end_b[-1]
    blk = jnp.arange(nblk_cap, dtype=I32)
    bexp = jnp.sum((blk[:, None] >= seg_end_b[None, :]).astype(I32), axis=-1)
    last = jnp.sum((nused - 1 >= seg_end_b).astype(I32))
    bexp = jnp.where(blk < nused, bexp, last).astype(I32)
    onehot = (bexp[:, None] == jnp.arange(N_EXPERTS, dtype=I32)[None, :]).astype(I32)
    first_blk = seg_end_b - segb
    bsub = jnp.sum(onehot * (subs - NSUB * (blk[:, None] - first_blk[None, :])), axis=-1)
    bsub = jnp.where(blk < nused, jnp.clip(bsub, 1, NSUB), 0).astype(I32)
    eid = jnp.arange(N_EXPERTS, dtype=I32)
    owns = segb > 0
    ordinal = jnp.cumsum(owns.astype(I32)) - 1
    later = jnp.where(owns[None, :] & (eid[None, :] > eid[:, None]), eid[None, :], N_EXPERTS)
    nxt_e = jnp.min(later, axis=1)
    nxt_e = jnp.where(nxt_e < N_EXPERTS, nxt_e, -1)
    bpar = jnp.sum(onehot * (ordinal % 2)[None, :], axis=-1).astype(I32)
    bnext = jnp.sum(onehot * nxt_e[None, :], axis=-1).astype(I32)
    return dict(nrb=nrb.astype(I32), dst=dst.reshape(-1), src=src.reshape(-1),
                tails=tails.reshape(-1).astype(I32),
                nused=nused.reshape(1).astype(I32), bexp=bexp, bsub=bsub, bpar=bpar, bnext=bnext)


def _rows_copy(src, dst, sem):
    return pltpu.make_async_copy(src, dst, sem)


def _row_onehot(riota, pos, base, value):
    p = jnp.zeros((TM, TM), BF16)
    for k in range(TOP_K):
        d = jnp.clip(pos[k:k + 1] - base, -1, TM).astype(F32).astype(BF16)
        v = 1.0 if value is None else value[k:k + 1]
        p = jnp.where(riota == d, v, p)
    return p


def _dispatch_kernel(nt, nrb_ref, dst_ref, tails_ref, h2_ref, pos_ref, xs_hbm, buf, sem):
    i = pl.program_id(0)
    slot = i % 2

    def drain(tile, sl):
        def body(rb, carry):
            _rows_copy(buf.at[sl, pl.ds(0, TM)], xs_hbm.at[pl.ds(0, TM)], sem.at[sl]).wait()
            return carry
        lax.fori_loop(0, jnp.maximum(nrb_ref[tile], EAGER_RB), body, 0)

    @pl.when((i >= 2) & (i < nt))
    def _():
        drain(i - 2, slot)

    @pl.when(i < nt)
    def _():
        h2 = h2_ref[...]
        pos = pos_ref[...]
        riota = lax.broadcasted_iota(I32, (TM, TM), 0).astype(F32).astype(BF16)
        nrb = nrb_ref[i]

        def block(b):
            p = _row_onehot(riota, pos, b * TM, None)
            buf[slot, b * TM:(b + 1) * TM, :] = _dot(p, h2).astype(BF16)
            for c in range(CPH):
                d = dst_ref[i * NCH_CAP + b * CPH + c]
                _rows_copy(buf.at[slot, pl.ds(b * TM + c * CHUNK, CHUNK)],
                           xs_hbm.at[pl.ds(pl.multiple_of(d * CHUNK, CHUNK), CHUNK)],
                           sem.at[slot]).start()

        for b in range(EAGER_RB):
            block(b)
        for b in range(EAGER_RB, NRB_CAP):
            pl.when(b < nrb)(functools.partial(block, b))

    @pl.when(i == nt)
    def _():
        drain(nt - 2, nt % 2)
        drain(nt - 1, (nt - 1) % 2)
        buf[0, pl.ds(0, CHUNK), :] = jnp.zeros((CHUNK, D_MODEL), BF16)
        ntail = N_EXPERTS * (CPH - 1)

        def tail(j, carry):
            d = tails_ref[j]

            @pl.when(d >= 0)
            def _():
                _rows_copy(buf.at[0, pl.ds(0, CHUNK)],
                           xs_hbm.at[pl.ds(pl.multiple_of(d * CHUNK, CHUNK), CHUNK)], sem.at[0]).start()
            return carry

        lax.fori_loop(0, ntail, tail, 0)

        def tail_wait(j, carry):
            @pl.when(tails_ref[j] >= 0)
            def _():
                _rows_copy(buf.at[0, pl.ds(0, CHUNK)], xs_hbm.at[pl.ds(0, CHUNK)], sem.at[0]).wait()
            return carry

        lax.fori_loop(0, ntail, tail_wait, 0)


def _dispatch(tb, h2, pos_t, nt, nblk_cap):
    last = nt - 1
    return pl.pallas_call(
        functools.partial(_dispatch_kernel, nt),
        grid_spec=pltpu.PrefetchScalarGridSpec(
            num_scalar_prefetch=3,
            grid=(nt + 1,),
            in_specs=[pl.BlockSpec((TM, D_MODEL), lambda i, *_: (jnp.minimum(i, last), 0)),
                      pl.BlockSpec((TOP_K, TM), lambda i, *_: (0, jnp.minimum(i, last)))],
            out_specs=pl.BlockSpec(memory_space=pl.ANY),
            scratch_shapes=[pltpu.VMEM((2, RCAP, D_MODEL), BF16), pltpu.SemaphoreType.DMA((2,))],
        ),
        out_shape=jax.ShapeDtypeStruct((nblk_cap * BM + nt * SPARE * CHUNK, D_MODEL), BF16),
        compiler_params=_cparams(("arbitrary",)),
        name="moe_dispatch",
    )(tb["nrb"], tb["dst"], tb["tails"], h2, pos_t)


RING = 3


def _expert_kernel(layer, bexp_ref, bsub_ref, bpar_ref, bnext_ref, nused_ref,
                   xs_hbm, wg_hbm, wu_hbm, wd_hbm, ys_hbm,
                   xb, yb, stg_g, stg_u, stg_d, wgu, wdn, xsem, ysem, wsem):
    i = pl.program_id(0)
    nused = nused_ref[0]

    def x_copy(blk, sl, n):
        return _rows_copy(xs_hbm.at[pl.ds(pl.multiple_of(blk * BM, BM), n * HB)],
                          xb.at[sl, pl.ds(0, n * HB)], xsem.at[sl])

    def y_copy(blk, sl, n):
        return _rows_copy(yb.at[sl, pl.ds(0, n * HB)],
                          ys_hbm.at[pl.ds(pl.multiple_of(blk * BM, BM), n * HB)], ysem.at[sl])

    def sized(blk, fn):
        live = bsub_ref[blk]
        for n in range(1, NSUB + 1):
            pl.when(live == n)(functools.partial(fn, n))

    def w_copies(ex, sl):
        return (_rows_copy(wg_hbm.at[layer, ex], stg_g.at[sl], wsem.at[sl]),
                _rows_copy(wu_hbm.at[layer, ex], stg_u.at[sl], wsem.at[sl]),
                _rows_copy(wd_hbm.at[layer, ex], stg_d.at[sl], wsem.at[sl]))

    @pl.when(i == 0)
    def _():
        sized(0, lambda n: x_copy(0, 0, n).start())

        @pl.when(nused > 1)
        def _():
            sized(1, lambda n: x_copy(1, 1, n).start())
        for c in w_copies(bexp_ref[0], bpar_ref[0]):
            c.start()

    @pl.when(i < nused)
    def _():
        sl = i % RING

        @pl.when(i + 2 < nused)
        def _():
            sized(i + 2, lambda n: x_copy(i + 2, (i + 2) % RING, n).start())

        e = bexp_ref[i]
        par = bpar_ref[i]

        @pl.when((i == 0) | (e != bexp_ref[jnp.maximum(i - 1, 0)]))
        def _():
            for c in w_copies(e, par):
                c.wait()
            wgu[:, :D_EXPERT] = stg_g[par].astype(BF16)
            wgu[:, D_EXPERT:] = stg_u[par].astype(BF16)
            wdn[...] = stg_d[par].astype(BF16)
            nxt = bnext_ref[i]

            @pl.when(nxt >= 0)
            def _():
                for c in w_copies(nxt, 1 - par):
                    c.start()

        @pl.when(i >= RING)
        def _():
            sized(i - RING, lambda n: y_copy(i - RING, sl, n).wait())

        def step(n):
            x_copy(i, sl, n).wait()
            for c in range(n):
                r0 = c * HB
                gu = _dot(xb[sl, pl.ds(r0, HB), :], wgu[...])
                h = _silu(gu[:, :D_EXPERT]) * gu[:, D_EXPERT:]
                yb[sl, pl.ds(r0, HB), :] = _dot(h.astype(BF16), wdn[...]).astype(BF16)
            y_copy(i, sl, n).start()

        sized(i, step)

        @pl.when(i == nused - 1)
        def _():
            for back in range(RING):
                @pl.when(i - back >= 0)
                def _():
                    sized(i - back, lambda n, back=back: y_copy(i - back, (i - back) % RING, n).wait())


def _experts(tb, xs, w_gate, w_up, w_down, layer, nblk_cap):
    hbm = pl.BlockSpec(memory_space=pl.ANY)
    return pl.pallas_call(
        functools.partial(_expert_kernel, layer),
        grid_spec=pltpu.PrefetchScalarGridSpec(
            num_scalar_prefetch=5,
            grid=(nblk_cap,),
            in_specs=[hbm, hbm, hbm, hbm],
            out_specs=hbm,
            scratch_shapes=[pltpu.VMEM((RING, BM, D_MODEL), BF16),
                            pltpu.VMEM((RING, BM, D_MODEL), BF16),
                            pltpu.VMEM((2, D_MODEL, D_EXPERT), F32),
                            pltpu.VMEM((2, D_MODEL, D_EXPERT), F32),
                            pltpu.VMEM((2, D_EXPERT, D_MODEL), F32),
                            pltpu.VMEM((D_MODEL, 2 * D_EXPERT), BF16),
                            pltpu.VMEM((D_EXPERT, D_MODEL), BF16),
                            pltpu.SemaphoreType.DMA((RING,)),
                            pltpu.SemaphoreType.DMA((RING,)),
                            pltpu.SemaphoreType.DMA((2,))],
        ),
        out_shape=jax.ShapeDtypeStruct((nblk_cap * BM, D_MODEL), BF16),
        compiler_params=_cparams(("arbitrary",)),
        name="moe_experts",
    )(tb["bexp"], tb["bsub"], tb["bpar"], tb["bnext"], tb["nused"], xs, w_gate, w_up, w_down)


def _combine_kernel(final, nt, nrb_ref, src_ref, ys_hbm, pos_ref, gate_ref, h2_ref, x1_ref,
                    mod_ref, sgu_ref, sdn_ref, fg_ref, o_ref, buf, acc, sem):
    i = pl.program_id(0)
    slot = i % 2
    nxt = jnp.minimum(i + 1, nt - 1)
    nrb = nrb_ref[i]
    nrb_next = nrb_ref[nxt]

    def fetch(tile, sl, b):
        for c in range(CPH):
            s = src_ref[tile * NCH_CAP + b * CPH + c]
            _rows_copy(ys_hbm.at[pl.ds(pl.multiple_of(s * CHUNK, CHUNK), CHUNK)],
                       buf.at[sl, pl.ds(b * TM + c * CHUNK, CHUNK)], sem.at[sl]).start()

    def wait_blocks(sl, n):
        def body(rb, carry):
            _rows_copy(ys_hbm.at[pl.ds(0, TM)], buf.at[sl, pl.ds(0, TM)], sem.at[sl]).wait()
            return carry
        lax.fori_loop(0, jnp.maximum(n, EAGER_RB), body, 0)

    @pl.when(i == 0)
    def _():
        for b in range(EAGER_RB):
            fetch(0, 0, b)
        for b in range(EAGER_RB, NRB_CAP):
            pl.when(b < nrb)(functools.partial(fetch, 0, 0, b))

    wait_blocks(slot, nrb)

    pos = pos_ref[...]
    gate = gate_ref[...].astype(BF16)
    riota = lax.broadcasted_iota(I32, (TM, TM), 0).astype(F32).astype(BF16)
    tlhs = (((0,), (0,)), ((), ()))

    gu = _dot(h2_ref[...], sgu_ref[...])
    hs = _silu(gu[:, :D_EXPERT]) * gu[:, D_EXPERT:]
    tot = _dot(hs.astype(BF16), sdn_ref[...])
    for b in range(EAGER_RB):
        fetch(nxt, 1 - slot, b)
        pb = _row_onehot(riota, pos, b * TM, gate)
        tot = tot + lax.dot_general(pb, buf[slot, b * TM:(b + 1) * TM, :], tlhs,
                                    preferred_element_type=F32)
    acc[...] = tot
    for b in range(EAGER_RB, NRB_CAP):
        pl.when(b < nrb_next)(functools.partial(fetch, nxt, 1 - slot, b))

        @pl.when(b < nrb)
        def _():
            pb = _row_onehot(riota, pos, b * TM, gate)
            acc[...] += lax.dot_general(pb, buf[slot, b * TM:(b + 1) * TM, :], tlhs,
                                        preferred_element_type=F32)

    @pl.when(i == nt - 1)
    def _():
        wait_blocks(1 - slot, nrb_next)

    x2 = x1_ref[...] + mod_ref[0][5:6] * acc[...]
    if final:
        x2 = x2 * lax.rsqrt(jnp.mean(x2 * x2, axis=-1, keepdims=True) + RMS_EPS) * fg_ref[...]
    o_ref[...] = x2


def _combine(tb, ys, pos, gate, h2, x1, mod, sgu, sdn, fg, nt, n_xt, tpb, nb, final):
    t = nt * TM
    row = lambda i, *_: (jnp.where(i < n_xt, i // tpb, nb), 0, 0)
    tile = lambda i, *_: (i, 0)
    slots = lambda i, *_: (0, i)
    const = lambda i, *_: (0, 0)
    return pl.pallas_call(
        functools.partial(_combine_kernel, final, nt),
        grid_spec=pltpu.PrefetchScalarGridSpec(
            num_scalar_prefetch=2,
            grid=(nt,),
            in_specs=[pl.BlockSpec(memory_space=pl.ANY),
                      pl.BlockSpec((TOP_K, TM), slots),
                      pl.BlockSpec((TOP_K, TM), slots),
                      pl.BlockSpec((TM, D_MODEL), tile),
                      pl.BlockSpec((TM, D_MODEL), tile),
                      pl.BlockSpec((1, 6, D_MODEL), row),
                      pl.BlockSpec(sgu.shape, const),
                      pl.BlockSpec(sdn.shape, const),
                      pl.BlockSpec((1, D_MODEL), const)],
            out_specs=pl.BlockSpec((TM, D_MODEL), tile),
            scratch_shapes=[pltpu.VMEM((2, RCAP, D_MODEL), BF16),
                            pltpu.VMEM((TM, D_MODEL), F32),
                            pltpu.SemaphoreType.DMA((2,))],
        ),
        out_shape=jax.ShapeDtypeStruct((t, D_MODEL), F32),
        compiler_params=_cparams(("arbitrary",)),
        name="moe_combine",
    )(tb["nrb"], tb["src"], ys, pos, gate, h2, x1, mod, sgu, sdn, fg)


def _moe(h2, pos_t, gate_t, cnt, x1, mod, w_gate, w_up, w_down, layer, sgu, sdn, fg,
         nt, n_xt, tpb, nb, final):
    nblk_cap = (TOP_K * nt * TM + (CHUNK - 1) * N_EXPERTS * nt) // BM + N_EXPERTS
    tb = _moe_tables(cnt[:, :, 0].astype(I32), nblk_cap)
    xs = _dispatch(tb, h2, pos_t, nt, nblk_cap)
    ys = _experts(tb, xs, w_gate, w_up, w_down, layer, nblk_cap)
    return _combine(tb, ys, pos_t, gate_t, h2, x1, mod, sgu, sdn, fg, nt, n_xt, tpb, nb, final)


def _lru_in_kernel(x_ref, mod_ref, g_ref, w_ref, gate_ref, u_ref):
    m = mod_ref[0]
    h = _norm_mod(x_ref[...], g_ref[...], m[0:1], m[1:2])
    p = _dot(h.astype(BF16), w_ref[...])
    z = p[:, :D_RNN]
    cdf = 0.5 * (1.0 + jnp.tanh(math.sqrt(2.0 / math.pi) * (z + 0.044715 * (z * z * z))))
    gate_ref[...] = (z * cdf).astype(BF16)
    u_ref[...] = p[:, D_RNN:]


def _lru_in(xall, mod, g, w, n_xt, tpb, nb):
    t = xall.shape[0]
    nt = t // TM
    row = lambda i: jnp.where(i < n_xt, i // tpb, nb)
    return pl.pallas_call(
        _lru_in_kernel,
        grid=(nt,),
        in_specs=[pl.BlockSpec((TM, D_MODEL), lambda i: (i, 0)),
                  pl.BlockSpec((1, 6, D_MODEL), lambda i: (row(i), 0, 0)),
                  pl.BlockSpec((1, D_MODEL), lambda i: (0, 0)),
                  pl.BlockSpec(w.shape, lambda i: (0, 0))],
        out_specs=[pl.BlockSpec((TM, D_RNN), lambda i: (i, 0)),
                   pl.BlockSpec((TM, D_RNN), lambda i: (i, 0))],
        out_shape=[jax.ShapeDtypeStruct((t, D_RNN), BF16),
                   jax.ShapeDtypeStruct((t, D_RNN), F32)],
        compiler_params=_cparams(("arbitrary",)),
        name="lru_in",
    )(xall, mod, g, w)


def _sigmoid(x):
    return 0.5 * (1.0 + jnp.tanh(0.5 * x))


def _gate_windows():
    wins = []
    for c0 in range(0, D_RNN, 256):
        c1 = min(c0 + 256, D_RNN)
        k0 = (c0 // LRU_BLOCK_W) * LRU_BLOCK_W
        k1 = ((c1 - 1) // LRU_BLOCK_W + 1) * LRU_BLOCK_W
        wins.append((c0, c1, (k0 // 128) * 128, min(-(-k1 // 128) * 128, D_RNN)))
    return wins


def _block_diag_dot(ub, w_ref):
    return jnp.concatenate([_dot(ub[:, k0:k1], w_ref[k0:k1, c0:c1]) for c0, c1, k0, k1 in _gate_windows()],
                           axis=1)


def _lru_sweep_kernel(rev, mix, ns, u_ref, up_ref, un_ref, cw_ref, cb_ref, wa_ref, wx_ref,
                      ba_ref, bx_ref, lam_ref, h0_ref, *rest):
    if mix:
        hf_ref, gate_ref, o_ref, a_s, b_s, carry = rest
    else:
        o_ref, a_s, b_s, carry = rest
    s = pl.program_id(1)
    ss = (ns - 1 - s) if rev else s

    @pl.when(s == 0)
    def _():
        carry[...] = jnp.broadcast_to(h0_ref[0], (8, D_RNN))

    prev = jnp.where(ss > 0, up_ref[...], 0.0)
    nxt = jnp.where(ss < ns - 1, un_ref[...], 0.0)
    ext = jnp.concatenate([prev, u_ref[...], nxt], axis=0)
    n_ext = TM + 16
    cw = cw_ref[...]
    u = cb_ref[...]
    ng_ext = n_ext // 8
    ext3 = ext.reshape(ng_ext, 8, D_RNN)
    sub3 = lax.broadcasted_iota(I32, (ng_ext, 8, D_RNN), 1)
    for j in range(CONV_W):
        d = j - CONV_LEFT
        if d == 0:
            tap = ext3
        else:
            rot = pltpu.roll(ext3, (-d) % 8, 1)
            if d < 0:
                nb_rows = jnp.concatenate([rot[-1:], rot[:-1]], axis=0)
                tap = jnp.where(sub3 < -d, nb_rows, rot)
            else:
                nb_rows = jnp.concatenate([rot[1:], rot[:1]], axis=0)
                tap = jnp.where(sub3 >= 8 - d, nb_rows, rot)
        u = u + cw[j:j + 1] * tap[1:1 + TM // 8].reshape(TM, D_RNN)

    ub = u.astype(BF16)
    r = _sigmoid(_block_diag_dot(ub, wa_ref) + ba_ref[...])
    ig = _sigmoid(_block_diag_dot(ub, wx_ref) + bx_ref[...])
    nl = -lam_ref[...]
    softplus = jnp.maximum(nl, 0.0) + jnp.log1p(jnp.exp(-jnp.abs(nl)))
    y = LRU_C * r * softplus
    a = jnp.exp(-y)
    a_s[...] = a
    b_s[...] = jnp.sqrt(jnp.tanh(y) * (1.0 + a * a)) * (ig * u)

    sub = lax.broadcasted_iota(I32, (8, D_RNN), 0)
    h = carry[...]
    ng = TM // 8
    for gi in range(ng):
        r0 = 8 * ((ng - 1 - gi) if rev else gi)
        a = a_s[r0:r0 + 8, :]
        b = b_s[r0:r0 + 8, :]
        for sh in (1, 2, 4):
            amt = (8 - sh) if rev else sh
            keep = (sub < 8 - sh) if rev else (sub >= sh)
            a_sh = pltpu.roll(a, amt, 0)
            b_sh = pltpu.roll(b, amt, 0)
            b = jnp.where(keep, a * b_sh + b, b)
            a = jnp.where(keep, a * a_sh, a)
        hg = a * h + b
        edge = hg[0:1] if rev else hg[7:8]
        h = jnp.broadcast_to(edge, (8, D_RNN))
        if mix:
            o_ref[r0:r0 + 8, :] = ((hf_ref[r0:r0 + 8, :] + hg)
                                   * gate_ref[r0:r0 + 8, :].astype(F32)).astype(BF16)
        else:
            o_ref[r0:r0 + 8, :] = hg
    carry[...] = h


def _lru_sweep(u_all, row0, nb, seq, cw, cb, wa, wx, ba, bx, lam, h0, rev, hf=None, gate=None):
    ns = seq // TM
    blk0 = row0 // TM
    mix = hf is not None
    sidx = (lambda s: ns - 1 - s) if rev else (lambda s: s)
    cur = lambda b, s: (blk0 + b * ns + sidx(s), 0)
    out = lambda b, s: (b * ns + sidx(s), 0)
    h8 = TM // 8
    nblk8 = u_all.shape[0] // 8
    prv = lambda b, s: (jnp.maximum((blk0 + b * ns + sidx(s)) * h8 - 1, 0), 0)
    nxt = lambda b, s: (jnp.minimum((blk0 + b * ns + sidx(s) + 1) * h8, nblk8 - 1), 0)
    vec = lambda b, s: (0, 0)
    in_specs = [pl.BlockSpec((TM, D_RNN), cur),
                pl.BlockSpec((8, D_RNN), prv),
                pl.BlockSpec((8, D_RNN), nxt),
                pl.BlockSpec((CONV_W, D_RNN), vec),
                pl.BlockSpec((1, D_RNN), vec),
                pl.BlockSpec((D_RNN, D_RNN), vec),
                pl.BlockSpec((D_RNN, D_RNN), vec),
                pl.BlockSpec((1, D_RNN), vec),
                pl.BlockSpec((1, D_RNN), vec),
                pl.BlockSpec((1, D_RNN), vec),
                pl.BlockSpec((1, 1, D_RNN), lambda b, s: (b, 0, 0))]
    args = [u_all, u_all, u_all, cw, cb, wa, wx, ba, bx, lam, h0]
    if mix:
        in_specs += [pl.BlockSpec((TM, D_RNN), out), pl.BlockSpec((TM, D_RNN), cur)]
        args += [hf, gate]
    return pl.pallas_call(
        functools.partial(_lru_sweep_kernel, rev, mix, ns),
        grid=(nb, ns),
        in_specs=in_specs,
        out_specs=pl.BlockSpec((TM, D_RNN), out),
        out_shape=jax.ShapeDtypeStruct((nb * seq, D_RNN), BF16 if mix else F32),
        scratch_shapes=[pltpu.VMEM((TM, D_RNN), F32), pltpu.VMEM((TM, D_RNN), F32),
                        pltpu.VMEM((8, D_RNN), F32)],
        compiler_params=_cparams(("arbitrary", "arbitrary")),
        name="lru_sweep_rev" if rev else "lru_sweep_fwd",
    )(*args)


def _block_diag(w):
    rows = jnp.tile(w.reshape(D_RNN, LRU_BLOCK_W), (1, LRU_BLOCKS))
    blk = jnp.arange(D_RNN, dtype=I32) // LRU_BLOCK_W
    return jnp.where(blk[:, None] == blk[None, :], rows, 0.0).astype(BF16)


def kernel(x, c, ctx, c_ctx, ada_w, ada_b, norm1_g, norm2_g, attn_w_qkv, attn_w_o, attn_sink,
           lru_w_in, lru_conv_w, lru_conv_b, lru_wa, lru_ba, lru_wx, lru_bx, lru_lam, lru_w_out,
           moe_router_w, moe_router_bias, moe_w_gate, moe_w_up, moe_w_down,
           shared_w_gate, shared_w_up, shared_w_down, final_g):
    nb, seq, d = x.shape
    ctx_len = ctx.shape[1]
    assert d == D_MODEL and seq % TM == 0 and ctx_len == TM and nb < 8
    tx = nb * seq
    tall = tx + nb * ctx_len
    tpb = seq // TM
    n_xt = tx // TM
    nt_all = tall // TM

    cvec = jnp.zeros((8, d), F32).at[:nb].set(c).at[nb].set(c_ctx)
    mod = _adaln(cvec, ada_w, ada_b).reshape(DEPTH, 8, 6, d)
    x2d = x.reshape(tx, d)
    c2d = ctx.reshape(nb * ctx_len, d)
    row = lambda v: v.reshape(1, -1)
    shared = lambda i: (jnp.concatenate([shared_w_gate[i], shared_w_up[i]], axis=1).astype(BF16),
                        shared_w_down[i].astype(BF16))

    def router(i):
        hi = moe_router_w[i].astype(BF16)
        lo = (moe_router_w[i] - hi.astype(F32)).astype(BF16)
        return jnp.concatenate([hi, lo], axis=1)

    qd = N_HEADS * HEAD_DIM
    kd = N_KV_HEADS * HEAD_DIM
    wq = attn_w_qkv[0][:, :qd]
    wk = attn_w_qkv[0][:, qd:qd + kd].reshape(d, N_KV_HEADS, 1, HEAD_DIM)
    wv = attn_w_qkv[0][:, qd + kd:].reshape(d, N_KV_HEADS, 1, HEAD_DIM)
    dup = lambda w: jnp.broadcast_to(w, (d, N_KV_HEADS, 2, HEAD_DIM)).reshape(d, 2 * kd)
    wqkv = jnp.concatenate([wq, dup(wk), dup(wv)], axis=1).astype(BF16)
    cos, sin = _rope_tables(seq)
    q, k, v = _qkv(x2d, c2d, mod[0], row(norm1_g[0]), wqkv, cos, sin, n_xt, tpb, nb)
    o = _attention(q, k, v, attn_sink[0], nb, seq, ctx_len)
    x1, h2, pos_t, gate_t, cnt = _post(
        o, attn_w_o[0].astype(BF16), x2d, c2d, mod[0], row(norm2_g[0]),
        router(0), moe_router_bias[0].reshape(-1, 1), nt_all, n_xt, tpb, nb)
    sgu, sdn = shared(0)
    xall = _moe(h2, pos_t, gate_t, cnt, x1, mod[0], moe_w_gate, moe_w_up, moe_w_down, 0,
                sgu, sdn, row(final_g), nt_all, n_xt, tpb, nb, False)

    gate, u_pre = _lru_in(xall, mod[1], row(norm1_g[1]), lru_w_in[0].astype(BF16), n_xt, tpb, nb)
    cw, cb = lru_conv_w[0], row(lru_conv_b[0])
    hdir = []
    for dr, rev in ((0, False), (1, True)):
        wa = _block_diag(lru_wa[0, dr])
        wx = _block_diag(lru_wx[0, dr])
        prm = (cw, cb, wa, wx, row(lru_ba[0, dr]), row(lru_bx[0, dr]), row(lru_lam[0, dr]))
        zero = jnp.zeros((nb, 1, D_RNN), F32)
        hc = _lru_sweep(u_pre, tx, nb, ctx_len, *prm, zero, rev).reshape(nb, ctx_len, D_RNN)
        h0 = hc[:, 0:1] if rev else hc[:, ctx_len - 1:ctx_len]
        if not rev:
            hdir.append(_lru_sweep(u_pre, 0, nb, seq, *prm, h0, rev))
        else:
            mixed = _lru_sweep(u_pre, 0, nb, seq, *prm, h0, rev, hf=hdir[0], gate=gate)
    x1, h2, pos_t, gate_t, cnt = _post(
        mixed, lru_w_out[0].astype(BF16), xall, xall, mod[1], row(norm2_g[1]),
        router(1), moe_router_bias[1].reshape(-1, 1), n_xt, n_xt, tpb, nb)
    sgu, sdn = shared(1)
    out = _moe(h2, pos_t, gate_t, cnt, x1, mod[1], moe_w_gate, moe_w_up, moe_w_down, 1,
               sgu, sdn, row(final_g), n_xt, n_xt, tpb, nb, True)
    return out.reshape(nb, seq, d)
```

```python
import functools
import math

import jax
import jax.numpy as jnp
from jax import lax
from jax.experimental import pallas as pl
from jax.experimental.pallas import tpu as pltpu

D_MODEL = 1024
DEPTH = 2
GRID_W = 64
HEAD_DIM = 64
N_HEADS = 16
N_KV_HEADS = 4
GROUP = N_HEADS // N_KV_HEADS
WINDOW = 128
ATTN_BLOCK = 128
ROPE_THETA = 10000.0
D_RNN = 1280
LRU_BLOCKS = 16
LRU_BLOCK_W = D_RNN // LRU_BLOCKS
CONV_W = 4
CONV_LEFT = 2
LRU_C = 8.0
N_EXPERTS = 64
TOP_K = 8
N_GROUPS = 8
TOPK_GROUPS = 4
D_EXPERT = 256
ROUTED_SCALE = 2.5
RMS_EPS = 1e-6

F32 = jnp.float32
BF16 = jnp.bfloat16
I32 = jnp.int32
HIGHEST = lax.Precision.HIGHEST
LOG2E = math.log2(math.e)

TM = 256
RT = 2
CHUNK = 16
HB = 256
NSUB = 4
BM = NSUB * HB
CPB = BM // CHUNK
CPH = HB // CHUNK
RCAP = ((TOP_K * TM + (CHUNK - 1) * N_EXPERTS + TM - 1) // TM) * TM
NCH_CAP = RCAP // CHUNK
NRB_CAP = RCAP // TM
MIN_RB = TOP_K
EAGER_RB = 10
SPARE = (NRB_CAP - MIN_RB) * CPH
VMEM_LIMIT = 56 * 1024 * 1024


def _cparams(sem):
    return pltpu.CompilerParams(dimension_semantics=sem, vmem_limit_bytes=VMEM_LIMIT)


def _silu(x):
    return x * jax.nn.sigmoid(x)


def _dot(a, b):
    return jnp.dot(a, b, preferred_element_type=F32)


def _dot_nt(a, b, precision=None):
    return lax.dot_general(a, b, (((1,), (1,)), ((), ())), precision=precision,
                           preferred_element_type=F32)


def _adaln_kernel(c_ref, w_ref, b_ref, o_ref):
    s = _silu(c_ref[...])
    o_ref[0] = jnp.dot(s, w_ref[0], precision=HIGHEST, preferred_element_type=F32) + b_ref[0]


def _adaln(cvec, ada_w, ada_b):
    nb = 1536
    d6 = 6 * D_MODEL
    return pl.pallas_call(
        _adaln_kernel,
        grid=(DEPTH, d6 // nb),
        in_specs=[pl.BlockSpec((8, D_MODEL), lambda l, j: (0, 0)),
                  pl.BlockSpec((1, D_MODEL, nb), lambda l, j: (l, 0, j)),
                  pl.BlockSpec((1, 1, nb), lambda l, j: (l, 0, j))],
        out_specs=pl.BlockSpec((1, 8, nb), lambda l, j: (l, 0, j)),
        out_shape=jax.ShapeDtypeStruct((DEPTH, 8, d6), F32),
        compiler_params=_cparams(("arbitrary", "arbitrary")),
        name="adaln",
    )(cvec, ada_w, ada_b.reshape(DEPTH, 1, d6))


def _norm_mod(x, g, shift, scale):
    xn = x * lax.rsqrt(jnp.mean(x * x, axis=-1, keepdims=True) + RMS_EPS) * g
    return xn * (1.0 + scale) + shift


def _qkv_kernel(n_xt, x_ref, c_ref, mod_ref, g_ref, w_ref, cos_ref, sin_ref, q_ref, k_ref, v_ref):
    m = mod_ref[0]
    x = jnp.where(pl.program_id(0) < n_xt, x_ref[...], c_ref[...])
    h = _norm_mod(x, g_ref[...], m[0:1], m[1:2])
    p = _dot(h.astype(BF16), w_ref[...])
    cos = cos_ref[...]
    sin = sin_ref[...]
    lane = lax.broadcasted_iota(I32, (TM, 128), 1)
    second = (lane & 16) != 0
    scale = HEAD_DIM ** -0.5 * LOG2E
    nq = N_HEADS * HEAD_DIM // 128
    nk = 2 * N_KV_HEADS * HEAD_DIM // 128
    for c in range(nq + nk):
        blk = p[:, 128 * c:128 * (c + 1)]
        partner = jnp.where(second, pltpu.roll(blk, 16, 1), pltpu.roll(blk, 112, 1))
        r = blk * cos + partner * sin
        if c < nq:
            q_ref[:, 128 * c:128 * (c + 1)] = (r * scale).astype(BF16)
        else:
            k_ref[:, 128 * (c - nq):128 * (c - nq + 1)] = r.astype(BF16)
    v_ref[...] = p[:, 128 * (nq + nk):].astype(BF16)


def _qkv(x2d, c2d, mod, g, w, cos, sin, n_xt, tpb, nb):
    t = x2d.shape[0] + c2d.shape[0]
    nt = t // TM
    kw = 2 * N_KV_HEADS * HEAD_DIM
    row = lambda i: jnp.where(i < n_xt, i // tpb, nb)
    pos = lambda i: jnp.where(i < n_xt, i % tpb, tpb)
    return pl.pallas_call(
        functools.partial(_qkv_kernel, n_xt),
        grid=(nt,),
        in_specs=[pl.BlockSpec((TM, D_MODEL), lambda i: (jnp.minimum(i, n_xt - 1), 0)),
                  pl.BlockSpec((TM, D_MODEL), lambda i: (jnp.maximum(i - n_xt, 0), 0)),
                  pl.BlockSpec((1, 6, D_MODEL), lambda i: (row(i), 0, 0)),
                  pl.BlockSpec((1, D_MODEL), lambda i: (0, 0)),
                  pl.BlockSpec(w.shape, lambda i: (0, 0)),
                  pl.BlockSpec((TM, 128), lambda i: (pos(i), 0)),
                  pl.BlockSpec((TM, 128), lambda i: (pos(i), 0))],
        out_specs=[pl.BlockSpec((TM, D_MODEL), lambda i: (i, 0)),
                   pl.BlockSpec((TM, kw), lambda i: (i, 0)),
                   pl.BlockSpec((TM, kw), lambda i: (i, 0))],
        out_shape=[jax.ShapeDtypeStruct((t, D_MODEL), BF16),
                   jax.ShapeDtypeStruct((t, kw), BF16),
                   jax.ShapeDtypeStruct((t, kw), BF16)],
        compiler_params=_cparams(("arbitrary",)),
        name="qkv_rope",
    )(x2d, c2d, mod, g, w, cos, sin)


def _rope_tables(seq):
    s = jnp.arange(seq)
    row = (s // GRID_W).astype(F32)
    col = (s % GRID_W).astype(F32)
    n_freq = HEAD_DIM // 4
    inv = jnp.exp(-math.log(ROPE_THETA) * jnp.arange(n_freq, dtype=F32) / n_freq)
    ar = row[:, None] * inv
    ac = col[:, None] * inv
    cos = jnp.concatenate([jnp.cos(ar), jnp.cos(ar), jnp.cos(ac), jnp.cos(ac)], axis=-1)
    sin = jnp.concatenate([-jnp.sin(ar), jnp.sin(ar), -jnp.sin(ac), jnp.sin(ac)], axis=-1)
    cos = jnp.concatenate([jnp.tile(cos, (1, 2)), jnp.ones((TM, 128), F32)], axis=0)
    sin = jnp.concatenate([jnp.tile(sin, (1, 2)), jnp.zeros((TM, 128), F32)], axis=0)
    return cos, sin


def _attn_kernel(nqb, sink_ref, q_ref, kp_ref, kc_ref, kn_ref, kx_ref,
                 vp_ref, vc_ref, vn_ref, vx_ref, o_ref):
    n = pl.program_id(1)
    qb = ATTN_BLOCK
    nloc = 3 * qb
    nkeys = nloc + kx_ref.shape[0]
    rows = GROUP * qb
    r = lax.broadcasted_iota(I32, (rows, nloc), 0) % qb
    c = lax.broadcasted_iota(I32, (rows, nloc), 1)
    d = c - r
    ok = (d >= 0) & (d <= 2 * WINDOW) & (n < nqb)
    ok = ok & ((n > 0) | (c >= qb)) & ((n < nqb - 1) | (c < 2 * qb))
    lane = lax.broadcasted_iota(I32, (qb, 128), 1)
    rsub = lax.broadcasted_iota(I32, (rows, 1), 0) // qb
    for j in range(N_KV_HEADS):
        ks = slice(128 * j, 128 * (j + 1))
        kall = jnp.concatenate([kp_ref[:, ks], kc_ref[:, ks], kn_ref[:, ks], kx_ref[:, ks]], axis=0)
        vall = jnp.concatenate([vp_ref[:, ks], vc_ref[:, ks], vn_ref[:, ks], vx_ref[:, ks]], axis=0)
        qs = []
        sink = jnp.zeros((rows, 1), F32)
        for g in range(GROUP):
            pair, half = divmod(g, 2)
            qp = q_ref[:, 256 * j + 128 * pair:256 * j + 128 * (pair + 1)]
            keep = (lane >= 64) if half else (lane < 64)
            qs.append(jnp.where(keep, qp, jnp.zeros_like(qp)))
            sink = jnp.where(rsub == g, sink_ref[GROUP * j + g] * LOG2E, sink)
        qst = jnp.concatenate(qs, axis=0)
        s = _dot_nt(qst, kall)
        s = jnp.concatenate([jnp.where(ok, s[:, :nloc], -jnp.inf), s[:, nloc:]], axis=1)
        m = jnp.maximum(jnp.max(s, axis=-1, keepdims=True), sink)
        p = jnp.exp2(s - m)
        den = jnp.sum(p, axis=-1, keepdims=True) + jnp.exp2(sink - m)
        o = _dot(p.astype(BF16), vall) * (1.0 / den)
        for pair in range(GROUP // 2):
            o0 = o[(2 * pair) * qb:(2 * pair + 1) * qb]
            o1 = o[(2 * pair + 1) * qb:(2 * pair + 2) * qb]
            o_ref[:, 256 * j + 128 * pair:256 * j + 128 * (pair + 1)] = (
                jnp.where(lane < 64, o0, o1).astype(BF16))


def _attention(q, k, v, sink, nb, seq, ctx_len):
    t = q.shape[0]
    qb = ATTN_BLOCK
    nqb = seq // qb
    ncb = ctx_len // qb
    xq = nb * nqb
    kw = k.shape[1]
    qrow = lambda b, n: jnp.where(n < nqb, b * nqb + n, xq + b * ncb + (n - nqb))
    kprev = lambda b, n, s: (b * nqb + jnp.clip(n - 1, 0, nqb - 1), 0)
    kcur = lambda b, n, s: (b * nqb + jnp.minimum(n, nqb - 1), 0)
    knext = lambda b, n, s: (b * nqb + jnp.minimum(n + 1, nqb - 1), 0)
    kctx = lambda b, n, s: (nb * seq // ctx_len + b, 0)
    kspec = lambda f: pl.BlockSpec((qb, kw), f)
    xspec = pl.BlockSpec((ctx_len, kw), kctx)
    return pl.pallas_call(
        functools.partial(_attn_kernel, nqb),
        grid_spec=pltpu.PrefetchScalarGridSpec(
            num_scalar_prefetch=1,
            grid=(nb, nqb + ncb),
            in_specs=[pl.BlockSpec((qb, D_MODEL), lambda b, n, s: (qrow(b, n), 0)),
                      kspec(kprev), kspec(kcur), kspec(knext), xspec,
                      kspec(kprev), kspec(kcur), kspec(knext), xspec],
            out_specs=pl.BlockSpec((qb, D_MODEL), lambda b, n, s: (qrow(b, n), 0)),
        ),
        out_shape=jax.ShapeDtypeStruct((t, D_MODEL), BF16),
        compiler_params=_cparams(("arbitrary", "arbitrary")),
        name="window_attn",
    )(sink, q, k, k, k, k, v, v, v, v)


def _post_kernel(n_xt, m_ref, w_ref, x_ref, c_ref, mod_ref, g_ref, rw_ref, rb_ref,
                 x1_ref, h2_ref, pos_ref, gate_ref, cnt_ref):
    latent = pl.program_id(0) * RT < n_xt
    for s in range(RT):
        _post_tile(s, latent, m_ref, w_ref, x_ref, c_ref, mod_ref, g_ref, rw_ref, rb_ref,
                   x1_ref, h2_ref, pos_ref, gate_ref, cnt_ref)


def _post_tile(s, latent, m_ref, w_ref, x_ref, c_ref, mod_ref, g_ref, rw_ref, rb_ref,
               x1_ref, h2_ref, pos_ref, gate_ref, cnt_ref):
    rows = slice(s * TM, (s + 1) * TM)
    md = mod_ref[0]
    y = _dot(m_ref[rows, :], w_ref[...])
    x = jnp.where(latent, x_ref[rows, :], c_ref[rows, :])
    x1 = x + md[2:3] * y
    x1_ref[rows, :] = x1
    h2 = _norm_mod(x1, g_ref[...], md[3:4], md[4:5])
    h2b = h2.astype(BF16)
    h2_ref[rows, :] = h2b

    ne = N_EXPERTS
    per = ne // N_GROUPS
    h2lo = (h2 - h2b.astype(F32)).astype(BF16)
    lg = _dot(h2b, rw_ref[...]) + _dot(h2lo, rw_ref[...])
    lg = lg + pltpu.roll(lg, ne, 1)
    logit = lg.T[0:ne]
    sc = jax.nn.sigmoid(logit)
    sel = sc + rb_ref[...]
    sub8 = lax.broadcasted_iota(I32, (per, TM), 0)
    gs = jnp.zeros((N_GROUPS, TM), F32)
    gi = lax.broadcasted_iota(I32, (N_GROUPS, TM), 0)
    for g in range(N_GROUPS):
        blk = sel[per * g:per * (g + 1)]
        m1 = jnp.max(blk, axis=0, keepdims=True)
        i1 = jnp.min(jnp.where(blk == m1, sub8, per), axis=0, keepdims=True)
        m2 = jnp.max(jnp.where(sub8 == i1, -jnp.inf, blk), axis=0, keepdims=True)
        gs = jnp.where(gi == g, m1 + m2, gs)
    grank = jnp.zeros((N_GROUPS, TM), F32)
    for g in range(N_GROUPS):
        v = gs[g:g + 1]
        grank = grank + jnp.where(gi > g, jnp.where(v >= gs, 1.0, 0.0), jnp.where(v > gs, 1.0, 0.0))
    ei = lax.broadcasted_iota(I32, (ne, TM), 0)
    gsel = jnp.zeros((ne, TM), F32)
    for g in range(N_GROUPS):
        gsel = jnp.where(ei // per == g, grank[g:g + 1], gsel)
    selm = jnp.where(gsel < TOPK_GROUPS, sel, -jnp.inf)
    eif = ei.astype(F32)
    s8 = jnp.zeros((ne, TM), F32)
    for _ in range(TOP_K):
        best = jnp.max(selm, axis=0, keepdims=True)
        first = jnp.min(jnp.where(selm == best, eif, float(ne)), axis=0, keepdims=True)
        hit = eif == first
        s8 = jnp.where(hit, 1.0, s8)
        selm = jnp.where(hit, -jnp.inf, selm)
    ws = s8 * sc
    gate = ws / jnp.sum(ws, axis=0, keepdims=True) * ROUTED_SCALE

    s8b = s8.astype(BF16)
    er = lax.broadcasted_iota(I32, (ne, ne), 0)
    ec = lax.broadcasted_iota(I32, (ne, ne), 1)
    lower = jnp.where(ec < er, 1.0, 0.0).astype(BF16)
    tr = lax.broadcasted_iota(I32, (TM, TM), 0)
    tc = lax.broadcasted_iota(I32, (TM, TM), 1)
    upper = jnp.where(tr < tc, 1.0, 0.0).astype(BF16)
    slot = _dot(lower, s8b)
    rank = _dot(s8b, upper)
    cnt = jnp.sum(s8, axis=1, keepdims=True)
    pc = jnp.floor((cnt + (CHUNK - 1)) * (1.0 / CHUNK))
    pcb = jnp.broadcast_to(pc, (ne, 128))
    cnt_ref[s] = pcb
    loc = _dot(lower, pcb.astype(BF16))[:, 0:1] * CHUNK
    prow = loc + rank
    k8 = lax.broadcasted_iota(I32, (TOP_K, TM), 0)
    pos_t = jnp.zeros((TOP_K, TM), F32)
    gate_t = jnp.zeros((TOP_K, TM), F32)
    for k in range(TOP_K):
        mk = jnp.where(slot == k, s8, 0.0)
        pos_t = jnp.where(k8 == k, jnp.sum(mk * prow, axis=0, keepdims=True), pos_t)
        gate_t = jnp.where(k8 == k, jnp.sum(mk * gate, axis=0, keepdims=True), gate_t)
    pos_ref[:, rows] = pos_t.astype(I32)
    gate_ref[:, rows] = gate_t


def _post(m, w, x2d, c2d, mod, g, rw_t, rb, nt, n_xt, tpb, nb):
    assert nt % RT == 0 and n_xt % RT == 0 and tpb % RT == 0
    t = nt * TM
    kd = m.shape[1]
    rm = RT * TM
    nxs = n_xt // RT
    row = lambda i: jnp.where(i < nxs, i * RT // tpb, nb)
    return pl.pallas_call(
        functools.partial(_post_kernel, n_xt),
        grid=(nt // RT,),
        in_specs=[pl.BlockSpec((rm, kd), lambda i: (i, 0)),
                  pl.BlockSpec(w.shape, lambda i: (0, 0)),
                  pl.BlockSpec((rm, D_MODEL), lambda i: (jnp.minimum(i, nxs - 1), 0)),
                  pl.BlockSpec((rm, D_MODEL), lambda i: (jnp.maximum(i - nxs, 0), 0)),
                  pl.BlockSpec((1, 6, D_MODEL), lambda i: (row(i), 0, 0)),
                  pl.BlockSpec((1, D_MODEL), lambda i: (0, 0)),
                  pl.BlockSpec((D_MODEL, 2 * N_EXPERTS), lambda i: (0, 0)),
                  pl.BlockSpec((N_EXPERTS, 1), lambda i: (0, 0))],
        out_specs=[pl.BlockSpec((rm, D_MODEL), lambda i: (i, 0)),
                   pl.BlockSpec((rm, D_MODEL), lambda i: (i, 0)),
                   pl.BlockSpec((TOP_K, rm), lambda i: (0, i)),
                   pl.BlockSpec((TOP_K, rm), lambda i: (0, i)),
                   pl.BlockSpec((RT, N_EXPERTS, 128), lambda i: (i, 0, 0))],
        out_shape=[jax.ShapeDtypeStruct((t, D_MODEL), F32),
                   jax.ShapeDtypeStruct((t, D_MODEL), BF16),
                   jax.ShapeDtypeStruct((TOP_K, t), I32),
                   jax.ShapeDtypeStruct((TOP_K, t), F32),
                   jax.ShapeDtypeStruct((nt, N_EXPERTS, 128), F32)],
        compiler_params=_cparams(("arbitrary",)),
        name="mix_out_route",
    )(m, w, x2d, c2d, mod, g, rw_t, rb)


def _moe_tables(pc, nblk_cap):
    nt = pc.shape[0]
    seg = jnp.sum(pc, axis=0)
    segb = (seg + CPB - 1) // CPB
    seg_end_b = jnp.cumsum(segb)
    seg_start = (seg_end_b - segb) * CPB
    goff = seg_start[None, :] + jnp.cumsum(pc, axis=0) - pc
    loc_end = jnp.cumsum(pc, axis=1)
    loc = loc_end - pc
    nch = loc_end[:, -1]
    nrb = (nch + CPH - 1) // CPH
    j = jnp.arange(NCH_CAP, dtype=I32)
    inrun = (j[None, :, None] >= loc[:, None, :]) & (j[None, :, None] < loc_end[:, None, :])
    shift = jnp.sum(jnp.where(inrun, (goff - loc)[:, None, :], 0), axis=-1)
    live = j[None, :] < nch[:, None]
    trash = nblk_cap * CPB + jnp.arange(nt, dtype=I32)[:, None] * SPARE + j[None, :] - MIN_RB * CPH
    dst = jnp.where(live, shift + j[None, :], trash).astype(I32)
    src = jnp.where(live, shift + j[None, :], 0).astype(I32)
    subs = (seg + CPH - 1) // CPH
    tl = jnp.arange(CPH - 1, dtype=I32)
    tail_len = subs * CPH - seg
    tails = jnp.where(tl[None, :] < tail_len[:, None], (seg_start + seg)[:, None] + tl[None, :], -1)
    nused = seg_end_b[-1]
    blk = jnp.arange(nblk_cap, dtype=I32)
    bexp = jnp.sum((blk[:, None] >= seg_end_b[None, :]).astype(I32), axis=-1)
    last = jnp.sum((nused - 1 >= seg_end_b).astype(I32))
    bexp = jnp.where(blk < nused, bexp, last).astype(I32)
    onehot = (bexp[:, None] == jnp.arange(N_EXPERTS, dtype=I32)[None, :]).astype(I32)
    first_blk = seg_end_b - segb
    bsub = jnp.sum(onehot * (subs - NSUB * (blk[:, None] - first_blk[None, :])), axis=-1)
    bsub = jnp.where(blk < nused, jnp.clip(bsub, 1, NSUB), 0).astype(I32)
    eid = jnp.arange(N_EXPERTS, dtype=I32)
    owns = segb > 0
    ordinal = jnp.cumsum(owns.astype(I32)) - 1
    later = jnp.where(owns[None, :] & (eid[None, :] > eid[:, None]), eid[None, :], N_EXPERTS)
    nxt_e = jnp.min(later, axis=1)
    nxt_e = jnp.where(nxt_e < N_EXPERTS, nxt_e, -1)
    bpar = jnp.sum(onehot * (ordinal % 2)[None, :], axis=-1).astype(I32)
    bnext = jnp.sum(onehot * nxt_e[None, :], axis=-1).astype(I32)
    return dict(nrb=nrb.astype(I32), dst=dst.reshape(-1), src=src.reshape(-1),
                tails=tails.reshape(-1).astype(I32),
                nused=nused.reshape(1).astype(I32), bexp=bexp, bsub=bsub, bpar=bpar, bnext=bnext)


def _rows_copy(src, dst, sem):
    return pltpu.make_async_copy(src, dst, sem)


def _row_onehot(riota, pos, base, value):
    p = jnp.zeros((TM, TM), BF16)
    for k in range(TOP_K):
        d = jnp.clip(pos[k:k + 1] - base, -1, TM).astype(F32).astype(BF16)
        v = 1.0 if value is None else value[k:k + 1]
        p = jnp.where(riota == d, v, p)
    return p


def _dispatch_kernel(nt, nrb_ref, dst_ref, tails_ref, h2_ref, pos_ref, xs_hbm, buf, sem):
    i = pl.program_id(0)
    slot = i % 2

    def drain(tile, sl):
        def body(rb, carry):
            _rows_copy(buf.at[sl, pl.ds(0, TM)], xs_hbm.at[pl.ds(0, TM)], sem.at[sl]).wait()
            return carry
        lax.fori_loop(0, jnp.maximum(nrb_ref[tile], EAGER_RB), body, 0)

    @pl.when((i >= 2) & (i < nt))
    def _():
        drain(i - 2, slot)

    @pl.when(i < nt)
    def _():
        h2 = h2_ref[...]
        pos = pos_ref[...]
        riota = lax.broadcasted_iota(I32, (TM, TM), 0).astype(F32).astype(BF16)
        nrb = nrb_ref[i]

        def block(b):
            p = _row_onehot(riota, pos, b * TM, None)
            buf[slot, b * TM:(b + 1) * TM, :] = _dot(p, h2).astype(BF16)
            for c in range(CPH):
                d = dst_ref[i * NCH_CAP + b * CPH + c]
                _rows_copy(buf.at[slot, pl.ds(b * TM + c * CHUNK, CHUNK)],
                           xs_hbm.at[pl.ds(pl.multiple_of(d * CHUNK, CHUNK), CHUNK)],
                           sem.at[slot]).start()

        for b in range(EAGER_RB):
            block(b)
        for b in range(EAGER_RB, NRB_CAP):
            pl.when(b < nrb)(functools.partial(block, b))

    @pl.when(i == nt)
    def _():
        drain(nt - 2, nt % 2)
        drain(nt - 1, (nt - 1) % 2)
        buf[0, pl.ds(0, CHUNK), :] = jnp.zeros((CHUNK, D_MODEL), BF16)
        ntail = N_EXPERTS * (CPH - 1)

        def tail(j, carry):
            d = tails_ref[j]

            @pl.when(d >= 0)
            def _():
                _rows_copy(buf.at[0, pl.ds(0, CHUNK)],
                           xs_hbm.at[pl.ds(pl.multiple_of(d * CHUNK, CHUNK), CHUNK)], sem.at[0]).start()
            return carry

        lax.fori_loop(0, ntail, tail, 0)

        def tail_wait(j, carry):
            @pl.when(tails_ref[j] >= 0)
            def _():
                _rows_copy(buf.at[0, pl.ds(0, CHUNK)], xs_hbm.at[pl.ds(0, CHUNK)], sem.at[0]).wait()
            return carry

        lax.fori_loop(0, ntail, tail_wait, 0)


def _dispatch(tb, h2, pos_t, nt, nblk_cap):
    last = nt - 1
    return pl.pallas_call(
        functools.partial(_dispatch_kernel, nt),
        grid_spec=pltpu.PrefetchScalarGridSpec(
            num_scalar_prefetch=3,
            grid=(nt + 1,),
            in_specs=[pl.BlockSpec((TM, D_MODEL), lambda i, *_: (jnp.minimum(i, last), 0)),
                      pl.BlockSpec((TOP_K, TM), lambda i, *_: (0, jnp.minimum(i, last)))],
            out_specs=pl.BlockSpec(memory_space=pl.ANY),
            scratch_shapes=[pltpu.VMEM((2, RCAP, D_MODEL), BF16), pltpu.SemaphoreType.DMA((2,))],
        ),
        out_shape=jax.ShapeDtypeStruct((nblk_cap * BM + nt * SPARE * CHUNK, D_MODEL), BF16),
        compiler_params=_cparams(("arbitrary",)),
        name="moe_dispatch",
    )(tb["nrb"], tb["dst"], tb["tails"], h2, pos_t)


RING = 3
CSLOT = 3


def _expert_kernel(layer, bexp_ref, bsub_ref, bpar_ref, bnext_ref, nused_ref,
                   xs_hbm, wg_hbm, wu_hbm, wd_hbm, ys_hbm,
                   xb, yb, stg_g, stg_u, stg_d, wgu, wdn, xsem, ysem, wsem):
    i = pl.program_id(0)
    nused = nused_ref[0]

    def x_copy(blk, sl, n):
        return _rows_copy(xs_hbm.at[pl.ds(pl.multiple_of(blk * BM, BM), n * HB)],
                          xb.at[sl, pl.ds(0, n * HB)], xsem.at[sl])

    def y_copy(blk, sl, n):
        return _rows_copy(yb.at[sl, pl.ds(0, n * HB)],
                          ys_hbm.at[pl.ds(pl.multiple_of(blk * BM, BM), n * HB)], ysem.at[sl])

    def sized(blk, fn):
        live = bsub_ref[blk]
        for n in range(1, NSUB + 1):
            pl.when(live == n)(functools.partial(fn, n))

    def w_copies(ex, sl):
        return (_rows_copy(wg_hbm.at[layer, ex], stg_g.at[sl], wsem.at[sl]),
                _rows_copy(wu_hbm.at[layer, ex], stg_u.at[sl], wsem.at[sl]),
                _rows_copy(wd_hbm.at[layer, ex], stg_d.at[sl], wsem.at[sl]))

    @pl.when(i == 0)
    def _():
        sized(0, lambda n: x_copy(0, 0, n).start())

        @pl.when(nused > 1)
        def _():
            sized(1, lambda n: x_copy(1, 1, n).start())
        for c in w_copies(bexp_ref[0], bpar_ref[0]):
            c.start()

    @pl.when(i < nused)
    def _():
        sl = i % RING

        @pl.when(i + 2 < nused)
        def _():
            sized(i + 2, lambda n: x_copy(i + 2, (i + 2) % RING, n).start())

        e = bexp_ref[i]
        par = bpar_ref[i]

        @pl.when((i == 0) | (e != bexp_ref[jnp.maximum(i - 1, 0)]))
        def _():
            for c in w_copies(e, par):
                c.wait()
            wgu[:, :D_EXPERT] = stg_g[par].astype(BF16)
            wgu[:, D_EXPERT:] = stg_u[par].astype(BF16)
            wdn[...] = stg_d[par].astype(BF16)
            nxt = bnext_ref[i]

            @pl.when(nxt >= 0)
            def _():
                for c in w_copies(nxt, 1 - par):
                    c.start()

        @pl.when(i >= RING)
        def _():
            sized(i - RING, lambda n: y_copy(i - RING, sl, n).wait())

        def step(n):
            x_copy(i, sl, n).wait()
            for c in range(n):
                r0 = c * HB
                gu = _dot(xb[sl, pl.ds(r0, HB), :], wgu[...])
                h = _silu(gu[:, :D_EXPERT]) * gu[:, D_EXPERT:]
                yb[sl, pl.ds(r0, HB), :] = _dot(h.astype(BF16), wdn[...]).astype(BF16)
            y_copy(i, sl, n).start()

        sized(i, step)

        @pl.when(i == nused - 1)
        def _():
            for back in range(RING):
                @pl.when(i - back >= 0)
                def _():
                    sized(i - back, lambda n, back=back: y_copy(i - back, (i - back) % RING, n).wait())


def _experts(tb, xs, w_gate, w_up, w_down, layer, nblk_cap):
    hbm = pl.BlockSpec(memory_space=pl.ANY)
    return pl.pallas_call(
        functools.partial(_expert_kernel, layer),
        grid_spec=pltpu.PrefetchScalarGridSpec(
            num_scalar_prefetch=5,
            grid=(nblk_cap,),
            in_specs=[hbm, hbm, hbm, hbm],
            out_specs=hbm,
            scratch_shapes=[pltpu.VMEM((RING, BM, D_MODEL), BF16),
                            pltpu.VMEM((RING, BM, D_MODEL), BF16),
                            pltpu.VMEM((2, D_MODEL, D_EXPERT), F32),
                            pltpu.VMEM((2, D_MODEL, D_EXPERT), F32),
                            pltpu.VMEM((2, D_EXPERT, D_MODEL), F32),
                            pltpu.VMEM((D_MODEL, 2 * D_EXPERT), BF16),
                            pltpu.VMEM((D_EXPERT, D_MODEL), BF16),
                            pltpu.SemaphoreType.DMA((RING,)),
                            pltpu.SemaphoreType.DMA((RING,)),
                            pltpu.SemaphoreType.DMA((2,))],
        ),
        out_shape=jax.ShapeDtypeStruct((nblk_cap * BM, D_MODEL), BF16),
        compiler_params=_cparams(("arbitrary",)),
        name="moe_experts",
    )(tb["bexp"], tb["bsub"], tb["bpar"], tb["bnext"], tb["nused"], xs, w_gate, w_up, w_down)


def _combine_kernel(final, nt, nrb_ref, src_ref, ys_hbm, pos_ref, gate_ref, h2_ref, x1_ref,
                    mod_ref, sgu_ref, sdn_ref, fg_ref, o_ref, buf, acc, sem):
    i = pl.program_id(0)
    slot = i % CSLOT
    nxt = jnp.minimum(i + CSLOT - 1, nt - 1)
    nrb = nrb_ref[i]
    nrb_next = nrb_ref[nxt]
    nslot = (i + CSLOT - 1) % CSLOT

    def fetch(tile, sl, b):
        for c in range(CPH):
            s = src_ref[tile * NCH_CAP + b * CPH + c]
            _rows_copy(ys_hbm.at[pl.ds(pl.multiple_of(s * CHUNK, CHUNK), CHUNK)],
                       buf.at[sl, pl.ds(b * TM + c * CHUNK, CHUNK)], sem.at[sl]).start()

    def wait_blocks(sl, n):
        def body(rb, carry):
            _rows_copy(ys_hbm.at[pl.ds(0, TM)], buf.at[sl, pl.ds(0, TM)], sem.at[sl]).wait()
            return carry
        lax.fori_loop(0, jnp.maximum(n, EAGER_RB), body, 0)

    @pl.when(i == 0)
    def _():
        for tile in range(min(CSLOT - 1, nt)):
            for b in range(EAGER_RB):
                fetch(tile, tile, b)
            for b in range(EAGER_RB, NRB_CAP):
                pl.when(b < nrb_ref[tile])(functools.partial(fetch, tile, tile, b))

    wait_blocks(slot, nrb)

    pos = pos_ref[...]
    gate = gate_ref[...].astype(BF16)
    riota = lax.broadcasted_iota(I32, (TM, TM), 0).astype(F32).astype(BF16)
    tlhs = (((0,), (0,)), ((), ()))

    gu = _dot(h2_ref[...], sgu_ref[...])
    hs = _silu(gu[:, :D_EXPERT]) * gu[:, D_EXPERT:]
    tot = _dot(hs.astype(BF16), sdn_ref[...])
    for b in range(EAGER_RB):
        fetch(nxt, nslot, b)
        pb = _row_onehot(riota, pos, b * TM, gate)
        tot = tot + lax.dot_general(pb, buf[slot, b * TM:(b + 1) * TM, :], tlhs,
                                    preferred_element_type=F32)
    acc[...] = tot
    for b in range(EAGER_RB, NRB_CAP):
        pl.when(b < nrb_next)(functools.partial(fetch, nxt, nslot, b))

        @pl.when(b < nrb)
        def _():
            pb = _row_onehot(riota, pos, b * TM, gate)
            acc[...] += lax.dot_general(pb, buf[slot, b * TM:(b + 1) * TM, :], tlhs,
                                        preferred_element_type=F32)

    @pl.when(i == nt - 1)
    def _():
        for back in range(min(CSLOT - 1, nt)):
            wait_blocks((i - back + CSLOT - 1) % CSLOT, nrb_next)

    x2 = x1_ref[...] + mod_ref[0][5:6] * acc[...]
    if final:
        x2 = x2 * lax.rsqrt(jnp.mean(x2 * x2, axis=-1, keepdims=True) + RMS_EPS) * fg_ref[...]
    o_ref[...] = x2


def _combine(tb, ys, pos, gate, h2, x1, mod, sgu, sdn, fg, nt, n_xt, tpb, nb, final):
    t = nt * TM
    row = lambda i, *_: (jnp.where(i < n_xt, i // tpb, nb), 0, 0)
    tile = lambda i, *_: (i, 0)
    slots = lambda i, *_: (0, i)
    const = lambda i, *_: (0, 0)
    return pl.pallas_call(
        functools.partial(_combine_kernel, final, nt),
        grid_spec=pltpu.PrefetchScalarGridSpec(
            num_scalar_prefetch=2,
            grid=(nt,),
            in_specs=[pl.BlockSpec(memory_space=pl.ANY),
                      pl.BlockSpec((TOP_K, TM), slots),
                      pl.BlockSpec((TOP_K, TM), slots),
                      pl.BlockSpec((TM, D_MODEL), tile),
                      pl.BlockSpec((TM, D_MODEL), tile),
                      pl.BlockSpec((1, 6, D_MODEL), row),
                      pl.BlockSpec(sgu.shape, const),
                      pl.BlockSpec(sdn.shape, const),
                      pl.BlockSpec((1, D_MODEL), const)],
            out_specs=pl.BlockSpec((TM, D_MODEL), tile),
            scratch_shapes=[pltpu.VMEM((CSLOT, RCAP, D_MODEL), BF16),
                            pltpu.VMEM((TM, D_MODEL), F32),
                            pltpu.SemaphoreType.DMA((CSLOT,))],
        ),
        out_shape=jax.ShapeDtypeStruct((t, D_MODEL), F32),
        compiler_params=_cparams(("arbitrary",)),
        name="moe_combine",
    )(tb["nrb"], tb["src"], ys, pos, gate, h2, x1, mod, sgu, sdn, fg)


def _moe(h2, pos_t, gate_t, cnt, x1, mod, w_gate, w_up, w_down, layer, sgu, sdn, fg,
         nt, n_xt, tpb, nb, final):
    nblk_cap = (TOP_K * nt * TM + (CHUNK - 1) * N_EXPERTS * nt) // BM + N_EXPERTS
    tb = _moe_tables(cnt[:, :, 0].astype(I32), nblk_cap)
    xs = _dispatch(tb, h2, pos_t, nt, nblk_cap)
    ys = _experts(tb, xs, w_gate, w_up, w_down, layer, nblk_cap)
    return _combine(tb, ys, pos_t, gate_t, h2, x1, mod, sgu, sdn, fg, nt, n_xt, tpb, nb, final)


def _lru_in_kernel(x_ref, mod_ref, g_ref, w_ref, gate_ref, u_ref):
    m = mod_ref[0]
    h = _norm_mod(x_ref[...], g_ref[...], m[0:1], m[1:2])
    p = _dot(h.astype(BF16), w_ref[...])
    z = p[:, :D_RNN]
    cdf = 0.5 * (1.0 + jnp.tanh(math.sqrt(2.0 / math.pi) * (z + 0.044715 * (z * z * z))))
    gate_ref[...] = (z * cdf).astype(BF16)
    u_ref[...] = p[:, D_RNN:]


def _lru_in(xall, mod, g, w, n_xt, tpb, nb):
    t = xall.shape[0]
    nt = t // TM
    row = lambda i: jnp.where(i < n_xt, i // tpb, nb)
    return pl.pallas_call(
        _lru_in_kernel,
        grid=(nt,),
        in_specs=[pl.BlockSpec((TM, D_MODEL), lambda i: (i, 0)),
                  pl.BlockSpec((1, 6, D_MODEL), lambda i: (row(i), 0, 0)),
                  pl.BlockSpec((1, D_MODEL), lambda i: (0, 0)),
                  pl.BlockSpec(w.shape, lambda i: (0, 0))],
        out_specs=[pl.BlockSpec((TM, D_RNN), lambda i: (i, 0)),
                   pl.BlockSpec((TM, D_RNN), lambda i: (i, 0))],
        out_shape=[jax.ShapeDtypeStruct((t, D_RNN), BF16),
                   jax.ShapeDtypeStruct((t, D_RNN), F32)],
        compiler_params=_cparams(("arbitrary",)),
        name="lru_in",
    )(xall, mod, g, w)


def _gate_windows():
    wins = []
    for c0 in range(0, D_RNN, 256):
        c1 = min(c0 + 256, D_RNN)
        k0 = (c0 // LRU_BLOCK_W) * LRU_BLOCK_W
        k1 = ((c1 - 1) // LRU_BLOCK_W + 1) * LRU_BLOCK_W
        wins.append((c0, c1, (k0 // 128) * 128, min(-(-k1 // 128) * 128, D_RNN)))
    return wins


def _block_diag_dot(ub, w_ref):
    return jnp.concatenate([_dot(ub[:, k0:k1], w_ref[k0:k1, c0:c1]) for c0, c1, k0, k1 in _gate_windows()],
                           axis=1)


def _lru_sweep_kernel(rev, mix, conv, ns, u_ref, up_ref, un_ref, cw_ref, cb_ref, wa_ref, wx_ref,
                      ba_ref, bx_ref, lam_ref, h0_ref, *rest):
    rest = list(rest)
    hf_ref, gate_ref = (rest.pop(0), rest.pop(0)) if mix else (None, None)
    o_ref = rest.pop(0)
    uc_ref = rest.pop(0) if conv == "emit" else None
    a_s, b_s, carry = rest
    s = pl.program_id(1)
    ss = (ns - 1 - s) if rev else s

    @pl.when(s == 0)
    def _():
        carry[...] = jnp.broadcast_to(h0_ref[0], (8, D_RNN))

    if conv == "reuse":
        u = u_ref[...]
    else:
        prev = jnp.where(ss > 0, up_ref[...], 0.0)
        nxt = jnp.where(ss < ns - 1, un_ref[...], 0.0)
        ext = jnp.concatenate([prev, u_ref[...], nxt], axis=0)
        n_ext = TM + 16
        cw = cw_ref[...]
        u = cb_ref[...]
        for j in range(CONV_W):
            shift = (CONV_LEFT - j) % n_ext
            tap = ext if shift == 0 else pltpu.roll(ext, shift, 0)
            u = u + cw[j:j + 1] * tap[8:8 + TM]
        if uc_ref is not None:
            uc_ref[...] = u

    ub = u.astype(BF16)
    ta = jnp.tanh(_block_diag_dot(ub, wa_ref) + ba_ref[...])
    tx = jnp.tanh(_block_diag_dot(ub, wx_ref) + bx_ref[...])
    nl = -lam_ref[...]
    c0 = (0.5 * LRU_C) * (jnp.maximum(nl, 0.0) + jnp.log1p(jnp.exp(-jnp.abs(nl))))
    y = c0 + c0 * ta
    a = jnp.exp(-y)
    a_s[...] = a
    hu = 0.5 * u
    b_s[...] = jnp.sqrt(jnp.tanh(y) * (1.0 + a * a)) * (hu + hu * tx)

    sub = lax.broadcasted_iota(I32, (8, D_RNN), 0)
    h = carry[...]
    ng = TM // 8
    for gi in range(ng):
        r0 = 8 * ((ng - 1 - gi) if rev else gi)
        a = a_s[r0:r0 + 8, :]
        b = b_s[r0:r0 + 8, :]
        for sh in (1, 2, 4):
            amt = (8 - sh) if rev else sh
            keep = (sub < 8 - sh) if rev else (sub >= sh)
            a_sh = pltpu.roll(a, amt, 0)
            b_sh = pltpu.roll(b, amt, 0)
            b = jnp.where(keep, a * b_sh + b, b)
            a = jnp.where(keep, a * a_sh, a)
        hg = a * h + b
        edge = hg[0:1] if rev else hg[7:8]
        h = jnp.broadcast_to(edge, (8, D_RNN))
        if mix:
            o_ref[r0:r0 + 8, :] = ((hf_ref[r0:r0 + 8, :] + hg)
                                   * gate_ref[r0:r0 + 8, :].astype(F32)).astype(BF16)
        else:
            o_ref[r0:r0 + 8, :] = hg
    carry[...] = h


def _lru_sweep(u_all, row0, nb, seq, cw, cb, wa, wx, ba, bx, lam, h0, rev, hf=None, gate=None,
               conv="compute"):
    ns = seq // TM
    blk0 = row0 // TM
    mix = hf is not None
    sidx = (lambda s: ns - 1 - s) if rev else (lambda s: s)
    cur = lambda b, s: (blk0 + b * ns + sidx(s), 0)
    out = lambda b, s: (b * ns + sidx(s), 0)
    h8 = TM // 8
    nblk8 = u_all.shape[0] // 8
    prv = lambda b, s: (jnp.maximum((blk0 + b * ns + sidx(s)) * h8 - 1, 0), 0)
    nxt = lambda b, s: (jnp.minimum((blk0 + b * ns + sidx(s) + 1) * h8, nblk8 - 1), 0)
    vec = lambda b, s: (0, 0)
    in_specs = [pl.BlockSpec((TM, D_RNN), cur),
                pl.BlockSpec((8, D_RNN), prv),
                pl.BlockSpec((8, D_RNN), nxt),
                pl.BlockSpec((CONV_W, D_RNN), vec),
                pl.BlockSpec((1, D_RNN), vec),
                pl.BlockSpec((D_RNN, D_RNN), vec),
                pl.BlockSpec((D_RNN, D_RNN), vec),
                pl.BlockSpec((1, D_RNN), vec),
                pl.BlockSpec((1, D_RNN), vec),
                pl.BlockSpec((1, D_RNN), vec),
                pl.BlockSpec((1, 1, D_RNN), lambda b, s: (b, 0, 0))]
    args = [u_all, u_all, u_all, cw, cb, wa, wx, ba, bx, lam, h0]
    if mix:
        in_specs += [pl.BlockSpec((TM, D_RNN), out), pl.BlockSpec((TM, D_RNN), cur)]
        args += [hf, gate]
    out_specs = [pl.BlockSpec((TM, D_RNN), out)]
    out_shape = [jax.ShapeDtypeStruct((nb * seq, D_RNN), BF16 if mix else F32)]
    if conv == "emit":
        out_specs.append(pl.BlockSpec((TM, D_RNN), out))
        out_shape.append(jax.ShapeDtypeStruct((nb * seq, D_RNN), F32))
    return pl.pallas_call(
        functools.partial(_lru_sweep_kernel, rev, mix, conv, ns),
        grid=(nb, ns),
        in_specs=in_specs,
        out_specs=out_specs,
        out_shape=out_shape,
        scratch_shapes=[pltpu.VMEM((TM, D_RNN), F32), pltpu.VMEM((TM, D_RNN), F32),
                        pltpu.VMEM((8, D_RNN), F32)],
        compiler_params=_cparams(("arbitrary", "arbitrary")),
        name="lru_sweep_rev" if rev else "lru_sweep_fwd",
    )(*args)


def _block_diag(w):
    rows = jnp.tile(w.reshape(D_RNN, LRU_BLOCK_W), (1, LRU_BLOCKS))
    blk = jnp.arange(D_RNN, dtype=I32) // LRU_BLOCK_W
    return jnp.where(blk[:, None] == blk[None, :], rows, 0.0).astype(BF16)


def kernel(x, c, ctx, c_ctx, ada_w, ada_b, norm1_g, norm2_g, attn_w_qkv, attn_w_o, attn_sink,
           lru_w_in, lru_conv_w, lru_conv_b, lru_wa, lru_ba, lru_wx, lru_bx, lru_lam, lru_w_out,
           moe_router_w, moe_router_bias, moe_w_gate, moe_w_up, moe_w_down,
           shared_w_gate, shared_w_up, shared_w_down, final_g):
    nb, seq, d = x.shape
    ctx_len = ctx.shape[1]
    assert d == D_MODEL and seq % TM == 0 and ctx_len == TM and nb < 8
    tx = nb * seq
    tall = tx + nb * ctx_len
    tpb = seq // TM
    n_xt = tx // TM
    nt_all = tall // TM

    cvec = jnp.zeros((8, d), F32).at[:nb].set(c).at[nb].set(c_ctx)
    mod = _adaln(cvec, ada_w, ada_b).reshape(DEPTH, 8, 6, d)
    x2d = x.reshape(tx, d)
    c2d = ctx.reshape(nb * ctx_len, d)
    row = lambda v: v.reshape(1, -1)
    shared = lambda i: (jnp.concatenate([shared_w_gate[i], shared_w_up[i]], axis=1).astype(BF16),
                        shared_w_down[i].astype(BF16))

    def router(i):
        hi = moe_router_w[i].astype(BF16)
        lo = (moe_router_w[i] - hi.astype(F32)).astype(BF16)
        return jnp.concatenate([hi, lo], axis=1)

    qd = N_HEADS * HEAD_DIM
    kd = N_KV_HEADS * HEAD_DIM
    wq = attn_w_qkv[0][:, :qd]
    wk = attn_w_qkv[0][:, qd:qd + kd].reshape(d, N_KV_HEADS, 1, HEAD_DIM)
    wv = attn_w_qkv[0][:, qd + kd:].reshape(d, N_KV_HEADS, 1, HEAD_DIM)
    dup = lambda w: jnp.broadcast_to(w, (d, N_KV_HEADS, 2, HEAD_DIM)).reshape(d, 2 * kd)
    wqkv = jnp.concatenate([wq, dup(wk), dup(wv)], axis=1).astype(BF16)
    cos, sin = _rope_tables(seq)
    q, k, v = _qkv(x2d, c2d, mod[0], row(norm1_g[0]), wqkv, cos, sin, n_xt, tpb, nb)
    o = _attention(q, k, v, attn_sink[0], nb, seq, ctx_len)
    x1, h2, pos_t, gate_t, cnt = _post(
        o, attn_w_o[0].astype(BF16), x2d, c2d, mod[0], row(norm2_g[0]),
        router(0), moe_router_bias[0].reshape(-1, 1), nt_all, n_xt, tpb, nb)
    sgu, sdn = shared(0)
    xall = _moe(h2, pos_t, gate_t, cnt, x1, mod[0], moe_w_gate, moe_w_up, moe_w_down, 0,
                sgu, sdn, row(final_g), nt_all, n_xt, tpb, nb, False)

    gate, u_pre = _lru_in(xall, mod[1], row(norm1_g[1]), lru_w_in[0].astype(BF16), n_xt, tpb, nb)
    cw, cb = lru_conv_w[0], row(lru_conv_b[0])
    hdir = []
    for dr, rev in ((0, False), (1, True)):
        wa = _block_diag(0.5 * lru_wa[0, dr])
        wx = _block_diag(0.5 * lru_wx[0, dr])
        prm = (cw, cb, wa, wx, row(0.5 * lru_ba[0, dr]), row(0.5 * lru_bx[0, dr]), row(lru_lam[0, dr]))
        zero = jnp.zeros((nb, 1, D_RNN), F32)
        hc = _lru_sweep(u_pre, tx, nb, ctx_len, *prm, zero, rev)[0].reshape(nb, ctx_len, D_RNN)
        h0 = hc[:, 0:1] if rev else hc[:, ctx_len - 1:ctx_len]
        if not rev:
            hdir = _lru_sweep(u_pre, 0, nb, seq, *prm, h0, rev, conv="emit")
        else:
            mixed = _lru_sweep(hdir[1], 0, nb, seq, *prm, h0, rev, hf=hdir[0], gate=gate,
                               conv="reuse")[0]
    x1, h2, pos_t, gate_t, cnt = _post(
        mixed, lru_w_out[0].astype(BF16), xall, xall, mod[1], row(norm2_g[1]),
        router(1), moe_router_bias[1].reshape(-1, 1), n_xt, n_xt, tpb, nb)
    sgu, sdn = shared(1)
    out = _moe(h2, pos_t, gate_t, cnt, x1, mod[1], moe_w_gate, moe_w_up, moe_w_down, 1,
               sgu, sdn, row(final_g), n_xt, n_xt, tpb, nb, True)
    return out.reshape(nb, seq, d)
```

```python
import functools
import math

import jax
import jax.numpy as jnp
from jax import lax
from jax.experimental import pallas as pl
from jax.experimental.pallas import tpu as pltpu

D_MODEL = 1024
DEPTH = 2
GRID_W = 64
HEAD_DIM = 64
N_HEADS = 16
N_KV_HEADS = 4
GROUP = N_HEADS // N_KV_HEADS
WINDOW = 128
ATTN_BLOCK = 128
ROPE_THETA = 10000.0
D_RNN = 1280
LRU_BLOCKS = 16
LRU_BLOCK_W = D_RNN // LRU_BLOCKS
CONV_W = 4
CONV_LEFT = 2
LRU_C = 8.0
N_EXPERTS = 64
TOP_K = 8
N_GROUPS = 8
TOPK_GROUPS = 4
D_EXPERT = 256
ROUTED_SCALE = 2.5
RMS_EPS = 1e-6

F32 = jnp.float32
BF16 = jnp.bfloat16
I32 = jnp.int32
HIGHEST = lax.Precision.HIGHEST
LOG2E = math.log2(math.e)

TM = 256
RT = 2
CHUNK = 16
HB = 256
NSUB = 4
BM = NSUB * HB
CPB = BM // CHUNK
CPH = HB // CHUNK
RCAP = ((TOP_K * TM + (CHUNK - 1) * N_EXPERTS + TM - 1) // TM) * TM
NCH_CAP = RCAP // CHUNK
NRB_CAP = RCAP // TM
MIN_RB = TOP_K
EAGER_RB = 10
SPARE = (NRB_CAP - MIN_RB) * CPH
VMEM_LIMIT = 56 * 1024 * 1024


def _cparams(sem):
    return pltpu.CompilerParams(dimension_semantics=sem, vmem_limit_bytes=VMEM_LIMIT)


def _silu(x):
    return x * jax.nn.sigmoid(x)


def _dot(a, b):
    return jnp.dot(a, b, preferred_element_type=F32)


def _dot_nt(a, b, precision=None):
    return lax.dot_general(a, b, (((1,), (1,)), ((), ())), precision=precision,
                           preferred_element_type=F32)


def _adaln_kernel(c_ref, w_ref, b_ref, o_ref):
    s = _silu(c_ref[...])
    o_ref[0] = jnp.dot(s, w_ref[0], precision=HIGHEST, preferred_element_type=F32) + b_ref[0]


def _adaln(cvec, ada_w, ada_b):
    nb = 1536
    d6 = 6 * D_MODEL
    return pl.pallas_call(
        _adaln_kernel,
        grid=(DEPTH, d6 // nb),
        in_specs=[pl.BlockSpec((8, D_MODEL), lambda l, j: (0, 0)),
                  pl.BlockSpec((1, D_MODEL, nb), lambda l, j: (l, 0, j)),
                  pl.BlockSpec((1, 1, nb), lambda l, j: (l, 0, j))],
        out_specs=pl.BlockSpec((1, 8, nb), lambda l, j: (l, 0, j)),
        out_shape=jax.ShapeDtypeStruct((DEPTH, 8, d6), F32),
        compiler_params=_cparams(("arbitrary", "arbitrary")),
        name="adaln",
    )(cvec, ada_w, ada_b.reshape(DEPTH, 1, d6))


def _norm_mod(x, g, shift, scale):
    xn = x * lax.rsqrt(jnp.mean(x * x, axis=-1, keepdims=True) + RMS_EPS) * g
    return xn * (1.0 + scale) + shift


def _qkv_kernel(n_xt, x_ref, c_ref, mod_ref, g_ref, w_ref, cos_ref, sin_ref, q_ref, k_ref, v_ref):
    m = mod_ref[0]
    x = jnp.where(pl.program_id(0) < n_xt, x_ref[...], c_ref[...])
    h = _norm_mod(x, g_ref[...], m[0:1], m[1:2])
    p = _dot(h.astype(BF16), w_ref[...])
    cos = cos_ref[...]
    sin = sin_ref[...]
    lane = lax.broadcasted_iota(I32, (TM, 128), 1)
    second = (lane & 16) != 0
    scale = HEAD_DIM ** -0.5 * LOG2E
    nq = N_HEADS * HEAD_DIM // 128
    nk = 2 * N_KV_HEADS * HEAD_DIM // 128
    for c in range(nq + nk):
        blk = p[:, 128 * c:128 * (c + 1)]
        partner = jnp.where(second, pltpu.roll(blk, 16, 1), pltpu.roll(blk, 112, 1))
        r = blk * cos + partner * sin
        if c < nq:
            q_ref[:, 128 * c:128 * (c + 1)] = (r * scale).astype(BF16)
        else:
            k_ref[:, 128 * (c - nq):128 * (c - nq + 1)] = r.astype(BF16)
    v_ref[...] = p[:, 128 * (nq + nk):].astype(BF16)


def _qkv(x2d, c2d, mod, g, w, cos, sin, n_xt, tpb, nb):
    t = x2d.shape[0] + c2d.shape[0]
    nt = t // TM
    kw = 2 * N_KV_HEADS * HEAD_DIM
    row = lambda i: jnp.where(i < n_xt, i // tpb, nb)
    pos = lambda i: jnp.where(i < n_xt, i % tpb, tpb)
    return pl.pallas_call(
        functools.partial(_qkv_kernel, n_xt),
        grid=(nt,),
        in_specs=[pl.BlockSpec((TM, D_MODEL), lambda i: (jnp.minimum(i, n_xt - 1), 0)),
                  pl.BlockSpec((TM, D_MODEL), lambda i: (jnp.maximum(i - n_xt, 0), 0)),
                  pl.BlockSpec((1, 6, D_MODEL), lambda i: (row(i), 0, 0)),
                  pl.BlockSpec((1, D_MODEL), lambda i: (0, 0)),
                  pl.BlockSpec(w.shape, lambda i: (0, 0)),
                  pl.BlockSpec((TM, 128), lambda i: (pos(i), 0)),
                  pl.BlockSpec((TM, 128), lambda i: (pos(i), 0))],
        out_specs=[pl.BlockSpec((TM, D_MODEL), lambda i: (i, 0)),
                   pl.BlockSpec((TM, kw), lambda i: (i, 0)),
                   pl.BlockSpec((TM, kw), lambda i: (i, 0))],
        out_shape=[jax.ShapeDtypeStruct((t, D_MODEL), BF16),
                   jax.ShapeDtypeStruct((t, kw), BF16),
                   jax.ShapeDtypeStruct((t, kw), BF16)],
        compiler_params=_cparams(("arbitrary",)),
        name="qkv_rope",
    )(x2d, c2d, mod, g, w, cos, sin)


def _rope_tables(seq):
    s = jnp.arange(seq)
    row = (s // GRID_W).astype(F32)
    col = (s % GRID_W).astype(F32)
    n_freq = HEAD_DIM // 4
    inv = jnp.exp(-math.log(ROPE_THETA) * jnp.arange(n_freq, dtype=F32) / n_freq)
    ar = row[:, None] * inv
    ac = col[:, None] * inv
    cos = jnp.concatenate([jnp.cos(ar), jnp.cos(ar), jnp.cos(ac), jnp.cos(ac)], axis=-1)
    sin = jnp.concatenate([-jnp.sin(ar), jnp.sin(ar), -jnp.sin(ac), jnp.sin(ac)], axis=-1)
    cos = jnp.concatenate([jnp.tile(cos, (1, 2)), jnp.ones((TM, 128), F32)], axis=0)
    sin = jnp.concatenate([jnp.tile(sin, (1, 2)), jnp.zeros((TM, 128), F32)], axis=0)
    return cos, sin


def _attn_kernel(nqb, sink_ref, q_ref, kp_ref, kc_ref, kn_ref, kx_ref,
                 vp_ref, vc_ref, vn_ref, vx_ref, o_ref):
    n = pl.program_id(1)
    qb = ATTN_BLOCK
    nloc = 3 * qb
    nkeys = nloc + kx_ref.shape[0]
    rows = GROUP * qb
    r = lax.broadcasted_iota(I32, (rows, nloc), 0) % qb
    c = lax.broadcasted_iota(I32, (rows, nloc), 1)
    d = c - r
    ok = (d >= 0) & (d <= 2 * WINDOW) & (n < nqb)
    ok = ok & ((n > 0) | (c >= qb)) & ((n < nqb - 1) | (c < 2 * qb))
    lane = lax.broadcasted_iota(I32, (qb, 128), 1)
    rsub = lax.broadcasted_iota(I32, (rows, 1), 0) // qb
    scores = []
    for j in range(N_KV_HEADS):
        ks = slice(128 * j, 128 * (j + 1))
        kall = jnp.concatenate([kp_ref[:, ks], kc_ref[:, ks], kn_ref[:, ks], kx_ref[:, ks]], axis=0)
        qs = []
        for g in range(GROUP):
            pair, half = divmod(g, 2)
            qp = q_ref[:, 256 * j + 128 * pair:256 * j + 128 * (pair + 1)]
            keep = (lane >= 64) if half else (lane < 64)
            qs.append(jnp.where(keep, qp, jnp.zeros_like(qp)))
        qst = jnp.concatenate(qs, axis=0)
        scores.append(_dot_nt(qst, kall))
    probs = []
    for j in range(N_KV_HEADS):
        sink = jnp.zeros((rows, 1), F32)
        for g in range(GROUP):
            sink = jnp.where(rsub == g, sink_ref[GROUP * j + g] * LOG2E, sink)
        s = scores[j]
        s = jnp.concatenate([jnp.where(ok, s[:, :nloc], -jnp.inf), s[:, nloc:]], axis=1)
        m = jnp.maximum(jnp.max(s, axis=-1, keepdims=True), sink)
        p = jnp.exp2(s - m)
        den = jnp.sum(p, axis=-1, keepdims=True) + jnp.exp2(sink - m)
        probs.append((p.astype(BF16), 1.0 / den))
    for j in range(N_KV_HEADS):
        ks = slice(128 * j, 128 * (j + 1))
        vall = jnp.concatenate([vp_ref[:, ks], vc_ref[:, ks], vn_ref[:, ks], vx_ref[:, ks]], axis=0)
        pb, rden = probs[j]
        o = _dot(pb, vall) * rden
        for pair in range(GROUP // 2):
            o0 = o[(2 * pair) * qb:(2 * pair + 1) * qb]
            o1 = o[(2 * pair + 1) * qb:(2 * pair + 2) * qb]
            o_ref[:, 256 * j + 128 * pair:256 * j + 128 * (pair + 1)] = (
                jnp.where(lane < 64, o0, o1).astype(BF16))


def _attention(q, k, v, sink, nb, seq, ctx_len):
    t = q.shape[0]
    qb = ATTN_BLOCK
    nqb = seq // qb
    ncb = ctx_len // qb
    xq = nb * nqb
    kw = k.shape[1]
    qrow = lambda b, n: jnp.where(n < nqb, b * nqb + n, xq + b * ncb + (n - nqb))
    kprev = lambda b, n, s: (b * nqb + jnp.clip(n - 1, 0, nqb - 1), 0)
    kcur = lambda b, n, s: (b * nqb + jnp.minimum(n, nqb - 1), 0)
    knext = lambda b, n, s: (b * nqb + jnp.minimum(n + 1, nqb - 1), 0)
    kctx = lambda b, n, s: (nb * seq // ctx_len + b, 0)
    kspec = lambda f: pl.BlockSpec((qb, kw), f)
    xspec = pl.BlockSpec((ctx_len, kw), kctx)
    return pl.pallas_call(
        functools.partial(_attn_kernel, nqb),
        grid_spec=pltpu.PrefetchScalarGridSpec(
            num_scalar_prefetch=1,
            grid=(nb, nqb + ncb),
            in_specs=[pl.BlockSpec((qb, D_MODEL), lambda b, n, s: (qrow(b, n), 0)),
                      kspec(kprev), kspec(kcur), kspec(knext), xspec,
                      kspec(kprev), kspec(kcur), kspec(knext), xspec],
            out_specs=pl.BlockSpec((qb, D_MODEL), lambda b, n, s: (qrow(b, n), 0)),
        ),
        out_shape=jax.ShapeDtypeStruct((t, D_MODEL), BF16),
        compiler_params=_cparams(("arbitrary", "arbitrary")),
        name="window_attn",
    )(sink, q, k, k, k, k, v, v, v, v)


def _post_kernel(n_xt, m_ref, w_ref, x_ref, c_ref, mod_ref, g_ref, rw_ref, rb_ref,
                 x1_ref, h2_ref, pos_ref, gate_ref, cnt_ref):
    latent = pl.program_id(0) * RT < n_xt
    for s in range(RT):
        _post_tile(s, latent, m_ref, w_ref, x_ref, c_ref, mod_ref, g_ref, rw_ref, rb_ref,
                   x1_ref, h2_ref, pos_ref, gate_ref, cnt_ref)


def _post_tile(s, latent, m_ref, w_ref, x_ref, c_ref, mod_ref, g_ref, rw_ref, rb_ref,
               x1_ref, h2_ref, pos_ref, gate_ref, cnt_ref):
    rows = slice(s * TM, (s + 1) * TM)
    md = mod_ref[0]
    y = _dot(m_ref[rows, :], w_ref[...])
    x = jnp.where(latent, x_ref[rows, :], c_ref[rows, :])
    x1 = x + md[2:3] * y
    x1_ref[rows, :] = x1
    h2 = _norm_mod(x1, g_ref[...], md[3:4], md[4:5])
    h2b = h2.astype(BF16)
    h2_ref[rows, :] = h2b

    ne = N_EXPERTS
    per = ne // N_GROUPS
    h2lo = (h2 - h2b.astype(F32)).astype(BF16)
    lg = _dot(h2b, rw_ref[...]) + _dot(h2lo, rw_ref[...])
    lg = lg + pltpu.roll(lg, ne, 1)
    logit = lg.T[0:ne]
    sc = jax.nn.sigmoid(logit)
    sel = sc + rb_ref[...]
    sub8 = lax.broadcasted_iota(I32, (per, TM), 0)
    gs = jnp.zeros((N_GROUPS, TM), F32)
    gi = lax.broadcasted_iota(I32, (N_GROUPS, TM), 0)
    for g in range(N_GROUPS):
        blk = sel[per * g:per * (g + 1)]
        m1 = jnp.max(blk, axis=0, keepdims=True)
        i1 = jnp.min(jnp.where(blk == m1, sub8, per), axis=0, keepdims=True)
        m2 = jnp.max(jnp.where(sub8 == i1, -jnp.inf, blk), axis=0, keepdims=True)
        gs = jnp.where(gi == g, m1 + m2, gs)
    grank = jnp.zeros((N_GROUPS, TM), F32)
    for g in range(N_GROUPS):
        v = gs[g:g + 1]
        grank = grank + jnp.where(gi > g, jnp.where(v >= gs, 1.0, 0.0), jnp.where(v > gs, 1.0, 0.0))
    ei = lax.broadcasted_iota(I32, (ne, TM), 0)
    gsel = jnp.zeros((ne, TM), F32)
    for g in range(N_GROUPS):
        gsel = jnp.where(ei // per == g, grank[g:g + 1], gsel)
    selm = jnp.where(gsel < TOPK_GROUPS, sel, -jnp.inf)
    eif = ei.astype(F32)
    s8 = jnp.zeros((ne, TM), F32)
    for _ in range(TOP_K):
        best = jnp.max(selm, axis=0, keepdims=True)
        first = jnp.min(jnp.where(selm == best, eif, float(ne)), axis=0, keepdims=True)
        hit = eif == first
        s8 = jnp.where(hit, 1.0, s8)
        selm = jnp.where(hit, -jnp.inf, selm)
    ws = s8 * sc
    gate = ws / jnp.sum(ws, axis=0, keepdims=True) * ROUTED_SCALE

    s8b = s8.astype(BF16)
    er = lax.broadcasted_iota(I32, (ne, ne), 0)
    ec = lax.broadcasted_iota(I32, (ne, ne), 1)
    lower = jnp.where(ec < er, 1.0, 0.0).astype(BF16)
    tr = lax.broadcasted_iota(I32, (TM, TM), 0)
    tc = lax.broadcasted_iota(I32, (TM, TM), 1)
    upper = jnp.where(tr < tc, 1.0, 0.0).astype(BF16)
    slot = _dot(lower, s8b)
    rank = _dot(s8b, upper)
    cnt = jnp.sum(s8, axis=1, keepdims=True)
    pc = jnp.floor((cnt + (CHUNK - 1)) * (1.0 / CHUNK))
    pcb = jnp.broadcast_to(pc, (ne, 128))
    cnt_ref[s] = pcb
    loc = _dot(lower, pcb.astype(BF16))[:, 0:1] * CHUNK
    prow = loc + rank
    k8 = lax.broadcasted_iota(I32, (TOP_K, TM), 0)
    pos_t = jnp.zeros((TOP_K, TM), F32)
    gate_t = jnp.zeros((TOP_K, TM), F32)
    for k in range(TOP_K):
        mk = jnp.where(slot == k, s8, 0.0)
        pos_t = jnp.where(k8 == k, jnp.sum(mk * prow, axis=0, keepdims=True), pos_t)
        gate_t = jnp.where(k8 == k, jnp.sum(mk * gate, axis=0, keepdims=True), gate_t)
    pos_ref[:, rows] = pos_t.astype(I32)
    gate_ref[:, rows] = gate_t


def _post(m, w, x2d, c2d, mod, g, rw_t, rb, nt, n_xt, tpb, nb):
    assert nt % RT == 0 and n_xt % RT == 0 and tpb % RT == 0
    t = nt * TM
    kd = m.shape[1]
    rm = RT * TM
    nxs = n_xt // RT
    row = lambda i: jnp.where(i < nxs, i * RT // tpb, nb)
    return pl.pallas_call(
        functools.partial(_post_kernel, n_xt),
        grid=(nt // RT,),
        in_specs=[pl.BlockSpec((rm, kd), lambda i: (i, 0)),
                  pl.BlockSpec(w.shape, lambda i: (0, 0)),
                  pl.BlockSpec((rm, D_MODEL), lambda i: (jnp.minimum(i, nxs - 1), 0)),
                  pl.BlockSpec((rm, D_MODEL), lambda i: (jnp.maximum(i - nxs, 0), 0)),
                  pl.BlockSpec((1, 6, D_MODEL), lambda i: (row(i), 0, 0)),
                  pl.BlockSpec((1, D_MODEL), lambda i: (0, 0)),
                  pl.BlockSpec((D_MODEL, 2 * N_EXPERTS), lambda i: (0, 0)),
                  pl.BlockSpec((N_EXPERTS, 1), lambda i: (0, 0))],
        out_specs=[pl.BlockSpec((rm, D_MODEL), lambda i: (i, 0)),
                   pl.BlockSpec((rm, D_MODEL), lambda i: (i, 0)),
                   pl.BlockSpec((TOP_K, rm), lambda i: (0, i)),
                   pl.BlockSpec((TOP_K, rm), lambda i: (0, i)),
                   pl.BlockSpec((RT, N_EXPERTS, 128), lambda i: (i, 0, 0))],
        out_shape=[jax.ShapeDtypeStruct((t, D_MODEL), F32),
                   jax.ShapeDtypeStruct((t, D_MODEL), BF16),
                   jax.ShapeDtypeStruct((TOP_K, t), I32),
                   jax.ShapeDtypeStruct((TOP_K, t), F32),
                   jax.ShapeDtypeStruct((nt, N_EXPERTS, 128), F32)],
        compiler_params=_cparams(("arbitrary",)),
        name="mix_out_route",
    )(m, w, x2d, c2d, mod, g, rw_t, rb)


def _moe_tables(pc, nblk_cap):
    nt = pc.shape[0]
    seg = jnp.sum(pc, axis=0)
    segb = (seg + CPB - 1) // CPB
    seg_end_b = jnp.cumsum(segb)
    seg_start = (seg_end_b - segb) * CPB
    goff = seg_start[None, :] + jnp.cumsum(pc, axis=0) - pc
    loc_end = jnp.cumsum(pc, axis=1)
    loc = loc_end - pc
    nch = loc_end[:, -1]
    nrb = (nch + CPH - 1) // CPH
    j = jnp.arange(NCH_CAP, dtype=I32)
    inrun = (j[None, :, None] >= loc[:, None, :]) & (j[None, :, None] < loc_end[:, None, :])
    shift = jnp.sum(jnp.where(inrun, (goff - loc)[:, None, :], 0), axis=-1)
    live = j[None, :] < nch[:, None]
    trash = nblk_cap * CPB + jnp.arange(nt, dtype=I32)[:, None] * SPARE + j[None, :] - MIN_RB * CPH
    dst = jnp.where(live, shift + j[None, :], trash).astype(I32)
    src = jnp.where(live, shift + j[None, :], 0).astype(I32)
    subs = (seg + CPH - 1) // CPH
    tl = jnp.arange(CPH - 1, dtype=I32)
    tail_len = subs * CPH - seg
    tails = jnp.where(tl[None, :] < tail_len[:, None], (seg_start + seg)[:, None] + tl[None, :], -1)
    nused = seg_end_b[-1]
    blk = jnp.arange(nblk_cap, dtype=I32)
    bexp = jnp.sum((blk[:, None] >= seg_end_b[None, :]).astype(I32), axis=-1)
    last = jnp.sum((nused - 1 >= seg_end_b).astype(I32))
    bexp = jnp.where(blk < nused, bexp, last).astype(I32)
    onehot = (bexp[:, None] == jnp.arange(N_EXPERTS, dtype=I32)[None, :]).astype(I32)
    first_blk = seg_end_b - segb
    bsub = jnp.sum(onehot * (subs - NSUB * (blk[:, None] - first_blk[None, :])), axis=-1)
    bsub = jnp.where(blk < nused, jnp.clip(bsub, 1, NSUB), 0).astype(I32)
    eid = jnp.arange(N_EXPERTS, dtype=I32)
    owns = segb > 0
    ordinal = jnp.cumsum(owns.astype(I32)) - 1
    later = jnp.where(owns[None, :] & (eid[None, :] > eid[:, None]), eid[None, :], N_EXPERTS)
    nxt_e = jnp.min(later, axis=1)
    nxt_e = jnp.where(nxt_e < N_EXPERTS, nxt_e, -1)
    bpar = jnp.sum(onehot * (ordinal % 2)[None, :], axis=-1).astype(I32)
    bnext = jnp.sum(onehot * nxt_e[None, :], axis=-1).astype(I32)
    return dict(nrb=nrb.astype(I32), dst=dst.reshape(-1), src=src.reshape(-1),
                tails=tails.reshape(-1).astype(I32),
                nused=nused.reshape(1).astype(I32), bexp=bexp, bsub=bsub, bpar=bpar, bnext=bnext)


def _rows_copy(src, dst, sem):
    return pltpu.make_async_copy(src, dst, sem)


def _row_onehot(riota, pos, base, value):
    p = jnp.zeros((TM, TM), BF16)
    for k in range(TOP_K):
        d = jnp.clip(pos[k:k + 1] - base, -1, TM).astype(F32).astype(BF16)
        v = 1.0 if value is None else value[k:k + 1]
        p = jnp.where(riota == d, v, p)
    return p


def _dispatch_kernel(nt, nrb_ref, dst_ref, tails_ref, h2_ref, pos_ref, xs_hbm, buf, sem):
    i = pl.program_id(0)
    slot = i % 2

    def drain(tile, sl):
        def body(rb, carry):
            _rows_copy(buf.at[sl, pl.ds(0, TM)], xs_hbm.at[pl.ds(0, TM)], sem.at[sl]).wait()
            return carry
        lax.fori_loop(0, jnp.maximum(nrb_ref[tile], EAGER_RB), body, 0)

    @pl.when((i >= 2) & (i < nt))
    def _():
        drain(i - 2, slot)

    @pl.when(i < nt)
    def _():
        h2 = h2_ref[...]
        pos = pos_ref[...]
        riota = lax.broadcasted_iota(I32, (TM, TM), 0).astype(F32).astype(BF16)
        nrb = nrb_ref[i]

        def block(b, p=None):
            if p is None:
                p = _row_onehot(riota, pos, b * TM, None)
            buf[slot, b * TM:(b + 1) * TM, :] = _dot(p, h2).astype(BF16)
            for c in range(CPH):
                d = dst_ref[i * NCH_CAP + b * CPH + c]
                _rows_copy(buf.at[slot, pl.ds(b * TM + c * CHUNK, CHUNK)],
                           xs_hbm.at[pl.ds(pl.multiple_of(d * CHUNK, CHUNK), CHUNK)],
                           sem.at[slot]).start()

        onehots = [_row_onehot(riota, pos, b * TM, None) for b in range(EAGER_RB)]
        for b in range(EAGER_RB):
            block(b, onehots[b])
        for b in range(EAGER_RB, NRB_CAP):
            pl.when(b < nrb)(functools.partial(block, b))

    @pl.when(i == nt)
    def _():
        drain(nt - 2, nt % 2)
        drain(nt - 1, (nt - 1) % 2)
        buf[0, pl.ds(0, CHUNK), :] = jnp.zeros((CHUNK, D_MODEL), BF16)
        ntail = N_EXPERTS * (CPH - 1)

        def tail(j, carry):
            d = tails_ref[j]

            @pl.when(d >= 0)
            def _():
                _rows_copy(buf.at[0, pl.ds(0, CHUNK)],
                           xs_hbm.at[pl.ds(pl.multiple_of(d * CHUNK, CHUNK), CHUNK)], sem.at[0]).start()
            return carry

        lax.fori_loop(0, ntail, tail, 0)

        def tail_wait(j, carry):
            @pl.when(tails_ref[j] >= 0)
            def _():
                _rows_copy(buf.at[0, pl.ds(0, CHUNK)], xs_hbm.at[pl.ds(0, CHUNK)], sem.at[0]).wait()
            return carry

        lax.fori_loop(0, ntail, tail_wait, 0)


def _dispatch(tb, h2, pos_t, nt, nblk_cap):
    last = nt - 1
    return pl.pallas_call(
        functools.partial(_dispatch_kernel, nt),
        grid_spec=pltpu.PrefetchScalarGridSpec(
            num_scalar_prefetch=3,
            grid=(nt + 1,),
            in_specs=[pl.BlockSpec((TM, D_MODEL), lambda i, *_: (jnp.minimum(i, last), 0)),
                      pl.BlockSpec((TOP_K, TM), lambda i, *_: (0, jnp.minimum(i, last)))],
            out_specs=pl.BlockSpec(memory_space=pl.ANY),
            scratch_shapes=[pltpu.VMEM((2, RCAP, D_MODEL), BF16), pltpu.SemaphoreType.DMA((2,))],
        ),
        out_shape=jax.ShapeDtypeStruct((nblk_cap * BM + nt * SPARE * CHUNK, D_MODEL), BF16),
        compiler_params=_cparams(("arbitrary",)),
        name="moe_dispatch",
    )(tb["nrb"], tb["dst"], tb["tails"], h2, pos_t)


RING = 3
CSLOT = 3


def _expert_kernel(layer, bexp_ref, bsub_ref, bpar_ref, bnext_ref, nused_ref,
                   xs_hbm, wg_hbm, wu_hbm, wd_hbm, ys_hbm,
                   xb, yb, stg_g, stg_u, stg_d, wgu, wdn, xsem, ysem, wsem):
    i = pl.program_id(0)
    nused = nused_ref[0]

    def x_copy(blk, sl, n):
        return _rows_copy(xs_hbm.at[pl.ds(pl.multiple_of(blk * BM, BM), n * HB)],
                          xb.at[sl, pl.ds(0, n * HB)], xsem.at[sl])

    def y_copy(blk, sl, n):
        return _rows_copy(yb.at[sl, pl.ds(0, n * HB)],
                          ys_hbm.at[pl.ds(pl.multiple_of(blk * BM, BM), n * HB)], ysem.at[sl])

    def sized(blk, fn):
        live = bsub_ref[blk]
        for n in range(1, NSUB + 1):
            pl.when(live == n)(functools.partial(fn, n))

    def w_copies(ex, sl):
        return (_rows_copy(wg_hbm.at[layer, ex], stg_g.at[sl], wsem.at[sl]),
                _rows_copy(wu_hbm.at[layer, ex], stg_u.at[sl], wsem.at[sl]),
                _rows_copy(wd_hbm.at[layer, ex], stg_d.at[sl], wsem.at[sl]))

    @pl.when(i == 0)
    def _():
        sized(0, lambda n: x_copy(0, 0, n).start())

        @pl.when(nused > 1)
        def _():
            sized(1, lambda n: x_copy(1, 1, n).start())
        for c in w_copies(bexp_ref[0], bpar_ref[0]):
            c.start()

    @pl.when(i < nused)
    def _():
        sl = i % RING

        @pl.when(i + 2 < nused)
        def _():
            sized(i + 2, lambda n: x_copy(i + 2, (i + 2) % RING, n).start())

        e = bexp_ref[i]
        par = bpar_ref[i]

        @pl.when((i == 0) | (e != bexp_ref[jnp.maximum(i - 1, 0)]))
        def _():
            for c in w_copies(e, par):
                c.wait()
            wgu[:, :D_EXPERT] = stg_g[par].astype(BF16)
            wgu[:, D_EXPERT:] = stg_u[par].astype(BF16)
            wdn[...] = stg_d[par].astype(BF16)
            nxt = bnext_ref[i]

            @pl.when(nxt >= 0)
            def _():
                for c in w_copies(nxt, 1 - par):
                    c.start()

        @pl.when(i >= RING)
        def _():
            sized(i - RING, lambda n: y_copy(i - RING, sl, n).wait())

        def step(n):
            x_copy(i, sl, n).wait()
            gus = [_dot(xb[sl, pl.ds(c * HB, HB), :], wgu[...]) for c in range(n)]
            hs = [(_silu(gu[:, :D_EXPERT]) * gu[:, D_EXPERT:]).astype(BF16) for gu in gus]
            for c in range(n):
                yb[sl, pl.ds(c * HB, HB), :] = _dot(hs[c], wdn[...]).astype(BF16)
            y_copy(i, sl, n).start()

        sized(i, step)

        @pl.when(i == nused - 1)
        def _():
            for back in range(RING):
                @pl.when(i - back >= 0)
                def _():
                    sized(i - back, lambda n, back=back: y_copy(i - back, (i - back) % RING, n).wait())


def _experts(tb, xs, w_gate, w_up, w_down, layer, nblk_cap):
    hbm = pl.BlockSpec(memory_space=pl.ANY)
    return pl.pallas_call(
        functools.partial(_expert_kernel, layer),
        grid_spec=pltpu.PrefetchScalarGridSpec(
            num_scalar_prefetch=5,
            grid=(nblk_cap,),
            in_specs=[hbm, hbm, hbm, hbm],
            out_specs=hbm,
            scratch_shapes=[pltpu.VMEM((RING, BM, D_MODEL), BF16),
                            pltpu.VMEM((RING, BM, D_MODEL), BF16),
                            pltpu.VMEM((2, D_MODEL, D_EXPERT), F32),
                            pltpu.VMEM((2, D_MODEL, D_EXPERT), F32),
                            pltpu.VMEM((2, D_EXPERT, D_MODEL), F32),
                            pltpu.VMEM((D_MODEL, 2 * D_EXPERT), BF16),
                            pltpu.VMEM((D_EXPERT, D_MODEL), BF16),
                            pltpu.SemaphoreType.DMA((RING,)),
                            pltpu.SemaphoreType.DMA((RING,)),
                            pltpu.SemaphoreType.DMA((2,))],
        ),
        out_shape=jax.ShapeDtypeStruct((nblk_cap * BM, D_MODEL), BF16),
        compiler_params=_cparams(("arbitrary",)),
        name="moe_experts",
    )(tb["bexp"], tb["bsub"], tb["bpar"], tb["bnext"], tb["nused"], xs, w_gate, w_up, w_down)


def _combine_kernel(final, nt, nrb_ref, src_ref, ys_hbm, pos_ref, gate_ref, h2_ref, x1_ref,
                    mod_ref, sgu_ref, sdn_ref, fg_ref, o_ref, buf, acc, sem):
    i = pl.program_id(0)
    slot = i % CSLOT
    nxt = jnp.minimum(i + CSLOT - 1, nt - 1)
    nrb = nrb_ref[i]
    nrb_next = nrb_ref[nxt]
    nslot = (i + CSLOT - 1) % CSLOT

    def fetch(tile, sl, b):
        for c in range(CPH):
            s = src_ref[tile * NCH_CAP + b * CPH + c]
            _rows_copy(ys_hbm.at[pl.ds(pl.multiple_of(s * CHUNK, CHUNK), CHUNK)],
                       buf.at[sl, pl.ds(b * TM + c * CHUNK, CHUNK)], sem.at[sl]).start()

    def wait_blocks(sl, n):
        def body(rb, carry):
            _rows_copy(ys_hbm.at[pl.ds(0, TM)], buf.at[sl, pl.ds(0, TM)], sem.at[sl]).wait()
            return carry
        lax.fori_loop(0, jnp.maximum(n, EAGER_RB), body, 0)

    @pl.when(i == 0)
    def _():
        for tile in range(min(CSLOT - 1, nt)):
            for b in range(EAGER_RB):
                fetch(tile, tile, b)
            for b in range(EAGER_RB, NRB_CAP):
                pl.when(b < nrb_ref[tile])(functools.partial(fetch, tile, tile, b))

    wait_blocks(slot, nrb)

    pos = pos_ref[...]
    gate = gate_ref[...].astype(BF16)
    riota = lax.broadcasted_iota(I32, (TM, TM), 0).astype(F32).astype(BF16)
    tlhs = (((0,), (0,)), ((), ()))

    gu = _dot(h2_ref[...], sgu_ref[...])
    hs = _silu(gu[:, :D_EXPERT]) * gu[:, D_EXPERT:]
    tot = _dot(hs.astype(BF16), sdn_ref[...])
    onehots = [_row_onehot(riota, pos, b * TM, gate) for b in range(EAGER_RB)]
    for b in range(EAGER_RB):
        fetch(nxt, nslot, b)
        tot = tot + lax.dot_general(onehots[b], buf[slot, b * TM:(b + 1) * TM, :], tlhs,
                                    preferred_element_type=F32)
    acc[...] = tot
    for b in range(EAGER_RB, NRB_CAP):
        pl.when(b < nrb_next)(functools.partial(fetch, nxt, nslot, b))

        @pl.when(b < nrb)
        def _():
            pb = _row_onehot(riota, pos, b * TM, gate)
            acc[...] += lax.dot_general(pb, buf[slot, b * TM:(b + 1) * TM, :], tlhs,
                                        preferred_element_type=F32)

    @pl.when(i == nt - 1)
    def _():
        for back in range(min(CSLOT - 1, nt)):
            wait_blocks((i - back + CSLOT - 1) % CSLOT, nrb_next)

    x2 = x1_ref[...] + mod_ref[0][5:6] * acc[...]
    if final:
        x2 = x2 * lax.rsqrt(jnp.mean(x2 * x2, axis=-1, keepdims=True) + RMS_EPS) * fg_ref[...]
    o_ref[...] = x2


def _combine(tb, ys, pos, gate, h2, x1, mod, sgu, sdn, fg, nt, n_xt, tpb, nb, final):
    t = nt * TM
    row = lambda i, *_: (jnp.where(i < n_xt, i // tpb, nb), 0, 0)
    tile = lambda i, *_: (i, 0)
    slots = lambda i, *_: (0, i)
    const = lambda i, *_: (0, 0)
    return pl.pallas_call(
        functools.partial(_combine_kernel, final, nt),
        grid_spec=pltpu.PrefetchScalarGridSpec(
            num_scalar_prefetch=2,
            grid=(nt,),
            in_specs=[pl.BlockSpec(memory_space=pl.ANY),
                      pl.BlockSpec((TOP_K, TM), slots),
                      pl.BlockSpec((TOP_K, TM), slots),
                      pl.BlockSpec((TM, D_MODEL), tile),
                      pl.BlockSpec((TM, D_MODEL), tile),
                      pl.BlockSpec((1, 6, D_MODEL), row),
                      pl.BlockSpec(sgu.shape, const),
                      pl.BlockSpec(sdn.shape, const),
                      pl.BlockSpec((1, D_MODEL), const)],
            out_specs=pl.BlockSpec((TM, D_MODEL), tile),
            scratch_shapes=[pltpu.VMEM((CSLOT, RCAP, D_MODEL), BF16),
                            pltpu.VMEM((TM, D_MODEL), F32),
                            pltpu.SemaphoreType.DMA((CSLOT,))],
        ),
        out_shape=jax.ShapeDtypeStruct((t, D_MODEL), F32),
        compiler_params=_cparams(("arbitrary",)),
        name="moe_combine",
    )(tb["nrb"], tb["src"], ys, pos, gate, h2, x1, mod, sgu, sdn, fg)


def _moe(h2, pos_t, gate_t, cnt, x1, mod, w_gate, w_up, w_down, layer, sgu, sdn, fg,
         nt, n_xt, tpb, nb, final):
    nblk_cap = (TOP_K * nt * TM + (CHUNK - 1) * N_EXPERTS * nt) // BM + N_EXPERTS
    tb = _moe_tables(cnt[:, :, 0].astype(I32), nblk_cap)
    xs = _dispatch(tb, h2, pos_t, nt, nblk_cap)
    ys = _experts(tb, xs, w_gate, w_up, w_down, layer, nblk_cap)
    return _combine(tb, ys, pos_t, gate_t, h2, x1, mod, sgu, sdn, fg, nt, n_xt, tpb, nb, final)


def _lru_in_kernel(x_ref, mod_ref, g_ref, w_ref, gate_ref, u_ref):
    m = mod_ref[0]
    h = _norm_mod(x_ref[...], g_ref[...], m[0:1], m[1:2])
    p = _dot(h.astype(BF16), w_ref[...])
    z = p[:, :D_RNN]
    cdf = 0.5 * (1.0 + jnp.tanh(math.sqrt(2.0 / math.pi) * (z + 0.044715 * (z * z * z))))
    gate_ref[...] = (z * cdf).astype(BF16)
    u_ref[...] = p[:, D_RNN:]


def _lru_in(xall, mod, g, w, n_xt, tpb, nb):
    t = xall.shape[0]
    nt = t // TM
    row = lambda i: jnp.where(i < n_xt, i // tpb, nb)
    return pl.pallas_call(
        _lru_in_kernel,
        grid=(nt,),
        in_specs=[pl.BlockSpec((TM, D_MODEL), lambda i: (i, 0)),
                  pl.BlockSpec((1, 6, D_MODEL), lambda i: (row(i), 0, 0)),
                  pl.BlockSpec((1, D_MODEL), lambda i: (0, 0)),
                  pl.BlockSpec(w.shape, lambda i: (0, 0))],
        out_specs=[pl.BlockSpec((TM, D_RNN), lambda i: (i, 0)),
                   pl.BlockSpec((TM, D_RNN), lambda i: (i, 0))],
        out_shape=[jax.ShapeDtypeStruct((t, D_RNN), BF16),
                   jax.ShapeDtypeStruct((t, D_RNN), F32)],
        compiler_params=_cparams(("arbitrary",)),
        name="lru_in",
    )(xall, mod, g, w)


def _gate_windows():
    wins = []
    for c0 in range(0, D_RNN, 256):
        c1 = min(c0 + 256, D_RNN)
        k0 = (c0 // LRU_BLOCK_W) * LRU_BLOCK_W
        k1 = ((c1 - 1) // LRU_BLOCK_W + 1) * LRU_BLOCK_W
        wins.append((c0, c1, (k0 // 128) * 128, min(-(-k1 // 128) * 128, D_RNN)))
    return wins


def _block_diag_dot(ub, w_ref):
    return jnp.concatenate([_dot(ub[:, k0:k1], w_ref[k0:k1, c0:c1]) for c0, c1, k0, k1 in _gate_windows()],
                           axis=1)


def _lru_sweep_kernel(rev, mix, conv, ns, u_ref, up_ref, un_ref, cw_ref, cb_ref, wa_ref, wx_ref,
                      ba_ref, bx_ref, lam_ref, h0_ref, *rest):
    rest = list(rest)
    hf_ref, gate_ref = (rest.pop(0), rest.pop(0)) if mix else (None, None)
    o_ref = rest.pop(0)
    uc_ref = rest.pop(0) if conv == "emit" else None
    a_s, b_s, carry = rest
    s = pl.program_id(1)
    ss = (ns - 1 - s) if rev else s

    @pl.when(s == 0)
    def _():
        carry[...] = jnp.broadcast_to(h0_ref[0], (8, D_RNN))

    if conv == "reuse":
        u = u_ref[...]
    else:
        prev = jnp.where(ss > 0, up_ref[...], 0.0)
        nxt = jnp.where(ss < ns - 1, un_ref[...], 0.0)
        ext = jnp.concatenate([prev, u_ref[...], nxt], axis=0)
        n_ext = TM + 16
        cw = cw_ref[...]
        u = cb_ref[...]
        for j in range(CONV_W):
            shift = (CONV_LEFT - j) % n_ext
            tap = ext if shift == 0 else pltpu.roll(ext, shift, 0)
            u = u + cw[j:j + 1] * tap[8:8 + TM]
        if uc_ref is not None:
            uc_ref[...] = u

    ub = u.astype(BF16)
    ta = jnp.tanh(_block_diag_dot(ub, wa_ref) + ba_ref[...])
    tx = jnp.tanh(_block_diag_dot(ub, wx_ref) + bx_ref[...])
    nl = -lam_ref[...]
    c0 = (0.5 * LRU_C) * (jnp.maximum(nl, 0.0) + jnp.log1p(jnp.exp(-jnp.abs(nl))))
    y = c0 + c0 * ta
    a = jnp.exp(-y)
    a_s[...] = a
    hu = 0.5 * u
    b_s[...] = jnp.sqrt(jnp.tanh(y) * (1.0 + a * a)) * (hu + hu * tx)

    sub = lax.broadcasted_iota(I32, (8, D_RNN), 0)
    h = carry[...]
    ng = TM // 8
    for gi in range(ng):
        r0 = 8 * ((ng - 1 - gi) if rev else gi)
        a = a_s[r0:r0 + 8, :]
        b = b_s[r0:r0 + 8, :]
        for sh in (1, 2, 4):
            amt = (8 - sh) if rev else sh
            keep = (sub < 8 - sh) if rev else (sub >= sh)
            a_sh = pltpu.roll(a, amt, 0)
            b_sh = pltpu.roll(b, amt, 0)
            b = jnp.where(keep, a * b_sh + b, b)
            a = jnp.where(keep, a * a_sh, a)
        hg = a * h + b
        edge = hg[0:1] if rev else hg[7:8]
        h = jnp.broadcast_to(edge, (8, D_RNN))
        if mix:
            o_ref[r0:r0 + 8, :] = ((hf_ref[r0:r0 + 8, :] + hg)
                                   * gate_ref[r0:r0 + 8, :].astype(F32)).astype(BF16)
        else:
            o_ref[r0:r0 + 8, :] = hg
    carry[...] = h


def _lru_sweep(u_all, row0, nb, seq, cw, cb, wa, wx, ba, bx, lam, h0, rev, hf=None, gate=None,
               conv="compute"):
    ns = seq // TM
    blk0 = row0 // TM
    mix = hf is not None
    sidx = (lambda s: ns - 1 - s) if rev else (lambda s: s)
    cur = lambda b, s: (blk0 + b * ns + sidx(s), 0)
    out = lambda b, s: (b * ns + sidx(s), 0)
    h8 = TM // 8
    nblk8 = u_all.shape[0] // 8
    prv = lambda b, s: (jnp.maximum((blk0 + b * ns + sidx(s)) * h8 - 1, 0), 0)
    nxt = lambda b, s: (jnp.minimum((blk0 + b * ns + sidx(s) + 1) * h8, nblk8 - 1), 0)
    vec = lambda b, s: (0, 0)
    in_specs = [pl.BlockSpec((TM, D_RNN), cur),
                pl.BlockSpec((8, D_RNN), prv),
                pl.BlockSpec((8, D_RNN), nxt),
                pl.BlockSpec((CONV_W, D_RNN), vec),
                pl.BlockSpec((1, D_RNN), vec),
                pl.BlockSpec((D_RNN, D_RNN), vec),
                pl.BlockSpec((D_RNN, D_RNN), vec),
                pl.BlockSpec((1, D_RNN), vec),
                pl.BlockSpec((1, D_RNN), vec),
                pl.BlockSpec((1, D_RNN), vec),
                pl.BlockSpec((1, 1, D_RNN), lambda b, s: (b, 0, 0))]
    args = [u_all, u_all, u_all, cw, cb, wa, wx, ba, bx, lam, h0]
    if mix:
        in_specs += [pl.BlockSpec((TM, D_RNN), out), pl.BlockSpec((TM, D_RNN), cur)]
        args += [hf, gate]
    out_specs = [pl.BlockSpec((TM, D_RNN), out)]
    out_shape = [jax.ShapeDtypeStruct((nb * seq, D_RNN), BF16 if mix else F32)]
    if conv == "emit":
        out_specs.append(pl.BlockSpec((TM, D_RNN), out))
        out_shape.append(jax.ShapeDtypeStruct((nb * seq, D_RNN), F32))
    return pl.pallas_call(
        functools.partial(_lru_sweep_kernel, rev, mix, conv, ns),
        grid=(nb, ns),
        in_specs=in_specs,
        out_specs=out_specs,
        out_shape=out_shape,
        scratch_shapes=[pltpu.VMEM((TM, D_RNN), F32), pltpu.VMEM((TM, D_RNN), F32),
                        pltpu.VMEM((8, D_RNN), F32)],
        compiler_params=_cparams(("arbitrary", "arbitrary")),
        name="lru_sweep_rev" if rev else "lru_sweep_fwd",
    )(*args)


def _block_diag(w):
    rows = jnp.tile(w.reshape(D_RNN, LRU_BLOCK_W), (1, LRU_BLOCKS))
    blk = jnp.arange(D_RNN, dtype=I32) // LRU_BLOCK_W
    return jnp.where(blk[:, None] == blk[None, :], rows, 0.0).astype(BF16)


def kernel(x, c, ctx, c_ctx, ada_w, ada_b, norm1_g, norm2_g, attn_w_qkv, attn_w_o, attn_sink,
           lru_w_in, lru_conv_w, lru_conv_b, lru_wa, lru_ba, lru_wx, lru_bx, lru_lam, lru_w_out,
           moe_router_w, moe_router_bias, moe_w_gate, moe_w_up, moe_w_down,
           shared_w_gate, shared_w_up, shared_w_down, final_g):
    nb, seq, d = x.shape
    ctx_len = ctx.shape[1]
    assert d == D_MODEL and seq % TM == 0 and ctx_len == TM and nb < 8
    tx = nb * seq
    tall = tx + nb * ctx_len
    tpb = seq // TM
    n_xt = tx // TM
    nt_all = tall // TM

    cvec = jnp.zeros((8, d), F32).at[:nb].set(c).at[nb].set(c_ctx)
    mod = _adaln(cvec, ada_w, ada_b).reshape(DEPTH, 8, 6, d)
    x2d = x.reshape(tx, d)
    c2d = ctx.reshape(nb * ctx_len, d)
    row = lambda v: v.reshape(1, -1)
    shared = lambda i: (jnp.concatenate([shared_w_gate[i], shared_w_up[i]], axis=1).astype(BF16),
                        shared_w_down[i].astype(BF16))

    def router(i):
        hi = moe_router_w[i].astype(BF16)
        lo = (moe_router_w[i] - hi.astype(F32)).astype(BF16)
        return jnp.concatenate([hi, lo], axis=1)

    qd = N_HEADS * HEAD_DIM
    kd = N_KV_HEADS * HEAD_DIM
    wq = attn_w_qkv[0][:, :qd]
    wk = attn_w_qkv[0][:, qd:qd + kd].reshape(d, N_KV_HEADS, 1, HEAD_DIM)
    wv = attn_w_qkv[0][:, qd + kd:].reshape(d, N_KV_HEADS, 1, HEAD_DIM)
    dup = lambda w: jnp.broadcast_to(w, (d, N_KV_HEADS, 2, HEAD_DIM)).reshape(d, 2 * kd)
    wqkv = jnp.concatenate([wq, dup(wk), dup(wv)], axis=1).astype(BF16)
    cos, sin = _rope_tables(seq)
    q, k, v = _qkv(x2d, c2d, mod[0], row(norm1_g[0]), wqkv, cos, sin, n_xt, tpb, nb)
    o = _attention(q, k, v, attn_sink[0], nb, seq, ctx_len)
    x1, h2, pos_t, gate_t, cnt = _post(
        o, attn_w_o[0].astype(BF16), x2d, c2d, mod[0], row(norm2_g[0]),
        router(0), moe_router_bias[0].reshape(-1, 1), nt_all, n_xt, tpb, nb)
    sgu, sdn = shared(0)
    xall = _moe(h2, pos_t, gate_t, cnt, x1, mod[0], moe_w_gate, moe_w_up, moe_w_down, 0,
                sgu, sdn, row(final_g), nt_all, n_xt, tpb, nb, False)

    gate, u_pre = _lru_in(xall, mod[1], row(norm1_g[1]), lru_w_in[0].astype(BF16), n_xt, tpb, nb)
    cw, cb = lru_conv_w[0], row(lru_conv_b[0])
    hdir = []
    for dr, rev in ((0, False), (1, True)):
        wa = _block_diag(0.5 * lru_wa[0, dr])
        wx = _block_diag(0.5 * lru_wx[0, dr])
        prm = (cw, cb, wa, wx, row(0.5 * lru_ba[0, dr]), row(0.5 * lru_bx[0, dr]), row(lru_lam[0, dr]))
        zero = jnp.zeros((nb, 1, D_RNN), F32)
        hc = _lru_sweep(u_pre, tx, nb, ctx_len, *prm, zero, rev)[0].reshape(nb, ctx_len, D_RNN)
        h0 = hc[:, 0:1] if rev else hc[:, ctx_len - 1:ctx_len]
        if not rev:
            hdir = _lru_sweep(u_pre, 0, nb, seq, *prm, h0, rev, conv="emit")
        else:
            mixed = _lru_sweep(hdir[1], 0, nb, seq, *prm, h0, rev, hf=hdir[0], gate=gate,
                               conv="reuse")[0]
    x1, h2, pos_t, gate_t, cnt = _post(
        mixed, lru_w_out[0].astype(BF16), xall, xall, mod[1], row(norm2_g[1]),
        router(1), moe_router_bias[1].reshape(-1, 1), n_xt, n_xt, tpb, nb)
    sgu, sdn = shared(1)
    out = _moe(h2, pos_t, gate_t, cnt, x1, mod[1], moe_w_gate, moe_w_up, moe_w_down, 1,
               sgu, sdn, row(final_g), n_xt, n_xt, tpb, nb, True)
    return out.reshape(nb, seq, d)
```

```python
import functools
import math

import jax
import jax.numpy as jnp
from jax import lax
from jax.experimental import pallas as pl
from jax.experimental.pallas import tpu as pltpu

D_MODEL = 1024
DEPTH = 2
GRID_W = 64
HEAD_DIM = 64
N_HEADS = 16
N_KV_HEADS = 4
GROUP = N_HEADS // N_KV_HEADS
WINDOW = 128
ATTN_BLOCK = 128
ROPE_THETA = 10000.0
D_RNN = 1280
LRU_BLOCKS = 16
LRU_BLOCK_W = D_RNN // LRU_BLOCKS
CONV_W = 4
CONV_LEFT = 2
LRU_C = 8.0
N_EXPERTS = 64
TOP_K = 8
N_GROUPS = 8
TOPK_GROUPS = 4
D_EXPERT = 256
ROUTED_SCALE = 2.5
RMS_EPS = 1e-6

F32 = jnp.float32
BF16 = jnp.bfloat16
I32 = jnp.int32
HIGHEST = lax.Precision.HIGHEST
LOG2E = math.log2(math.e)

TM = 256
RT = 4
CHUNK = 16
HB = 256
NSUB = 4
BM = NSUB * HB
CPB = BM // CHUNK
CPH = HB // CHUNK
RCAP = ((TOP_K * TM + (CHUNK - 1) * N_EXPERTS + TM - 1) // TM) * TM
NCH_CAP = RCAP // CHUNK
NRB_CAP = RCAP // TM
MIN_RB = TOP_K
EAGER_RB = 10
SPARE = (NRB_CAP - MIN_RB) * CPH
VMEM_LIMIT = 56 * 1024 * 1024


def _cparams(sem):
    return pltpu.CompilerParams(dimension_semantics=sem, vmem_limit_bytes=VMEM_LIMIT)


def _silu(x):
    return x * jax.nn.sigmoid(x)


def _dot(a, b):
    return jnp.dot(a, b, preferred_element_type=F32)


def _dot_nt(a, b, precision=None):
    return lax.dot_general(a, b, (((1,), (1,)), ((), ())), precision=precision,
                           preferred_element_type=F32)


def _adaln_kernel(c_ref, w_ref, b_ref, o_ref):
    s = _silu(c_ref[...])
    o_ref[0] = jnp.dot(s, w_ref[0], precision=HIGHEST, preferred_element_type=F32) + b_ref[0]


def _adaln(cvec, ada_w, ada_b):
    nb = 1536
    d6 = 6 * D_MODEL
    return pl.pallas_call(
        _adaln_kernel,
        grid=(DEPTH, d6 // nb),
        in_specs=[pl.BlockSpec((8, D_MODEL), lambda l, j: (0, 0)),
                  pl.BlockSpec((1, D_MODEL, nb), lambda l, j: (l, 0, j)),
                  pl.BlockSpec((1, 1, nb), lambda l, j: (l, 0, j))],
        out_specs=pl.BlockSpec((1, 8, nb), lambda l, j: (l, 0, j)),
        out_shape=jax.ShapeDtypeStruct((DEPTH, 8, d6), F32),
        compiler_params=_cparams(("arbitrary", "arbitrary")),
        name="adaln",
    )(cvec, ada_w, ada_b.reshape(DEPTH, 1, d6))


def _norm_mod(x, g, shift, scale):
    xn = x * lax.rsqrt(jnp.mean(x * x, axis=-1, keepdims=True) + RMS_EPS) * g
    return xn * (1.0 + scale) + shift


def _qkv_kernel(n_xt, x_ref, c_ref, mod_ref, g_ref, w_ref, cos_ref, sin_ref, q_ref, k_ref, v_ref):
    m = mod_ref[0]
    x = jnp.where(pl.program_id(0) < n_xt, x_ref[...], c_ref[...])
    h = _norm_mod(x, g_ref[...], m[0:1], m[1:2])
    p = _dot(h.astype(BF16), w_ref[...])
    cos = cos_ref[...]
    sin = sin_ref[...]
    lane = lax.broadcasted_iota(I32, (TM, 128), 1)
    second = (lane & 16) != 0
    scale = HEAD_DIM ** -0.5 * LOG2E
    nq = N_HEADS * HEAD_DIM // 128
    nk = 2 * N_KV_HEADS * HEAD_DIM // 128
    for c in range(nq + nk):
        blk = p[:, 128 * c:128 * (c + 1)]
        partner = jnp.where(second, pltpu.roll(blk, 16, 1), pltpu.roll(blk, 112, 1))
        r = blk * cos + partner * sin
        if c < nq:
            q_ref[:, 128 * c:128 * (c + 1)] = (r * scale).astype(BF16)
        else:
            k_ref[:, 128 * (c - nq):128 * (c - nq + 1)] = r.astype(BF16)
    v_ref[...] = p[:, 128 * (nq + nk):].astype(BF16)


def _qkv(x2d, c2d, mod, g, w, cos, sin, n_xt, tpb, nb):
    t = x2d.shape[0] + c2d.shape[0]
    nt = t // TM
    kw = 2 * N_KV_HEADS * HEAD_DIM
    row = lambda i: jnp.where(i < n_xt, i // tpb, nb)
    pos = lambda i: jnp.where(i < n_xt, i % tpb, tpb)
    return pl.pallas_call(
        functools.partial(_qkv_kernel, n_xt),
        grid=(nt,),
        in_specs=[pl.BlockSpec((TM, D_MODEL), lambda i: (jnp.minimum(i, n_xt - 1), 0)),
                  pl.BlockSpec((TM, D_MODEL), lambda i: (jnp.maximum(i - n_xt, 0), 0)),
                  pl.BlockSpec((1, 6, D_MODEL), lambda i: (row(i), 0, 0)),
                  pl.BlockSpec((1, D_MODEL), lambda i: (0, 0)),
                  pl.BlockSpec(w.shape, lambda i: (0, 0)),
                  pl.BlockSpec((TM, 128), lambda i: (pos(i), 0)),
                  pl.BlockSpec((TM, 128), lambda i: (pos(i), 0))],
        out_specs=[pl.BlockSpec((TM, D_MODEL), lambda i: (i, 0)),
                   pl.BlockSpec((TM, kw), lambda i: (i, 0)),
                   pl.BlockSpec((TM, kw), lambda i: (i, 0))],
        out_shape=[jax.ShapeDtypeStruct((t, D_MODEL), BF16),
                   jax.ShapeDtypeStruct((t, kw), BF16),
                   jax.ShapeDtypeStruct((t, kw), BF16)],
        compiler_params=_cparams(("arbitrary",)),
        name="qkv_rope",
    )(x2d, c2d, mod, g, w, cos, sin)


def _rope_tables(seq):
    s = jnp.arange(seq)
    row = (s // GRID_W).astype(F32)
    col = (s % GRID_W).astype(F32)
    n_freq = HEAD_DIM // 4
    inv = jnp.exp(-math.log(ROPE_THETA) * jnp.arange(n_freq, dtype=F32) / n_freq)
    ar = row[:, None] * inv
    ac = col[:, None] * inv
    cos = jnp.concatenate([jnp.cos(ar), jnp.cos(ar), jnp.cos(ac), jnp.cos(ac)], axis=-1)
    sin = jnp.concatenate([-jnp.sin(ar), jnp.sin(ar), -jnp.sin(ac), jnp.sin(ac)], axis=-1)
    cos = jnp.concatenate([jnp.tile(cos, (1, 2)), jnp.ones((TM, 128), F32)], axis=0)
    sin = jnp.concatenate([jnp.tile(sin, (1, 2)), jnp.zeros((TM, 128), F32)], axis=0)
    return cos, sin


def _attn_kernel(nqb, sink_ref, q_ref, kp_ref, kc_ref, kn_ref, kx_ref,
                 vp_ref, vc_ref, vn_ref, vx_ref, o_ref):
    n = pl.program_id(1)
    qb = ATTN_BLOCK
    nloc = 3 * qb
    nkeys = nloc + kx_ref.shape[0]
    rows = GROUP * qb
    r = lax.broadcasted_iota(I32, (rows, nloc), 0) % qb
    c = lax.broadcasted_iota(I32, (rows, nloc), 1)
    d = c - r
    ok = (d >= 0) & (d <= 2 * WINDOW) & (n < nqb)
    ok = ok & ((n > 0) | (c >= qb)) & ((n < nqb - 1) | (c < 2 * qb))
    lane = lax.broadcasted_iota(I32, (qb, 128), 1)
    rsub = lax.broadcasted_iota(I32, (rows, 1), 0) // qb
    def score(j):
        ks = slice(128 * j, 128 * (j + 1))
        kall = jnp.concatenate([kp_ref[:, ks], kc_ref[:, ks], kn_ref[:, ks], kx_ref[:, ks]], axis=0)
        qs = []
        for g in range(GROUP):
            pair, half = divmod(g, 2)
            qp = q_ref[:, 256 * j + 128 * pair:256 * j + 128 * (pair + 1)]
            keep = (lane >= 64) if half else (lane < 64)
            qs.append(jnp.where(keep, qp, jnp.zeros_like(qp)))
        qst = jnp.concatenate(qs, axis=0)
        return _dot_nt(qst, kall)

    def softmax(j, s):
        sink = jnp.zeros((rows, 1), F32)
        for g in range(GROUP):
            sink = jnp.where(rsub == g, sink_ref[GROUP * j + g] * LOG2E, sink)
        s = jnp.concatenate([jnp.where(ok, s[:, :nloc], -jnp.inf), s[:, nloc:]], axis=1)
        m = jnp.maximum(jnp.max(s, axis=-1, keepdims=True), sink)
        p = jnp.exp2(s - m)
        den = jnp.sum(p, axis=-1, keepdims=True) + jnp.exp2(sink - m)
        return p.astype(BF16), 1.0 / den

    def values(j, pb, rden):
        ks = slice(128 * j, 128 * (j + 1))
        vall = jnp.concatenate([vp_ref[:, ks], vc_ref[:, ks], vn_ref[:, ks], vx_ref[:, ks]], axis=0)
        o = _dot(pb, vall) * rden
        for pair in range(GROUP // 2):
            o0 = o[(2 * pair) * qb:(2 * pair + 1) * qb]
            o1 = o[(2 * pair + 1) * qb:(2 * pair + 2) * qb]
            o_ref[:, 256 * j + 128 * pair:256 * j + 128 * (pair + 1)] = (
                jnp.where(lane < 64, o0, o1).astype(BF16))

    nj = N_KV_HEADS
    sc = {0: score(0)}
    pr = {}
    for j in range(nj):
        if j + 1 < nj:
            sc[j + 1] = score(j + 1)
        pr[j] = softmax(j, sc.pop(j))
        if j >= 1:
            values(j - 1, *pr.pop(j - 1))
    values(nj - 1, *pr.pop(nj - 1))


def _attention(q, k, v, sink, nb, seq, ctx_len):
    t = q.shape[0]
    qb = ATTN_BLOCK
    nqb = seq // qb
    ncb = ctx_len // qb
    xq = nb * nqb
    kw = k.shape[1]
    qrow = lambda b, n: jnp.where(n < nqb, b * nqb + n, xq + b * ncb + (n - nqb))
    kprev = lambda b, n, s: (b * nqb + jnp.clip(n - 1, 0, nqb - 1), 0)
    kcur = lambda b, n, s: (b * nqb + jnp.minimum(n, nqb - 1), 0)
    knext = lambda b, n, s: (b * nqb + jnp.minimum(n + 1, nqb - 1), 0)
    kctx = lambda b, n, s: (nb * seq // ctx_len + b, 0)
    kspec = lambda f: pl.BlockSpec((qb, kw), f)
    xspec = pl.BlockSpec((ctx_len, kw), kctx)
    return pl.pallas_call(
        functools.partial(_attn_kernel, nqb),
        grid_spec=pltpu.PrefetchScalarGridSpec(
            num_scalar_prefetch=1,
            grid=(nb, nqb + ncb),
            in_specs=[pl.BlockSpec((qb, D_MODEL), lambda b, n, s: (qrow(b, n), 0)),
                      kspec(kprev), kspec(kcur), kspec(knext), xspec,
                      kspec(kprev), kspec(kcur), kspec(knext), xspec],
            out_specs=pl.BlockSpec((qb, D_MODEL), lambda b, n, s: (qrow(b, n), 0)),
        ),
        out_shape=jax.ShapeDtypeStruct((t, D_MODEL), BF16),
        compiler_params=_cparams(("arbitrary", "arbitrary")),
        name="window_attn",
    )(sink, q, k, k, k, k, v, v, v, v)


def _post_kernel(n_xt, m_ref, w_ref, x_ref, c_ref, mod_ref, g_ref, rw_ref, rb_ref,
                 x1_ref, h2_ref, pos_ref, gate_ref, cnt_ref):
    latent = pl.program_id(0) * RT < n_xt
    md = mod_ref[0]
    ne = N_EXPERTS
    per = ne // N_GROUPS
    tw = RT * TM
    ys = [_dot(m_ref[s * TM:(s + 1) * TM, :], w_ref[...]) for s in range(RT)]
    logits = []
    for s in range(RT):
        rows = slice(s * TM, (s + 1) * TM)
        x = jnp.where(latent, x_ref[rows, :], c_ref[rows, :])
        x1 = x + md[2:3] * ys[s]
        x1_ref[rows, :] = x1
        h2 = _norm_mod(x1, g_ref[...], md[3:4], md[4:5])
        h2b = h2.astype(BF16)
        h2_ref[rows, :] = h2b
        h2lo = (h2 - h2b.astype(F32)).astype(BF16)
        lg = _dot(h2b, rw_ref[...]) + _dot(h2lo, rw_ref[...])
        lg = lg + pltpu.roll(lg, ne, 1)
        logits.append(lg.T[0:ne])
    logit = jnp.concatenate(logits, axis=1)
    sc = jax.nn.sigmoid(logit)
    sel = sc + rb_ref[...]
    sub8 = lax.broadcasted_iota(I32, (per, tw), 0)
    gs = jnp.zeros((N_GROUPS, tw), F32)
    gi = lax.broadcasted_iota(I32, (N_GROUPS, tw), 0)
    for g in range(N_GROUPS):
        blk = sel[per * g:per * (g + 1)]
        m1 = jnp.max(blk, axis=0, keepdims=True)
        i1 = jnp.min(jnp.where(blk == m1, sub8, per), axis=0, keepdims=True)
        m2 = jnp.max(jnp.where(sub8 == i1, -jnp.inf, blk), axis=0, keepdims=True)
        gs = jnp.where(gi == g, m1 + m2, gs)
    grank = jnp.zeros((N_GROUPS, tw), F32)
    for g in range(N_GROUPS):
        v = gs[g:g + 1]
        grank = grank + jnp.where(gi > g, jnp.where(v >= gs, 1.0, 0.0), jnp.where(v > gs, 1.0, 0.0))
    ei = lax.broadcasted_iota(I32, (ne, tw), 0)
    gsel = jnp.zeros((ne, tw), F32)
    for g in range(N_GROUPS):
        gsel = jnp.where(ei // per == g, grank[g:g + 1], gsel)
    selm = jnp.where(gsel < TOPK_GROUPS, sel, -jnp.inf)
    eif = ei.astype(F32)
    s8 = jnp.zeros((ne, tw), F32)
    for _ in range(TOP_K):
        best = jnp.max(selm, axis=0, keepdims=True)
        first = jnp.min(jnp.where(selm == best, eif, float(ne)), axis=0, keepdims=True)
        hit = eif == first
        s8 = jnp.where(hit, 1.0, s8)
        selm = jnp.where(hit, -jnp.inf, selm)
    ws = s8 * sc
    gate = ws / jnp.sum(ws, axis=0, keepdims=True) * ROUTED_SCALE

    s8b = s8.astype(BF16)
    er = lax.broadcasted_iota(I32, (ne, ne), 0)
    ec = lax.broadcasted_iota(I32, (ne, ne), 1)
    lower = jnp.where(ec < er, 1.0, 0.0).astype(BF16)
    tr = lax.broadcasted_iota(I32, (TM, TM), 0)
    tc = lax.broadcasted_iota(I32, (TM, TM), 1)
    upper = jnp.where(tr < tc, 1.0, 0.0).astype(BF16)
    slot = _dot(lower, s8b)
    prows = []
    for s in range(RT):
        cols = slice(s * TM, (s + 1) * TM)
        rank = _dot(s8b[:, cols], upper)
        cnt = jnp.sum(s8[:, cols], axis=1, keepdims=True)
        pc = jnp.floor((cnt + (CHUNK - 1)) * (1.0 / CHUNK))
        pcb = jnp.broadcast_to(pc, (ne, 128))
        cnt_ref[s] = pcb
        loc = _dot(lower, pcb.astype(BF16))[:, 0:1] * CHUNK
        prows.append(loc + rank)
    prow = jnp.concatenate(prows, axis=1)
    k8 = lax.broadcasted_iota(I32, (TOP_K, tw), 0)
    pos_t = jnp.zeros((TOP_K, tw), F32)
    gate_t = jnp.zeros((TOP_K, tw), F32)
    for k in range(TOP_K):
        mk = jnp.where(slot == k, s8, 0.0)
        pos_t = jnp.where(k8 == k, jnp.sum(mk * prow, axis=0, keepdims=True), pos_t)
        gate_t = jnp.where(k8 == k, jnp.sum(mk * gate, axis=0, keepdims=True), gate_t)
    pos_ref[...] = pos_t.astype(I32)
    gate_ref[...] = gate_t


def _post(m, w, x2d, c2d, mod, g, rw_t, rb, nt, n_xt, tpb, nb):
    assert nt % RT == 0 and n_xt % RT == 0 and tpb % RT == 0
    t = nt * TM
    kd = m.shape[1]
    rm = RT * TM
    nxs = n_xt // RT
    row = lambda i: jnp.where(i < nxs, i * RT // tpb, nb)
    return pl.pallas_call(
        functools.partial(_post_kernel, n_xt),
        grid=(nt // RT,),
        in_specs=[pl.BlockSpec((rm, kd), lambda i: (i, 0)),
                  pl.BlockSpec(w.shape, lambda i: (0, 0)),
                  pl.BlockSpec((rm, D_MODEL), lambda i: (jnp.minimum(i, nxs - 1), 0)),
                  pl.BlockSpec((rm, D_MODEL), lambda i: (jnp.maximum(i - nxs, 0), 0)),
                  pl.BlockSpec((1, 6, D_MODEL), lambda i: (row(i), 0, 0)),
                  pl.BlockSpec((1, D_MODEL), lambda i: (0, 0)),
                  pl.BlockSpec((D_MODEL, 2 * N_EXPERTS), lambda i: (0, 0)),
                  pl.BlockSpec((N_EXPERTS, 1), lambda i: (0, 0))],
        out_specs=[pl.BlockSpec((rm, D_MODEL), lambda i: (i, 0)),
                   pl.BlockSpec((rm, D_MODEL), lambda i: (i, 0)),
                   pl.BlockSpec((TOP_K, rm), lambda i: (0, i)),
                   pl.BlockSpec((TOP_K, rm), lambda i: (0, i)),
                   pl.BlockSpec((RT, N_EXPERTS, 128), lambda i: (i, 0, 0))],
        out_shape=[jax.ShapeDtypeStruct((t, D_MODEL), F32),
                   jax.ShapeDtypeStruct((t, D_MODEL), BF16),
                   jax.ShapeDtypeStruct((TOP_K, t), I32),
                   jax.ShapeDtypeStruct((TOP_K, t), F32),
                   jax.ShapeDtypeStruct((nt, N_EXPERTS, 128), F32)],
        compiler_params=_cparams(("arbitrary",)),
        name="mix_out_route",
    )(m, w, x2d, c2d, mod, g, rw_t, rb)


def _moe_tables(pc, nblk_cap):
    nt = pc.shape[0]
    seg = jnp.sum(pc, axis=0)
    segb = (seg + CPB - 1) // CPB
    seg_end_b = jnp.cumsum(segb)
    seg_start = (seg_end_b - segb) * CPB
    goff = seg_start[None, :] + jnp.cumsum(pc, axis=0) - pc
    loc_end = jnp.cumsum(pc, axis=1)
    loc = loc_end - pc
    nch = loc_end[:, -1]
    nrb = (nch + CPH - 1) // CPH
    j = jnp.arange(NCH_CAP, dtype=I32)
    inrun = (j[None, :, None] >= loc[:, None, :]) & (j[None, :, None] < loc_end[:, None, :])
    shift = jnp.sum(jnp.where(inrun, (goff - loc)[:, None, :], 0), axis=-1)
    live = j[None, :] < nch[:, None]
    trash = nblk_cap * CPB + jnp.arange(nt, dtype=I32)[:, None] * SPARE + j[None, :] - MIN_RB * CPH
    dst = jnp.where(live, shift + j[None, :], trash).astype(I32)
    src = jnp.where(live, shift + j[None, :], 0).astype(I32)
    subs = (seg + CPH - 1) // CPH
    tl = jnp.arange(CPH - 1, dtype=I32)
    tail_len = subs * CPH - seg
    tails = jnp.where(tl[None, :] < tail_len[:, None], (seg_start + seg)[:, None] + tl[None, :], -1)
    nused = seg_end_b[-1]
    blk = jnp.arange(nblk_cap, dtype=I32)
    bexp = jnp.sum((blk[:, None] >= seg_end_b[None, :]).astype(I32), axis=-1)
    last = jnp.sum((nused - 1 >= seg_end_b).astype(I32))
    bexp = jnp.where(blk < nused, bexp, last).astype(I32)
    onehot = (bexp[:, None] == jnp.arange(N_EXPERTS, dtype=I32)[None, :]).astype(I32)
    first_blk = seg_end_b - segb
    bsub = jnp.sum(onehot * (subs - NSUB * (blk[:, None] - first_blk[None, :])), axis=-1)
    bsub = jnp.where(blk < nused, jnp.clip(bsub, 1, NSUB), 0).astype(I32)
    eid = jnp.arange(N_EXPERTS, dtype=I32)
    owns = segb > 0
    ordinal = jnp.cumsum(owns.astype(I32)) - 1
    later = jnp.where(owns[None, :] & (eid[None, :] > eid[:, None]), eid[None, :], N_EXPERTS)
    nxt_e = jnp.min(later, axis=1)
    nxt_e = jnp.where(nxt_e < N_EXPERTS, nxt_e, -1)
    bpar = jnp.sum(onehot * (ordinal % 2)[None, :], axis=-1).astype(I32)
    bnext = jnp.sum(onehot * nxt_e[None, :], axis=-1).astype(I32)
    return dict(nrb=nrb.astype(I32), dst=dst.reshape(-1), src=src.reshape(-1),
                tails=tails.reshape(-1).astype(I32),
                nused=nused.reshape(1).astype(I32), bexp=bexp, bsub=bsub, bpar=bpar, bnext=bnext)


def _rows_copy(src, dst, sem):
    return pltpu.make_async_copy(src, dst, sem)


def _row_onehot(riota, pos, base, value):
    p = jnp.zeros((TM, TM), BF16)
    for k in range(TOP_K):
        d = jnp.clip(pos[k:k + 1] - base, -1, TM).astype(F32).astype(BF16)
        v = 1.0 if value is None else value[k:k + 1]
        p = jnp.where(riota == d, v, p)
    return p


def _dispatch_kernel(nt, nrb_ref, dst_ref, tails_ref, h2_ref, pos_ref, xs_hbm, buf, sem):
    i = pl.program_id(0)
    slot = i % 2

    def drain(tile, sl):
        def body(rb, carry):
            _rows_copy(buf.at[sl, pl.ds(0, TM)], xs_hbm.at[pl.ds(0, TM)], sem.at[sl]).wait()
            return carry
        lax.fori_loop(0, jnp.maximum(nrb_ref[tile], EAGER_RB), body, 0)

    @pl.when((i >= 2) & (i < nt))
    def _():
        drain(i - 2, slot)

    @pl.when(i < nt)
    def _():
        h2 = h2_ref[...]
        pos = pos_ref[...]
        riota = lax.broadcasted_iota(I32, (TM, TM), 0).astype(F32).astype(BF16)
        nrb = nrb_ref[i]

        def rows_of(b):
            return _dot(_row_onehot(riota, pos, b * TM, None), h2).astype(BF16)

        def send(b, xb):
            buf[slot, b * TM:(b + 1) * TM, :] = xb
            for c in range(CPH):
                d = dst_ref[i * NCH_CAP + b * CPH + c]
                _rows_copy(buf.at[slot, pl.ds(b * TM + c * CHUNK, CHUNK)],
                           xs_hbm.at[pl.ds(pl.multiple_of(d * CHUNK, CHUNK), CHUNK)],
                           sem.at[slot]).start()

        prev = rows_of(0)
        for b in range(1, EAGER_RB):
            cur = rows_of(b)
            send(b - 1, prev)
            prev = cur
        send(EAGER_RB - 1, prev)
        for b in range(EAGER_RB, NRB_CAP):
            pl.when(b < nrb)(lambda b=b: send(b, rows_of(b)))

    @pl.when(i == nt)
    def _():
        drain(nt - 2, nt % 2)
        drain(nt - 1, (nt - 1) % 2)
        buf[0, pl.ds(0, CHUNK), :] = jnp.zeros((CHUNK, D_MODEL), BF16)
        ntail = N_EXPERTS * (CPH - 1)

        def tail(j, carry):
            d = tails_ref[j]

            @pl.when(d >= 0)
            def _():
                _rows_copy(buf.at[0, pl.ds(0, CHUNK)],
                           xs_hbm.at[pl.ds(pl.multiple_of(d * CHUNK, CHUNK), CHUNK)], sem.at[0]).start()
            return carry

        lax.fori_loop(0, ntail, tail, 0)

        def tail_wait(j, carry):
            @pl.when(tails_ref[j] >= 0)
            def _():
                _rows_copy(buf.at[0, pl.ds(0, CHUNK)], xs_hbm.at[pl.ds(0, CHUNK)], sem.at[0]).wait()
            return carry

        lax.fori_loop(0, ntail, tail_wait, 0)


def _dispatch(tb, h2, pos_t, nt, nblk_cap):
    last = nt - 1
    return pl.pallas_call(
        functools.partial(_dispatch_kernel, nt),
        grid_spec=pltpu.PrefetchScalarGridSpec(
            num_scalar_prefetch=3,
            grid=(nt + 1,),
            in_specs=[pl.BlockSpec((TM, D_MODEL), lambda i, *_: (jnp.minimum(i, last), 0)),
                      pl.BlockSpec((TOP_K, TM), lambda i, *_: (0, jnp.minimum(i, last)))],
            out_specs=pl.BlockSpec(memory_space=pl.ANY),
            scratch_shapes=[pltpu.VMEM((2, RCAP, D_MODEL), BF16), pltpu.SemaphoreType.DMA((2,))],
        ),
        out_shape=jax.ShapeDtypeStruct((nblk_cap * BM + nt * SPARE * CHUNK, D_MODEL), BF16),
        compiler_params=_cparams(("arbitrary",)),
        name="moe_dispatch",
    )(tb["nrb"], tb["dst"], tb["tails"], h2, pos_t)


RING = 3
CSLOT = 3


def _expert_kernel(layer, bexp_ref, bsub_ref, bpar_ref, bnext_ref, nused_ref,
                   xs_hbm, wg_hbm, wu_hbm, wd_hbm, ys_hbm,
                   xb, yb, stg_g, stg_u, stg_d, wgu, wdn, xsem, ysem, wsem):
    i = pl.program_id(0)
    nused = nused_ref[0]

    def x_copy(blk, sl, n):
        return _rows_copy(xs_hbm.at[pl.ds(pl.multiple_of(blk * BM, BM), n * HB)],
                          xb.at[sl, pl.ds(0, n * HB)], xsem.at[sl])

    def y_copy(blk, sl, n):
        return _rows_copy(yb.at[sl, pl.ds(0, n * HB)],
                          ys_hbm.at[pl.ds(pl.multiple_of(blk * BM, BM), n * HB)], ysem.at[sl])

    def sized(blk, fn):
        live = bsub_ref[blk]
        for n in range(1, NSUB + 1):
            pl.when(live == n)(functools.partial(fn, n))

    def w_copies(ex, sl):
        return (_rows_copy(wg_hbm.at[layer, ex], stg_g.at[sl], wsem.at[sl]),
                _rows_copy(wu_hbm.at[layer, ex], stg_u.at[sl], wsem.at[sl]),
                _rows_copy(wd_hbm.at[layer, ex], stg_d.at[sl], wsem.at[sl]))

    @pl.when(i == 0)
    def _():
        sized(0, lambda n: x_copy(0, 0, n).start())

        @pl.when(nused > 1)
        def _():
            sized(1, lambda n: x_copy(1, 1, n).start())
        for c in w_copies(bexp_ref[0], bpar_ref[0]):
            c.start()

    @pl.when(i < nused)
    def _():
        sl = i % RING

        @pl.when(i + 2 < nused)
        def _():
            sized(i + 2, lambda n: x_copy(i + 2, (i + 2) % RING, n).start())

        e = bexp_ref[i]
        par = bpar_ref[i]

        @pl.when((i == 0) | (e != bexp_ref[jnp.maximum(i - 1, 0)]))
        def _():
            for c in w_copies(e, par):
                c.wait()
            wgu[:, :D_EXPERT] = stg_g[par].astype(BF16)
            wgu[:, D_EXPERT:] = stg_u[par].astype(BF16)
            wdn[...] = stg_d[par].astype(BF16)
            nxt = bnext_ref[i]

            @pl.when(nxt >= 0)
            def _():
                for c in w_copies(nxt, 1 - par):
                    c.start()

        @pl.when(i >= RING)
        def _():
            sized(i - RING, lambda n: y_copy(i - RING, sl, n).wait())

        def step(n):
            x_copy(i, sl, n).wait()
            def up(c):
                return _dot(xb[sl, pl.ds(c * HB, HB), :], wgu[...])

            def act(gu):
                return (_silu(gu[:, :D_EXPERT]) * gu[:, D_EXPERT:]).astype(BF16)

            def down(c, h):
                yb[sl, pl.ds(c * HB, HB), :] = _dot(h, wdn[...]).astype(BF16)

            gus = {0: up(0)}
            hs = {}
            for c in range(n):
                if c + 1 < n:
                    gus[c + 1] = up(c + 1)
                hs[c] = act(gus.pop(c))
                if c >= 1:
                    down(c - 1, hs.pop(c - 1))
            down(n - 1, hs.pop(n - 1))
            y_copy(i, sl, n).start()

        sized(i, step)

        @pl.when(i == nused - 1)
        def _():
            for back in range(RING):
                @pl.when(i - back >= 0)
                def _():
                    sized(i - back, lambda n, back=back: y_copy(i - back, (i - back) % RING, n).wait())


def _experts(tb, xs, w_gate, w_up, w_down, layer, nblk_cap):
    hbm = pl.BlockSpec(memory_space=pl.ANY)
    return pl.pallas_call(
        functools.partial(_expert_kernel, layer),
        grid_spec=pltpu.PrefetchScalarGridSpec(
            num_scalar_prefetch=5,
            grid=(nblk_cap,),
            in_specs=[hbm, hbm, hbm, hbm],
            out_specs=hbm,
            scratch_shapes=[pltpu.VMEM((RING, BM, D_MODEL), BF16),
                            pltpu.VMEM((RING, BM, D_MODEL), BF16),
                            pltpu.VMEM((2, D_MODEL, D_EXPERT), F32),
                            pltpu.VMEM((2, D_MODEL, D_EXPERT), F32),
                            pltpu.VMEM((2, D_EXPERT, D_MODEL), F32),
                            pltpu.VMEM((D_MODEL, 2 * D_EXPERT), BF16),
                            pltpu.VMEM((D_EXPERT, D_MODEL), BF16),
                            pltpu.SemaphoreType.DMA((RING,)),
                            pltpu.SemaphoreType.DMA((RING,)),
                            pltpu.SemaphoreType.DMA((2,))],
        ),
        out_shape=jax.ShapeDtypeStruct((nblk_cap * BM, D_MODEL), BF16),
        compiler_params=_cparams(("arbitrary",)),
        name="moe_experts",
    )(tb["bexp"], tb["bsub"], tb["bpar"], tb["bnext"], tb["nused"], xs, w_gate, w_up, w_down)


def _combine_kernel(final, nt, nrb_ref, src_ref, ys_hbm, pos_ref, gate_ref, h2_ref, x1_ref,
                    mod_ref, sgu_ref, sdn_ref, fg_ref, o_ref, buf, acc, sem):
    i = pl.program_id(0)
    slot = i % CSLOT
    nxt = jnp.minimum(i + CSLOT - 1, nt - 1)
    nrb = nrb_ref[i]
    nrb_next = nrb_ref[nxt]
    nslot = (i + CSLOT - 1) % CSLOT

    def fetch(tile, sl, b):
        for c in range(CPH):
            s = src_ref[tile * NCH_CAP + b * CPH + c]
            _rows_copy(ys_hbm.at[pl.ds(pl.multiple_of(s * CHUNK, CHUNK), CHUNK)],
                       buf.at[sl, pl.ds(b * TM + c * CHUNK, CHUNK)], sem.at[sl]).start()

    def wait_blocks(sl, n):
        def body(rb, carry):
            _rows_copy(ys_hbm.at[pl.ds(0, TM)], buf.at[sl, pl.ds(0, TM)], sem.at[sl]).wait()
            return carry
        lax.fori_loop(0, jnp.maximum(n, EAGER_RB), body, 0)

    @pl.when(i == 0)
    def _():
        for tile in range(min(CSLOT - 1, nt)):
            for b in range(EAGER_RB):
                fetch(tile, tile, b)
            for b in range(EAGER_RB, NRB_CAP):
                pl.when(b < nrb_ref[tile])(functools.partial(fetch, tile, tile, b))

    wait_blocks(slot, nrb)

    pos = pos_ref[...]
    gate = gate_ref[...].astype(BF16)
    riota = lax.broadcasted_iota(I32, (TM, TM), 0).astype(F32).astype(BF16)
    tlhs = (((0,), (0,)), ((), ()))

    gu = _dot(h2_ref[...], sgu_ref[...])
    hs = _silu(gu[:, :D_EXPERT]) * gu[:, D_EXPERT:]
    tot = _dot(hs.astype(BF16), sdn_ref[...])
    onehots = [_row_onehot(riota, pos, b * TM, gate) for b in range(EAGER_RB)]
    for b in range(EAGER_RB):
        fetch(nxt, nslot, b)
        tot = tot + lax.dot_general(onehots[b], buf[slot, b * TM:(b + 1) * TM, :], tlhs,
                                    preferred_element_type=F32)
    acc[...] = tot
    for b in range(EAGER_RB, NRB_CAP):
        pl.when(b < nrb_next)(functools.partial(fetch, nxt, nslot, b))

        @pl.when(b < nrb)
        def _():
            pb = _row_onehot(riota, pos, b * TM, gate)
            acc[...] += lax.dot_general(pb, buf[slot, b * TM:(b + 1) * TM, :], tlhs,
                                        preferred_element_type=F32)

    @pl.when(i == nt - 1)
    def _():
        for back in range(min(CSLOT - 1, nt)):
            wait_blocks((i - back + CSLOT - 1) % CSLOT, nrb_next)

    x2 = x1_ref[...] + mod_ref[0][5:6] * acc[...]
    if final:
        x2 = x2 * lax.rsqrt(jnp.mean(x2 * x2, axis=-1, keepdims=True) + RMS_EPS) * fg_ref[...]
    o_ref[...] = x2


def _combine(tb, ys, pos, gate, h2, x1, mod, sgu, sdn, fg, nt, n_xt, tpb, nb, final):
    t = nt * TM
    row = lambda i, *_: (jnp.where(i < n_xt, i // tpb, nb), 0, 0)
    tile = lambda i, *_: (i, 0)
    slots = lambda i, *_: (0, i)
    const = lambda i, *_: (0, 0)
    return pl.pallas_call(
        functools.partial(_combine_kernel, final, nt),
        grid_spec=pltpu.PrefetchScalarGridSpec(
            num_scalar_prefetch=2,
            grid=(nt,),
            in_specs=[pl.BlockSpec(memory_space=pl.ANY),
                      pl.BlockSpec((TOP_K, TM), slots),
                      pl.BlockSpec((TOP_K, TM), slots),
                      pl.BlockSpec((TM, D_MODEL), tile),
                      pl.BlockSpec((TM, D_MODEL), tile),
                      pl.BlockSpec((1, 6, D_MODEL), row),
                      pl.BlockSpec(sgu.shape, const),
                      pl.BlockSpec(sdn.shape, const),
                      pl.BlockSpec((1, D_MODEL), const)],
            out_specs=pl.BlockSpec((TM, D_MODEL), tile),
            scratch_shapes=[pltpu.VMEM((CSLOT, RCAP, D_MODEL), BF16),
                            pltpu.VMEM((TM, D_MODEL), F32),
                            pltpu.SemaphoreType.DMA((CSLOT,))],
        ),
        out_shape=jax.ShapeDtypeStruct((t, D_MODEL), F32),
        compiler_params=_cparams(("arbitrary",)),
        name="moe_combine",
    )(tb["nrb"], tb["src"], ys, pos, gate, h2, x1, mod, sgu, sdn, fg)


def _moe(h2, pos_t, gate_t, cnt, x1, mod, w_gate, w_up, w_down, layer, sgu, sdn, fg,
         nt, n_xt, tpb, nb, final):
    nblk_cap = (TOP_K * nt * TM + (CHUNK - 1) * N_EXPERTS * nt) // BM + N_EXPERTS
    tb = _moe_tables(cnt[:, :, 0].astype(I32), nblk_cap)
    xs = _dispatch(tb, h2, pos_t, nt, nblk_cap)
    ys = _experts(tb, xs, w_gate, w_up, w_down, layer, nblk_cap)
    return _combine(tb, ys, pos_t, gate_t, h2, x1, mod, sgu, sdn, fg, nt, n_xt, tpb, nb, final)


def _lru_in_kernel(x_ref, mod_ref, g_ref, w_ref, gate_ref, u_ref):
    m = mod_ref[0]
    h = _norm_mod(x_ref[...], g_ref[...], m[0:1], m[1:2])
    p = _dot(h.astype(BF16), w_ref[...])
    z = p[:, :D_RNN]
    cdf = 0.5 * (1.0 + jnp.tanh(math.sqrt(2.0 / math.pi) * (z + 0.044715 * (z * z * z))))
    gate_ref[...] = (z * cdf).astype(BF16)
    u_ref[...] = p[:, D_RNN:]


def _lru_in(xall, mod, g, w, n_xt, tpb, nb):
    t = xall.shape[0]
    nt = t // TM
    row = lambda i: jnp.where(i < n_xt, i // tpb, nb)
    return pl.pallas_call(
        _lru_in_kernel,
        grid=(nt,),
        in_specs=[pl.BlockSpec((TM, D_MODEL), lambda i: (i, 0)),
                  pl.BlockSpec((1, 6, D_MODEL), lambda i: (row(i), 0, 0)),
                  pl.BlockSpec((1, D_MODEL), lambda i: (0, 0)),
                  pl.BlockSpec(w.shape, lambda i: (0, 0))],
        out_specs=[pl.BlockSpec((TM, D_RNN), lambda i: (i, 0)),
                   pl.BlockSpec((TM, D_RNN), lambda i: (i, 0))],
        out_shape=[jax.ShapeDtypeStruct((t, D_RNN), BF16),
                   jax.ShapeDtypeStruct((t, D_RNN), F32)],
        compiler_params=_cparams(("arbitrary",)),
        name="lru_in",
    )(xall, mod, g, w)


def _gate_windows():
    wins = []
    for c0 in range(0, D_RNN, 256):
        c1 = min(c0 + 256, D_RNN)
        k0 = (c0 // LRU_BLOCK_W) * LRU_BLOCK_W
        k1 = ((c1 - 1) // LRU_BLOCK_W + 1) * LRU_BLOCK_W
        wins.append((c0, c1, (k0 // 128) * 128, min(-(-k1 // 128) * 128, D_RNN)))
    return wins


def _block_diag_dot(ub, w_ref):
    return jnp.concatenate([_dot(ub[:, k0:k1], w_ref[k0:k1, c0:c1]) for c0, c1, k0, k1 in _gate_windows()],
                           axis=1)


def _lru_sweep_kernel(rev, mix, conv, ns, u_ref, up_ref, un_ref, cw_ref, cb_ref, wa_ref, wx_ref,
                      ba_ref, bx_ref, lam_ref, h0_ref, *rest):
    rest = list(rest)
    hf_ref, gate_ref = (rest.pop(0), rest.pop(0)) if mix else (None, None)
    o_ref = rest.pop(0)
    uc_ref = rest.pop(0) if conv == "emit" else None
    a_s, b_s, carry = rest
    s = pl.program_id(1)
    ss = (ns - 1 - s) if rev else s

    @pl.when(s == 0)
    def _():
        carry[...] = jnp.broadcast_to(h0_ref[0], (8, D_RNN))

    if conv == "reuse":
        u = u_ref[...]
    else:
        prev = jnp.where(ss > 0, up_ref[...], 0.0)
        nxt = jnp.where(ss < ns - 1, un_ref[...], 0.0)
        ext = jnp.concatenate([prev, u_ref[...], nxt], axis=0)
        n_ext = TM + 16
        cw = cw_ref[...]
        u = cb_ref[...]
        for j in range(CONV_W):
            shift = (CONV_LEFT - j) % n_ext
            tap = ext if shift == 0 else pltpu.roll(ext, shift, 0)
            u = u + cw[j:j + 1] * tap[8:8 + TM]
        if uc_ref is not None:
            uc_ref[...] = u

    ub = u.astype(BF16)
    ta = jnp.tanh(_block_diag_dot(ub, wa_ref) + ba_ref[...])
    tx = jnp.tanh(_block_diag_dot(ub, wx_ref) + bx_ref[...])
    nl = -lam_ref[...]
    c0 = (0.5 * LRU_C) * (jnp.maximum(nl, 0.0) + jnp.log1p(jnp.exp(-jnp.abs(nl))))
    y = c0 + c0 * ta
    a = jnp.exp(-y)
    a_s[...] = a
    hu = 0.5 * u
    b_s[...] = jnp.sqrt(jnp.tanh(y) * (1.0 + a * a)) * (hu + hu * tx)

    sub = lax.broadcasted_iota(I32, (8, D_RNN), 0)
    h = carry[...]
    ng = TM // 8
    for gi in range(ng):
        r0 = 8 * ((ng - 1 - gi) if rev else gi)
        a = a_s[r0:r0 + 8, :]
        b = b_s[r0:r0 + 8, :]
        for sh in (1, 2, 4):
            amt = (8 - sh) if rev else sh
            keep = (sub < 8 - sh) if rev else (sub >= sh)
            a_sh = pltpu.roll(a, amt, 0)
            b_sh = pltpu.roll(b, amt, 0)
            b = jnp.where(keep, a * b_sh + b, b)
            a = jnp.where(keep, a * a_sh, a)
        hg = a * h + b
        edge = hg[0:1] if rev else hg[7:8]
        h = jnp.broadcast_to(edge, (8, D_RNN))
        if mix:
            o_ref[r0:r0 + 8, :] = ((hf_ref[r0:r0 + 8, :] + hg)
                                   * gate_ref[r0:r0 + 8, :].astype(F32)).astype(BF16)
        else:
            o_ref[r0:r0 + 8, :] = hg
    carry[...] = h


def _lru_sweep(u_all, row0, nb, seq, cw, cb, wa, wx, ba, bx, lam, h0, rev, hf=None, gate=None,
               conv="compute"):
    ns = seq // TM
    blk0 = row0 // TM
    mix = hf is not None
    sidx = (lambda s: ns - 1 - s) if rev else (lambda s: s)
    cur = lambda b, s: (blk0 + b * ns + sidx(s), 0)
    out = lambda b, s: (b * ns + sidx(s), 0)
    h8 = TM // 8
    nblk8 = u_all.shape[0] // 8
    prv = lambda b, s: (jnp.maximum((blk0 + b * ns + sidx(s)) * h8 - 1, 0), 0)
    nxt = lambda b, s: (jnp.minimum((blk0 + b * ns + sidx(s) + 1) * h8, nblk8 - 1), 0)
    vec = lambda b, s: (0, 0)
    in_specs = [pl.BlockSpec((TM, D_RNN), cur),
                pl.BlockSpec((8, D_RNN), prv),
                pl.BlockSpec((8, D_RNN), nxt),
                pl.BlockSpec((CONV_W, D_RNN), vec),
                pl.BlockSpec((1, D_RNN), vec),
                pl.BlockSpec((D_RNN, D_RNN), vec),
                pl.BlockSpec((D_RNN, D_RNN), vec),
                pl.BlockSpec((1, D_RNN), vec),
                pl.BlockSpec((1, D_RNN), vec),
                pl.BlockSpec((1, D_RNN), vec),
                pl.BlockSpec((1, 1, D_RNN), lambda b, s: (b, 0, 0))]
    args = [u_all, u_all, u_all, cw, cb, wa, wx, ba, bx, lam, h0]
    if mix:
        in_specs += [pl.BlockSpec((TM, D_RNN), out), pl.BlockSpec((TM, D_RNN), cur)]
        args += [hf, gate]
    out_specs = [pl.BlockSpec((TM, D_RNN), out)]
    out_shape = [jax.ShapeDtypeStruct((nb * seq, D_RNN), BF16 if mix else F32)]
    if conv == "emit":
        out_specs.append(pl.BlockSpec((TM, D_RNN), out))
        out_shape.append(jax.ShapeDtypeStruct((nb * seq, D_RNN), F32))
    return pl.pallas_call(
        functools.partial(_lru_sweep_kernel, rev, mix, conv, ns),
        grid=(nb, ns),
        in_specs=in_specs,
        out_specs=out_specs,
        out_shape=out_shape,
        scratch_shapes=[pltpu.VMEM((TM, D_RNN), F32), pltpu.VMEM((TM, D_RNN), F32),
                        pltpu.VMEM((8, D_RNN), F32)],
        compiler_params=_cparams(("arbitrary", "arbitrary")),
        name="lru_sweep_rev" if rev else "lru_sweep_fwd",
    )(*args)


def _block_diag(w):
    rows = jnp.tile(w.reshape(D_RNN, LRU_BLOCK_W), (1, LRU_BLOCKS))
    blk = jnp.arange(D_RNN, dtype=I32) // LRU_BLOCK_W
    return jnp.where(blk[:, None] == blk[None, :], rows, 0.0).astype(BF16)


def kernel(x, c, ctx, c_ctx, ada_w, ada_b, norm1_g, norm2_g, attn_w_qkv, attn_w_o, attn_sink,
           lru_w_in, lru_conv_w, lru_conv_b, lru_wa, lru_ba, lru_wx, lru_bx, lru_lam, lru_w_out,
           moe_router_w, moe_router_bias, moe_w_gate, moe_w_up, moe_w_down,
           shared_w_gate, shared_w_up, shared_w_down, final_g):
    nb, seq, d = x.shape
    ctx_len = ctx.shape[1]
    assert d == D_MODEL and seq % TM == 0 and ctx_len == TM and nb < 8
    tx = nb * seq
    tall = tx + nb * ctx_len
    tpb = seq // TM
    n_xt = tx // TM
    nt_all = tall // TM

    cvec = jnp.zeros((8, d), F32).at[:nb].set(c).at[nb].set(c_ctx)
    mod = _adaln(cvec, ada_w, ada_b).reshape(DEPTH, 8, 6, d)
    x2d = x.reshape(tx, d)
    c2d = ctx.reshape(nb * ctx_len, d)
    row = lambda v: v.reshape(1, -1)
    shared = lambda i: (jnp.concatenate([shared_w_gate[i], shared_w_up[i]], axis=1).astype(BF16),
                        shared_w_down[i].astype(BF16))

    def router(i):
        hi = moe_router_w[i].astype(BF16)
        lo = (moe_router_w[i] - hi.astype(F32)).astype(BF16)
        return jnp.concatenate([hi, lo], axis=1)

    qd = N_HEADS * HEAD_DIM
    kd = N_KV_HEADS * HEAD_DIM
    wq = attn_w_qkv[0][:, :qd]
    wk = attn_w_qkv[0][:, qd:qd + kd].reshape(d, N_KV_HEADS, 1, HEAD_DIM)
    wv = attn_w_qkv[0][:, qd + kd:].reshape(d, N_KV_HEADS, 1, HEAD_DIM)
    dup = lambda w: jnp.broadcast_to(w, (d, N_KV_HEADS, 2, HEAD_DIM)).reshape(d, 2 * kd)
    wqkv = jnp.concatenate([wq, dup(wk), dup(wv)], axis=1).astype(BF16)
    cos, sin = _rope_tables(seq)
    q, k, v = _qkv(x2d, c2d, mod[0], row(norm1_g[0]), wqkv, cos, sin, n_xt, tpb, nb)
    o = _attention(q, k, v, attn_sink[0], nb, seq, ctx_len)
    x1, h2, pos_t, gate_t, cnt = _post(
        o, attn_w_o[0].astype(BF16), x2d, c2d, mod[0], row(norm2_g[0]),
        router(0), moe_router_bias[0].reshape(-1, 1), nt_all, n_xt, tpb, nb)
    sgu, sdn = shared(0)
    xall = _moe(h2, pos_t, gate_t, cnt, x1, mod[0], moe_w_gate, moe_w_up, moe_w_down, 0,
                sgu, sdn, row(final_g), nt_all, n_xt, tpb, nb, False)

    gate, u_pre = _lru_in(xall, mod[1], row(norm1_g[1]), lru_w_in[0].astype(BF16), n_xt, tpb, nb)
    cw, cb = lru_conv_w[0], row(lru_conv_b[0])
    hdir = []
    for dr, rev in ((0, False), (1, True)):
        wa = _block_diag(0.5 * lru_wa[0, dr])
        wx = _block_diag(0.5 * lru_wx[0, dr])
        prm = (cw, cb, wa, wx, row(0.5 * lru_ba[0, dr]), row(0.5 * lru_bx[0, dr]), row(lru_lam[0, dr]))
        zero = jnp.zeros((nb, 1, D_RNN), F32)
        hc = _lru_sweep(u_pre, tx, nb, ctx_len, *prm, zero, rev)[0].reshape(nb, ctx_len, D_RNN)
        h0 = hc[:, 0:1] if rev else hc[:, ctx_len - 1:ctx_len]
        if not rev:
            hdir = _lru_sweep(u_pre, 0, nb, seq, *prm, h0, rev, conv="emit")
        else:
            mixed = _lru_sweep(hdir[1], 0, nb, seq, *prm, h0, rev, hf=hdir[0], gate=gate,
                               conv="reuse")[0]
    x1, h2, pos_t, gate_t, cnt = _post(
        mixed, lru_w_out[0].astype(BF16), xall, xall, mod[1], row(norm2_g[1]),
        router(1), moe_router_bias[1].reshape(-1, 1), n_xt, n_xt, tpb, nb)
    sgu, sdn = shared(1)
    out = _moe(h2, pos_t, gate_t, cnt, x1, mod[1], moe_w_gate, moe_w_up, moe_w_down, 1,
               sgu, sdn, row(final_g), n_xt, n_xt, tpb, nb, True)
    return out.reshape(nb, seq, d)
```

```python
import functools
import math

import jax
import jax.numpy as jnp
from jax import lax
from jax.experimental import pallas as pl
from jax.experimental.pallas import tpu as pltpu

D_MODEL = 1024
DEPTH = 2
GRID_W = 64
HEAD_DIM = 64
N_HEADS = 16
N_KV_HEADS = 4
GROUP = N_HEADS // N_KV_HEADS
WINDOW = 128
ATTN_BLOCK = 128
ROPE_THETA = 10000.0
D_RNN = 1280
LRU_BLOCKS = 16
LRU_BLOCK_W = D_RNN // LRU_BLOCKS
CONV_W = 4
CONV_LEFT = 2
LRU_C = 8.0
N_EXPERTS = 64
TOP_K = 8
N_GROUPS = 8
TOPK_GROUPS = 4
D_EXPERT = 256
ROUTED_SCALE = 2.5
RMS_EPS = 1e-6

F32 = jnp.float32
BF16 = jnp.bfloat16
I32 = jnp.int32
HIGHEST = lax.Precision.HIGHEST
LOG2E = math.log2(math.e)

TM = 256
RT = 4
CHUNK = 16
HB = 256
NSUB = 6
BM = NSUB * HB
CPB = BM // CHUNK
CPH = HB // CHUNK
RCAP = ((TOP_K * TM + (CHUNK - 1) * N_EXPERTS + TM - 1) // TM) * TM
NCH_CAP = RCAP // CHUNK
NRB_CAP = RCAP // TM
MIN_RB = TOP_K
EAGER_RB = 10
SPARE = (NRB_CAP - MIN_RB) * CPH
VMEM_LIMIT = 56 * 1024 * 1024


def _cparams(sem):
    return pltpu.CompilerParams(dimension_semantics=sem, vmem_limit_bytes=VMEM_LIMIT)


def _silu(x):
    return x * jax.nn.sigmoid(x)


def _dot(a, b):
    return jnp.dot(a, b, preferred_element_type=F32)


def _dot_nt(a, b, precision=None):
    return lax.dot_general(a, b, (((1,), (1,)), ((), ())), precision=precision,
                           preferred_element_type=F32)


def _adaln_kernel(c_ref, w_ref, b_ref, o_ref):
    s = _silu(c_ref[...])
    o_ref[0] = jnp.dot(s, w_ref[0], precision=HIGHEST, preferred_element_type=F32) + b_ref[0]


def _adaln(cvec, ada_w, ada_b):
    nb = 1536
    d6 = 6 * D_MODEL
    return pl.pallas_call(
        _adaln_kernel,
        grid=(DEPTH, d6 // nb),
        in_specs=[pl.BlockSpec((8, D_MODEL), lambda l, j: (0, 0)),
                  pl.BlockSpec((1, D_MODEL, nb), lambda l, j: (l, 0, j)),
                  pl.BlockSpec((1, 1, nb), lambda l, j: (l, 0, j))],
        out_specs=pl.BlockSpec((1, 8, nb), lambda l, j: (l, 0, j)),
        out_shape=jax.ShapeDtypeStruct((DEPTH, 8, d6), F32),
        compiler_params=_cparams(("arbitrary", "arbitrary")),
        name="adaln",
    )(cvec, ada_w, ada_b.reshape(DEPTH, 1, d6))


def _norm_mod(x, g, shift, scale):
    xn = x * lax.rsqrt(jnp.mean(x * x, axis=-1, keepdims=True) + RMS_EPS) * g
    return xn * (1.0 + scale) + shift


def _qkv_kernel(n_xt, x_ref, c_ref, mod_ref, g_ref, w_ref, cos_ref, sin_ref, q_ref, k_ref, v_ref):
    m = mod_ref[0]
    x = jnp.where(pl.program_id(0) < n_xt, x_ref[...], c_ref[...])
    h = _norm_mod(x, g_ref[...], m[0:1], m[1:2])
    p = _dot(h.astype(BF16), w_ref[...])
    cos = cos_ref[...]
    sin = sin_ref[...]
    lane = lax.broadcasted_iota(I32, (TM, 128), 1)
    second = (lane & 16) != 0
    scale = HEAD_DIM ** -0.5 * LOG2E
    nq = N_HEADS * HEAD_DIM // 128
    nk = 2 * N_KV_HEADS * HEAD_DIM // 128
    for c in range(nq + nk):
        blk = p[:, 128 * c:128 * (c + 1)]
        partner = jnp.where(second, pltpu.roll(blk, 16, 1), pltpu.roll(blk, 112, 1))
        r = blk * cos + partner * sin
        if c < nq:
            q_ref[:, 128 * c:128 * (c + 1)] = (r * scale).astype(BF16)
        else:
            k_ref[:, 128 * (c - nq):128 * (c - nq + 1)] = r.astype(BF16)
    v_ref[...] = p[:, 128 * (nq + nk):].astype(BF16)


def _qkv(x2d, c2d, mod, g, w, cos, sin, n_xt, tpb, nb):
    t = x2d.shape[0] + c2d.shape[0]
    nt = t // TM
    kw = 2 * N_KV_HEADS * HEAD_DIM
    row = lambda i: jnp.where(i < n_xt, i // tpb, nb)
    pos = lambda i: jnp.where(i < n_xt, i % tpb, tpb)
    return pl.pallas_call(
        functools.partial(_qkv_kernel, n_xt),
        grid=(nt,),
        in_specs=[pl.BlockSpec((TM, D_MODEL), lambda i: (jnp.minimum(i, n_xt - 1), 0)),
                  pl.BlockSpec((TM, D_MODEL), lambda i: (jnp.maximum(i - n_xt, 0), 0)),
                  pl.BlockSpec((1, 6, D_MODEL), lambda i: (row(i), 0, 0)),
                  pl.BlockSpec((1, D_MODEL), lambda i: (0, 0)),
                  pl.BlockSpec(w.shape, lambda i: (0, 0)),
                  pl.BlockSpec((TM, 128), lambda i: (pos(i), 0)),
                  pl.BlockSpec((TM, 128), lambda i: (pos(i), 0))],
        out_specs=[pl.BlockSpec((TM, D_MODEL), lambda i: (i, 0)),
                   pl.BlockSpec((TM, kw), lambda i: (i, 0)),
                   pl.BlockSpec((TM, kw), lambda i: (i, 0))],
        out_shape=[jax.ShapeDtypeStruct((t, D_MODEL), BF16),
                   jax.ShapeDtypeStruct((t, kw), BF16),
                   jax.ShapeDtypeStruct((t, kw), BF16)],
        compiler_params=_cparams(("arbitrary",)),
        name="qkv_rope",
    )(x2d, c2d, mod, g, w, cos, sin)


def _rope_tables(seq):
    s = jnp.arange(seq)
    row = (s // GRID_W).astype(F32)
    col = (s % GRID_W).astype(F32)
    n_freq = HEAD_DIM // 4
    inv = jnp.exp(-math.log(ROPE_THETA) * jnp.arange(n_freq, dtype=F32) / n_freq)
    ar = row[:, None] * inv
    ac = col[:, None] * inv
    cos = jnp.concatenate([jnp.cos(ar), jnp.cos(ar), jnp.cos(ac), jnp.cos(ac)], axis=-1)
    sin = jnp.concatenate([-jnp.sin(ar), jnp.sin(ar), -jnp.sin(ac), jnp.sin(ac)], axis=-1)
    cos = jnp.concatenate([jnp.tile(cos, (1, 2)), jnp.ones((TM, 128), F32)], axis=0)
    sin = jnp.concatenate([jnp.tile(sin, (1, 2)), jnp.zeros((TM, 128), F32)], axis=0)
    return cos, sin


def _attn_kernel(nqb, sink_ref, q_ref, kp_ref, kc_ref, kn_ref, kx_ref,
                 vp_ref, vc_ref, vn_ref, vx_ref, o_ref):
    n = pl.program_id(1)
    qb = ATTN_BLOCK
    nloc = 3 * qb
    nkeys = nloc + kx_ref.shape[0]
    rows = GROUP * qb
    r = lax.broadcasted_iota(I32, (rows, nloc), 0) % qb
    c = lax.broadcasted_iota(I32, (rows, nloc), 1)
    d = c - r
    ok = (d >= 0) & (d <= 2 * WINDOW) & (n < nqb)
    ok = ok & ((n > 0) | (c >= qb)) & ((n < nqb - 1) | (c < 2 * qb))
    lane = lax.broadcasted_iota(I32, (qb, 128), 1)
    rsub = lax.broadcasted_iota(I32, (rows, 1), 0) // qb
    def score(j):
        ks = slice(128 * j, 128 * (j + 1))
        kall = jnp.concatenate([kp_ref[:, ks], kc_ref[:, ks], kn_ref[:, ks], kx_ref[:, ks]], axis=0)
        qs = []
        for g in range(GROUP):
            pair, half = divmod(g, 2)
            qp = q_ref[:, 256 * j + 128 * pair:256 * j + 128 * (pair + 1)]
            keep = (lane >= 64) if half else (lane < 64)
            qs.append(jnp.where(keep, qp, jnp.zeros_like(qp)))
        qst = jnp.concatenate(qs, axis=0)
        return _dot_nt(qst, kall)

    def softmax(j, s):
        sink = jnp.zeros((rows, 1), F32)
        for g in range(GROUP):
            sink = jnp.where(rsub == g, sink_ref[GROUP * j + g] * LOG2E, sink)
        s = jnp.concatenate([jnp.where(ok, s[:, :nloc], -jnp.inf), s[:, nloc:]], axis=1)
        m = jnp.maximum(jnp.max(s, axis=-1, keepdims=True), sink)
        p = jnp.exp2(s - m)
        den = jnp.sum(p, axis=-1, keepdims=True) + jnp.exp2(sink - m)
        return p.astype(BF16), 1.0 / den

    def values(j, pb, rden):
        ks = slice(128 * j, 128 * (j + 1))
        vall = jnp.concatenate([vp_ref[:, ks], vc_ref[:, ks], vn_ref[:, ks], vx_ref[:, ks]], axis=0)
        o = _dot(pb, vall) * rden
        for pair in range(GROUP // 2):
            o0 = o[(2 * pair) * qb:(2 * pair + 1) * qb]
            o1 = o[(2 * pair + 1) * qb:(2 * pair + 2) * qb]
            o_ref[:, 256 * j + 128 * pair:256 * j + 128 * (pair + 1)] = (
                jnp.where(lane < 64, o0, o1).astype(BF16))

    nj = N_KV_HEADS
    sc = {0: score(0)}
    pr = {}
    for j in range(nj):
        if j + 1 < nj:
            sc[j + 1] = score(j + 1)
        pr[j] = softmax(j, sc.pop(j))
        if j >= 1:
            values(j - 1, *pr.pop(j - 1))
    values(nj - 1, *pr.pop(nj - 1))


def _attention(q, k, v, sink, nb, seq, ctx_len):
    t = q.shape[0]
    qb = ATTN_BLOCK
    nqb = seq // qb
    ncb = ctx_len // qb
    xq = nb * nqb
    kw = k.shape[1]
    qrow = lambda b, n: jnp.where(n < nqb, b * nqb + n, xq + b * ncb + (n - nqb))
    kprev = lambda b, n, s: (b * nqb + jnp.clip(n - 1, 0, nqb - 1), 0)
    kcur = lambda b, n, s: (b * nqb + jnp.minimum(n, nqb - 1), 0)
    knext = lambda b, n, s: (b * nqb + jnp.minimum(n + 1, nqb - 1), 0)
    kctx = lambda b, n, s: (nb * seq // ctx_len + b, 0)
    kspec = lambda f: pl.BlockSpec((qb, kw), f)
    xspec = pl.BlockSpec((ctx_len, kw), kctx)
    return pl.pallas_call(
        functools.partial(_attn_kernel, nqb),
        grid_spec=pltpu.PrefetchScalarGridSpec(
            num_scalar_prefetch=1,
            grid=(nb, nqb + ncb),
            in_specs=[pl.BlockSpec((qb, D_MODEL), lambda b, n, s: (qrow(b, n), 0)),
                      kspec(kprev), kspec(kcur), kspec(knext), xspec,
                      kspec(kprev), kspec(kcur), kspec(knext), xspec],
            out_specs=pl.BlockSpec((qb, D_MODEL), lambda b, n, s: (qrow(b, n), 0)),
        ),
        out_shape=jax.ShapeDtypeStruct((t, D_MODEL), BF16),
        compiler_params=_cparams(("arbitrary", "arbitrary")),
        name="window_attn",
    )(sink, q, k, k, k, k, v, v, v, v)


def _post_kernel(n_xt, m_ref, w_ref, x_ref, c_ref, mod_ref, g_ref, rw_ref, rb_ref,
                 x1_ref, h2_ref, pos_ref, gate_ref, cnt_ref):
    latent = pl.program_id(0) * RT < n_xt
    md = mod_ref[0]
    ne = N_EXPERTS
    per = ne // N_GROUPS
    tw = RT * TM
    ys = [_dot(m_ref[s * TM:(s + 1) * TM, :], w_ref[...]) for s in range(RT)]
    logits = []
    for s in range(RT):
        rows = slice(s * TM, (s + 1) * TM)
        x = jnp.where(latent, x_ref[rows, :], c_ref[rows, :])
        x1 = x + md[2:3] * ys[s]
        x1_ref[rows, :] = x1
        h2 = _norm_mod(x1, g_ref[...], md[3:4], md[4:5])
        h2b = h2.astype(BF16)
        h2_ref[rows, :] = h2b
        h2lo = (h2 - h2b.astype(F32)).astype(BF16)
        lg = _dot(h2b, rw_ref[...]) + _dot(h2lo, rw_ref[...])
        lg = lg + pltpu.roll(lg, ne, 1)
        logits.append(lg.T[0:ne])
    logit = jnp.concatenate(logits, axis=1)
    sc = jax.nn.sigmoid(logit)
    sel = sc + rb_ref[...]
    sub8 = lax.broadcasted_iota(I32, (per, tw), 0)
    gs = jnp.zeros((N_GROUPS, tw), F32)
    gi = lax.broadcasted_iota(I32, (N_GROUPS, tw), 0)
    for g in range(N_GROUPS):
        blk = sel[per * g:per * (g + 1)]
        m1 = jnp.max(blk, axis=0, keepdims=True)
        i1 = jnp.min(jnp.where(blk == m1, sub8, per), axis=0, keepdims=True)
        m2 = jnp.max(jnp.where(sub8 == i1, -jnp.inf, blk), axis=0, keepdims=True)
        gs = jnp.where(gi == g, m1 + m2, gs)
    grank = jnp.zeros((N_GROUPS, tw), F32)
    for g in range(N_GROUPS):
        v = gs[g:g + 1]
        grank = grank + jnp.where(gi > g, jnp.where(v >= gs, 1.0, 0.0), jnp.where(v > gs, 1.0, 0.0))
    ei = lax.broadcasted_iota(I32, (ne, tw), 0)
    gsel = jnp.zeros((ne, tw), F32)
    for g in range(N_GROUPS):
        gsel = jnp.where(ei // per == g, grank[g:g + 1], gsel)
    selm = jnp.where(gsel < TOPK_GROUPS, sel, -jnp.inf)
    eif = ei.astype(F32)
    s8 = jnp.zeros((ne, tw), F32)
    for _ in range(TOP_K):
        best = jnp.max(selm, axis=0, keepdims=True)
        first = jnp.min(jnp.where(selm == best, eif, float(ne)), axis=0, keepdims=True)
        hit = eif == first
        s8 = jnp.where(hit, 1.0, s8)
        selm = jnp.where(hit, -jnp.inf, selm)
    ws = s8 * sc
    gate = ws / jnp.sum(ws, axis=0, keepdims=True) * ROUTED_SCALE

    s8b = s8.astype(BF16)
    er = lax.broadcasted_iota(I32, (ne, ne), 0)
    ec = lax.broadcasted_iota(I32, (ne, ne), 1)
    lower = jnp.where(ec < er, 1.0, 0.0).astype(BF16)
    tr = lax.broadcasted_iota(I32, (TM, TM), 0)
    tc = lax.broadcasted_iota(I32, (TM, TM), 1)
    upper = jnp.where(tr < tc, 1.0, 0.0).astype(BF16)
    slot = _dot(lower, s8b)
    prows = []
    for s in range(RT):
        cols = slice(s * TM, (s + 1) * TM)
        rank = _dot(s8b[:, cols], upper)
        cnt = jnp.sum(s8[:, cols], axis=1, keepdims=True)
        pc = jnp.floor((cnt + (CHUNK - 1)) * (1.0 / CHUNK))
        pcb = jnp.broadcast_to(pc, (ne, 128))
        cnt_ref[s] = pcb
        loc = _dot(lower, pcb.astype(BF16))[:, 0:1] * CHUNK
        prows.append(loc + rank)
    prow = jnp.concatenate(prows, axis=1)
    k8 = lax.broadcasted_iota(I32, (TOP_K, tw), 0)
    pos_t = jnp.zeros((TOP_K, tw), F32)
    gate_t = jnp.zeros((TOP_K, tw), F32)
    for k in range(TOP_K):
        mk = jnp.where(slot == k, s8, 0.0)
        pos_t = jnp.where(k8 == k, jnp.sum(mk * prow, axis=0, keepdims=True), pos_t)
        gate_t = jnp.where(k8 == k, jnp.sum(mk * gate, axis=0, keepdims=True), gate_t)
    pos_ref[...] = pos_t.astype(I32)
    gate_ref[...] = gate_t


def _post(m, w, x2d, c2d, mod, g, rw_t, rb, nt, n_xt, tpb, nb):
    assert nt % RT == 0 and n_xt % RT == 0 and tpb % RT == 0
    t = nt * TM
    kd = m.shape[1]
    rm = RT * TM
    nxs = n_xt // RT
    row = lambda i: jnp.where(i < nxs, i * RT // tpb, nb)
    return pl.pallas_call(
        functools.partial(_post_kernel, n_xt),
        grid=(nt // RT,),
        in_specs=[pl.BlockSpec((rm, kd), lambda i: (i, 0)),
                  pl.BlockSpec(w.shape, lambda i: (0, 0)),
                  pl.BlockSpec((rm, D_MODEL), lambda i: (jnp.minimum(i, nxs - 1), 0)),
                  pl.BlockSpec((rm, D_MODEL), lambda i: (jnp.maximum(i - nxs, 0), 0)),
                  pl.BlockSpec((1, 6, D_MODEL), lambda i: (row(i), 0, 0)),
                  pl.BlockSpec((1, D_MODEL), lambda i: (0, 0)),
                  pl.BlockSpec((D_MODEL, 2 * N_EXPERTS), lambda i: (0, 0)),
                  pl.BlockSpec((N_EXPERTS, 1), lambda i: (0, 0))],
        out_specs=[pl.BlockSpec((rm, D_MODEL), lambda i: (i, 0)),
                   pl.BlockSpec((rm, D_MODEL), lambda i: (i, 0)),
                   pl.BlockSpec((TOP_K, rm), lambda i: (0, i)),
                   pl.BlockSpec((TOP_K, rm), lambda i: (0, i)),
                   pl.BlockSpec((RT, N_EXPERTS, 128), lambda i: (i, 0, 0))],
        out_shape=[jax.ShapeDtypeStruct((t, D_MODEL), F32),
                   jax.ShapeDtypeStruct((t, D_MODEL), BF16),
                   jax.ShapeDtypeStruct((TOP_K, t), I32),
                   jax.ShapeDtypeStruct((TOP_K, t), F32),
                   jax.ShapeDtypeStruct((nt, N_EXPERTS, 128), F32)],
        compiler_params=_cparams(("arbitrary",)),
        name="mix_out_route",
    )(m, w, x2d, c2d, mod, g, rw_t, rb)


def _moe_tables(pc, nblk_cap):
    nt = pc.shape[0]
    seg = jnp.sum(pc, axis=0)
    segb = (seg + CPB - 1) // CPB
    seg_end_b = jnp.cumsum(segb)
    seg_start = (seg_end_b - segb) * CPB
    goff = seg_start[None, :] + jnp.cumsum(pc, axis=0) - pc
    loc_end = jnp.cumsum(pc, axis=1)
    loc = loc_end - pc
    nch = loc_end[:, -1]
    nrb = (nch + CPH - 1) // CPH
    j = jnp.arange(NCH_CAP, dtype=I32)
    inrun = (j[None, :, None] >= loc[:, None, :]) & (j[None, :, None] < loc_end[:, None, :])
    shift = jnp.sum(jnp.where(inrun, (goff - loc)[:, None, :], 0), axis=-1)
    live = j[None, :] < nch[:, None]
    trash = nblk_cap * CPB + jnp.arange(nt, dtype=I32)[:, None] * SPARE + j[None, :] - MIN_RB * CPH
    dst = jnp.where(live, shift + j[None, :], trash).astype(I32)
    src = jnp.where(live, shift + j[None, :], 0).astype(I32)
    subs = (seg + CPH - 1) // CPH
    tl = jnp.arange(CPH - 1, dtype=I32)
    tail_len = subs * CPH - seg
    tails = jnp.where(tl[None, :] < tail_len[:, None], (seg_start + seg)[:, None] + tl[None, :], -1)
    nused = seg_end_b[-1]
    blk = jnp.arange(nblk_cap, dtype=I32)
    bexp = jnp.sum((blk[:, None] >= seg_end_b[None, :]).astype(I32), axis=-1)
    last = jnp.sum((nused - 1 >= seg_end_b).astype(I32))
    bexp = jnp.where(blk < nused, bexp, last).astype(I32)
    onehot = (bexp[:, None] == jnp.arange(N_EXPERTS, dtype=I32)[None, :]).astype(I32)
    first_blk = seg_end_b - segb
    bsub = jnp.sum(onehot * (subs - NSUB * (blk[:, None] - first_blk[None, :])), axis=-1)
    bsub = jnp.where(blk < nused, jnp.clip(bsub, 1, NSUB), 0).astype(I32)
    eid = jnp.arange(N_EXPERTS, dtype=I32)
    owns = segb > 0
    ordinal = jnp.cumsum(owns.astype(I32)) - 1
    later = jnp.where(owns[None, :] & (eid[None, :] > eid[:, None]), eid[None, :], N_EXPERTS)
    nxt_e = jnp.min(later, axis=1)
    nxt_e = jnp.where(nxt_e < N_EXPERTS, nxt_e, -1)
    bpar = jnp.sum(onehot * (ordinal % 2)[None, :], axis=-1).astype(I32)
    bnext = jnp.sum(onehot * nxt_e[None, :], axis=-1).astype(I32)
    return dict(nrb=nrb.astype(I32), dst=dst.reshape(-1), src=src.reshape(-1),
                tails=tails.reshape(-1).astype(I32),
                nused=nused.reshape(1).astype(I32), bexp=bexp, bsub=bsub, bpar=bpar, bnext=bnext)


def _rows_copy(src, dst, sem):
    return pltpu.make_async_copy(src, dst, sem)


def _row_onehot(riota, pos, base, value):
    p = jnp.zeros((TM, TM), BF16)
    for k in range(TOP_K):
        d = jnp.clip(pos[k:k + 1] - base, -1, TM).astype(F32).astype(BF16)
        v = 1.0 if value is None else value[k:k + 1]
        p = jnp.where(riota == d, v, p)
    return p


def _dispatch_kernel(nt, nrb_ref, dst_ref, tails_ref, h2_ref, pos_ref, xs_hbm, buf, sem):
    i = pl.program_id(0)
    slot = i % 2

    def drain(tile, sl):
        def body(rb, carry):
            _rows_copy(buf.at[sl, pl.ds(0, TM)], xs_hbm.at[pl.ds(0, TM)], sem.at[sl]).wait()
            return carry
        lax.fori_loop(0, jnp.maximum(nrb_ref[tile], EAGER_RB), body, 0)

    @pl.when((i >= 2) & (i < nt))
    def _():
        drain(i - 2, slot)

    @pl.when(i < nt)
    def _():
        h2 = h2_ref[...]
        pos = pos_ref[...]
        riota = lax.broadcasted_iota(I32, (TM, TM), 0).astype(F32).astype(BF16)
        nrb = nrb_ref[i]

        def rows_of(b):
            return _dot(_row_onehot(riota, pos, b * TM, None), h2).astype(BF16)

        def send(b, xb):
            buf[slot, b * TM:(b + 1) * TM, :] = xb
            for c in range(CPH):
                d = dst_ref[i * NCH_CAP + b * CPH + c]
                _rows_copy(buf.at[slot, pl.ds(b * TM + c * CHUNK, CHUNK)],
                           xs_hbm.at[pl.ds(pl.multiple_of(d * CHUNK, CHUNK), CHUNK)],
                           sem.at[slot]).start()

        prev = rows_of(0)
        for b in range(1, EAGER_RB):
            cur = rows_of(b)
            send(b - 1, prev)
            prev = cur
        send(EAGER_RB - 1, prev)
        for b in range(EAGER_RB, NRB_CAP):
            pl.when(b < nrb)(lambda b=b: send(b, rows_of(b)))

    @pl.when(i == nt)
    def _():
        drain(nt - 2, nt % 2)
        drain(nt - 1, (nt - 1) % 2)
        buf[0, pl.ds(0, CHUNK), :] = jnp.zeros((CHUNK, D_MODEL), BF16)
        ntail = N_EXPERTS * (CPH - 1)

        def tail(j, carry):
            d = tails_ref[j]

            @pl.when(d >= 0)
            def _():
                _rows_copy(buf.at[0, pl.ds(0, CHUNK)],
                           xs_hbm.at[pl.ds(pl.multiple_of(d * CHUNK, CHUNK), CHUNK)], sem.at[0]).start()
            return carry

        lax.fori_loop(0, ntail, tail, 0)

        def tail_wait(j, carry):
            @pl.when(tails_ref[j] >= 0)
            def _():
                _rows_copy(buf.at[0, pl.ds(0, CHUNK)], xs_hbm.at[pl.ds(0, CHUNK)], sem.at[0]).wait()
            return carry

        lax.fori_loop(0, ntail, tail_wait, 0)


def _dispatch(tb, h2, pos_t, nt, nblk_cap):
    last = nt - 1
    return pl.pallas_call(
        functools.partial(_dispatch_kernel, nt),
        grid_spec=pltpu.PrefetchScalarGridSpec(
            num_scalar_prefetch=3,
            grid=(nt + 1,),
            in_specs=[pl.BlockSpec((TM, D_MODEL), lambda i, *_: (jnp.minimum(i, last), 0)),
                      pl.BlockSpec((TOP_K, TM), lambda i, *_: (0, jnp.minimum(i, last)))],
            out_specs=pl.BlockSpec(memory_space=pl.ANY),
            scratch_shapes=[pltpu.VMEM((2, RCAP, D_MODEL), BF16), pltpu.SemaphoreType.DMA((2,))],
        ),
        out_shape=jax.ShapeDtypeStruct((nblk_cap * BM + nt * SPARE * CHUNK, D_MODEL), BF16),
        compiler_params=_cparams(("arbitrary",)),
        name="moe_dispatch",
    )(tb["nrb"], tb["dst"], tb["tails"], h2, pos_t)


RING = 3
CSLOT = 3


def _expert_kernel(layer, bexp_ref, bsub_ref, bpar_ref, bnext_ref, nused_ref,
                   xs_hbm, wg_hbm, wu_hbm, wd_hbm, ys_hbm,
                   xb, yb, stg_g, stg_u, stg_d, wgu, wdn, xsem, ysem, wsem):
    i = pl.program_id(0)
    nused = nused_ref[0]

    def x_copy(blk, sl, n):
        return _rows_copy(xs_hbm.at[pl.ds(pl.multiple_of(blk * BM, BM), n * HB)],
                          xb.at[sl, pl.ds(0, n * HB)], xsem.at[sl])

    def y_copy(blk, sl, n):
        return _rows_copy(yb.at[sl, pl.ds(0, n * HB)],
                          ys_hbm.at[pl.ds(pl.multiple_of(blk * BM, BM), n * HB)], ysem.at[sl])

    def sized(blk, fn):
        live = bsub_ref[blk]
        for n in range(1, NSUB + 1):
            pl.when(live == n)(functools.partial(fn, n))

    def w_copies(ex, sl):
        return (_rows_copy(wg_hbm.at[layer, ex], stg_g.at[sl], wsem.at[sl]),
                _rows_copy(wu_hbm.at[layer, ex], stg_u.at[sl], wsem.at[sl]),
                _rows_copy(wd_hbm.at[layer, ex], stg_d.at[sl], wsem.at[sl]))

    @pl.when(i == 0)
    def _():
        sized(0, lambda n: x_copy(0, 0, n).start())

        @pl.when(nused > 1)
        def _():
            sized(1, lambda n: x_copy(1, 1, n).start())
        for c in w_copies(bexp_ref[0], bpar_ref[0]):
            c.start()

    @pl.when(i < nused)
    def _():
        sl = i % RING

        @pl.when(i + 2 < nused)
        def _():
            sized(i + 2, lambda n: x_copy(i + 2, (i + 2) % RING, n).start())

        e = bexp_ref[i]
        par = bpar_ref[i]

        @pl.when((i == 0) | (e != bexp_ref[jnp.maximum(i - 1, 0)]))
        def _():
            for c in w_copies(e, par):
                c.wait()
            wgu[:, :D_EXPERT] = stg_g[par].astype(BF16)
            wgu[:, D_EXPERT:] = stg_u[par].astype(BF16)
            wdn[...] = stg_d[par].astype(BF16)
            nxt = bnext_ref[i]

            @pl.when(nxt >= 0)
            def _():
                for c in w_copies(nxt, 1 - par):
                    c.start()

        @pl.when(i >= RING)
        def _():
            sized(i - RING, lambda n: y_copy(i - RING, sl, n).wait())

        def step(n):
            x_copy(i, sl, n).wait()
            def up(c):
                return _dot(xb[sl, pl.ds(c * HB, HB), :], wgu[...])

            def act(gu):
                return (_silu(gu[:, :D_EXPERT]) * gu[:, D_EXPERT:]).astype(BF16)

            def down(c, h):
                yb[sl, pl.ds(c * HB, HB), :] = _dot(h, wdn[...]).astype(BF16)

            gus = {0: up(0)}
            hs = {}
            for c in range(n):
                if c + 1 < n:
                    gus[c + 1] = up(c + 1)
                hs[c] = act(gus.pop(c))
                if c >= 1:
                    down(c - 1, hs.pop(c - 1))
            down(n - 1, hs.pop(n - 1))
            y_copy(i, sl, n).start()

        sized(i, step)

        @pl.when(i == nused - 1)
        def _():
            for back in range(RING):
                @pl.when(i - back >= 0)
                def _():
                    sized(i - back, lambda n, back=back: y_copy(i - back, (i - back) % RING, n).wait())


def _experts(tb, xs, w_gate, w_up, w_down, layer, nblk_cap):
    hbm = pl.BlockSpec(memory_space=pl.ANY)
    return pl.pallas_call(
        functools.partial(_expert_kernel, layer),
        grid_spec=pltpu.PrefetchScalarGridSpec(
            num_scalar_prefetch=5,
            grid=(nblk_cap,),
            in_specs=[hbm, hbm, hbm, hbm],
            out_specs=hbm,
            scratch_shapes=[pltpu.VMEM((RING, BM, D_MODEL), BF16),
                            pltpu.VMEM((RING, BM, D_MODEL), BF16),
                            pltpu.VMEM((2, D_MODEL, D_EXPERT), F32),
                            pltpu.VMEM((2, D_MODEL, D_EXPERT), F32),
                            pltpu.VMEM((2, D_EXPERT, D_MODEL), F32),
                            pltpu.VMEM((D_MODEL, 2 * D_EXPERT), BF16),
                            pltpu.VMEM((D_EXPERT, D_MODEL), BF16),
                            pltpu.SemaphoreType.DMA((RING,)),
                            pltpu.SemaphoreType.DMA((RING,)),
                            pltpu.SemaphoreType.DMA((2,))],
        ),
        out_shape=jax.ShapeDtypeStruct((nblk_cap * BM, D_MODEL), BF16),
        compiler_params=_cparams(("arbitrary",)),
        name="moe_experts",
    )(tb["bexp"], tb["bsub"], tb["bpar"], tb["bnext"], tb["nused"], xs, w_gate, w_up, w_down)


def _combine_kernel(final, nt, nrb_ref, src_ref, ys_hbm, pos_ref, gate_ref, h2_ref, x1_ref,
                    mod_ref, sgu_ref, sdn_ref, fg_ref, o_ref, buf, acc, sem):
    i = pl.program_id(0)
    slot = i % CSLOT
    nxt = jnp.minimum(i + CSLOT - 1, nt - 1)
    nrb = nrb_ref[i]
    nrb_next = nrb_ref[nxt]
    nslot = (i + CSLOT - 1) % CSLOT

    def fetch(tile, sl, b):
        for c in range(CPH):
            s = src_ref[tile * NCH_CAP + b * CPH + c]
            _rows_copy(ys_hbm.at[pl.ds(pl.multiple_of(s * CHUNK, CHUNK), CHUNK)],
                       buf.at[sl, pl.ds(b * TM + c * CHUNK, CHUNK)], sem.at[sl]).start()

    def wait_blocks(sl, n):
        def body(rb, carry):
            _rows_copy(ys_hbm.at[pl.ds(0, TM)], buf.at[sl, pl.ds(0, TM)], sem.at[sl]).wait()
            return carry
        lax.fori_loop(0, jnp.maximum(n, EAGER_RB), body, 0)

    @pl.when(i == 0)
    def _():
        for tile in range(min(CSLOT - 1, nt)):
            for b in range(EAGER_RB):
                fetch(tile, tile, b)
            for b in range(EAGER_RB, NRB_CAP):
                pl.when(b < nrb_ref[tile])(functools.partial(fetch, tile, tile, b))

    wait_blocks(slot, nrb)

    pos = pos_ref[...]
    gate = gate_ref[...].astype(BF16)
    riota = lax.broadcasted_iota(I32, (TM, TM), 0).astype(F32).astype(BF16)
    tlhs = (((0,), (0,)), ((), ()))

    gu = _dot(h2_ref[...], sgu_ref[...])
    hs = _silu(gu[:, :D_EXPERT]) * gu[:, D_EXPERT:]
    tot = _dot(hs.astype(BF16), sdn_ref[...])
    onehots = [_row_onehot(riota, pos, b * TM, gate) for b in range(EAGER_RB)]
    for b in range(EAGER_RB):
        fetch(nxt, nslot, b)
        tot = tot + lax.dot_general(onehots[b], buf[slot, b * TM:(b + 1) * TM, :], tlhs,
                                    preferred_element_type=F32)
    acc[...] = tot
    for b in range(EAGER_RB, NRB_CAP):
        pl.when(b < nrb_next)(functools.partial(fetch, nxt, nslot, b))

        @pl.when(b < nrb)
        def _():
            pb = _row_onehot(riota, pos, b * TM, gate)
            acc[...] += lax.dot_general(pb, buf[slot, b * TM:(b + 1) * TM, :], tlhs,
                                        preferred_element_type=F32)

    @pl.when(i == nt - 1)
    def _():
        for back in range(min(CSLOT - 1, nt)):
            wait_blocks((i - back + CSLOT - 1) % CSLOT, nrb_next)

    x2 = x1_ref[...] + mod_ref[0][5:6] * acc[...]
    if final:
        x2 = x2 * lax.rsqrt(jnp.mean(x2 * x2, axis=-1, keepdims=True) + RMS_EPS) * fg_ref[...]
    o_ref[...] = x2


def _combine(tb, ys, pos, gate, h2, x1, mod, sgu, sdn, fg, nt, n_xt, tpb, nb, final):
    t = nt * TM
    row = lambda i, *_: (jnp.where(i < n_xt, i // tpb, nb), 0, 0)
    tile = lambda i, *_: (i, 0)
    slots = lambda i, *_: (0, i)
    const = lambda i, *_: (0, 0)
    return pl.pallas_call(
        functools.partial(_combine_kernel, final, nt),
        grid_spec=pltpu.PrefetchScalarGridSpec(
            num_scalar_prefetch=2,
            grid=(nt,),
            in_specs=[pl.BlockSpec(memory_space=pl.ANY),
                      pl.BlockSpec((TOP_K, TM), slots),
                      pl.BlockSpec((TOP_K, TM), slots),
                      pl.BlockSpec((TM, D_MODEL), tile),
                      pl.BlockSpec((TM, D_MODEL), tile),
                      pl.BlockSpec((1, 6, D_MODEL), row),
                      pl.BlockSpec(sgu.shape, const),
                      pl.BlockSpec(sdn.shape, const),
                      pl.BlockSpec((1, D_MODEL), const)],
            out_specs=pl.BlockSpec((TM, D_MODEL), tile),
            scratch_shapes=[pltpu.VMEM((CSLOT, RCAP, D_MODEL), BF16),
                            pltpu.VMEM((TM, D_MODEL), F32),
                            pltpu.SemaphoreType.DMA((CSLOT,))],
        ),
        out_shape=jax.ShapeDtypeStruct((t, D_MODEL), F32),
        compiler_params=_cparams(("arbitrary",)),
        name="moe_combine",
    )(tb["nrb"], tb["src"], ys, pos, gate, h2, x1, mod, sgu, sdn, fg)


def _moe(h2, pos_t, gate_t, cnt, x1, mod, w_gate, w_up, w_down, layer, sgu, sdn, fg,
         nt, n_xt, tpb, nb, final):
    nblk_cap = (TOP_K * nt * TM + (CHUNK - 1) * N_EXPERTS * nt) // BM + N_EXPERTS
    tb = _moe_tables(cnt[:, :, 0].astype(I32), nblk_cap)
    xs = _dispatch(tb, h2, pos_t, nt, nblk_cap)
    ys = _experts(tb, xs, w_gate, w_up, w_down, layer, nblk_cap)
    return _combine(tb, ys, pos_t, gate_t, h2, x1, mod, sgu, sdn, fg, nt, n_xt, tpb, nb, final)


def _lru_in_kernel(x_ref, mod_ref, g_ref, w_ref, gate_ref, u_ref):
    m = mod_ref[0]
    h = _norm_mod(x_ref[...], g_ref[...], m[0:1], m[1:2])
    p = _dot(h.astype(BF16), w_ref[...])
    z = p[:, :D_RNN]
    cdf = 0.5 * (1.0 + jnp.tanh(math.sqrt(2.0 / math.pi) * (z + 0.044715 * (z * z * z))))
    gate_ref[...] = (z * cdf).astype(BF16)
    u_ref[...] = p[:, D_RNN:]


def _lru_in(xall, mod, g, w, n_xt, tpb, nb):
    t = xall.shape[0]
    nt = t // TM
    row = lambda i: jnp.where(i < n_xt, i // tpb, nb)
    return pl.pallas_call(
        _lru_in_kernel,
        grid=(nt,),
        in_specs=[pl.BlockSpec((TM, D_MODEL), lambda i: (i, 0)),
                  pl.BlockSpec((1, 6, D_MODEL), lambda i: (row(i), 0, 0)),
                  pl.BlockSpec((1, D_MODEL), lambda i: (0, 0)),
                  pl.BlockSpec(w.shape, lambda i: (0, 0))],
        out_specs=[pl.BlockSpec((TM, D_RNN), lambda i: (i, 0)),
                   pl.BlockSpec((TM, D_RNN), lambda i: (i, 0))],
        out_shape=[jax.ShapeDtypeStruct((t, D_RNN), BF16),
                   jax.ShapeDtypeStruct((t, D_RNN), F32)],
        compiler_params=_cparams(("arbitrary",)),
        name="lru_in",
    )(xall, mod, g, w)


def _gate_windows():
    wins = []
    for c0 in range(0, D_RNN, 256):
        c1 = min(c0 + 256, D_RNN)
        k0 = (c0 // LRU_BLOCK_W) * LRU_BLOCK_W
        k1 = ((c1 - 1) // LRU_BLOCK_W + 1) * LRU_BLOCK_W
        wins.append((c0, c1, (k0 // 128) * 128, min(-(-k1 // 128) * 128, D_RNN)))
    return wins


def _block_diag_dot(ub, w_ref):
    return jnp.concatenate([_dot(ub[:, k0:k1], w_ref[k0:k1, c0:c1]) for c0, c1, k0, k1 in _gate_windows()],
                           axis=1)


def _lru_sweep_kernel(rev, mix, conv, ns, u_ref, up_ref, un_ref, cw_ref, cb_ref, wa_ref, wx_ref,
                      ba_ref, bx_ref, lam_ref, h0_ref, *rest):
    rest = list(rest)
    hf_ref, gate_ref = (rest.pop(0), rest.pop(0)) if mix else (None, None)
    o_ref = rest.pop(0)
    uc_ref = rest.pop(0) if conv == "emit" else None
    a_s, b_s, carry = rest
    s = pl.program_id(1)
    ss = (ns - 1 - s) if rev else s

    @pl.when(s == 0)
    def _():
        carry[...] = jnp.broadcast_to(h0_ref[0], (8, D_RNN))

    if conv == "reuse":
        u = u_ref[...]
    else:
        prev = jnp.where(ss > 0, up_ref[...], 0.0)
        nxt = jnp.where(ss < ns - 1, un_ref[...], 0.0)
        ext = jnp.concatenate([prev, u_ref[...], nxt], axis=0)
        n_ext = TM + 16
        cw = cw_ref[...]
        u = cb_ref[...]
        for j in range(CONV_W):
            shift = (CONV_LEFT - j) % n_ext
            tap = ext if shift == 0 else pltpu.roll(ext, shift, 0)
            u = u + cw[j:j + 1] * tap[8:8 + TM]
        if uc_ref is not None:
            uc_ref[...] = u

    ub = u.astype(BF16)
    ta = jnp.tanh(_block_diag_dot(ub, wa_ref) + ba_ref[...])
    tx = jnp.tanh(_block_diag_dot(ub, wx_ref) + bx_ref[...])
    nl = -lam_ref[...]
    c0 = (0.5 * LRU_C) * (jnp.maximum(nl, 0.0) + jnp.log1p(jnp.exp(-jnp.abs(nl))))
    y = c0 + c0 * ta
    a = jnp.exp(-y)
    a_s[...] = a
    hu = 0.5 * u
    b_s[...] = jnp.sqrt(jnp.tanh(y) * (1.0 + a * a)) * (hu + hu * tx)

    sub = lax.broadcasted_iota(I32, (8, D_RNN), 0)
    h = carry[...]
    ng = TM // 8
    for gi in range(ng):
        r0 = 8 * ((ng - 1 - gi) if rev else gi)
        a = a_s[r0:r0 + 8, :]
        b = b_s[r0:r0 + 8, :]
        for sh in (1, 2, 4):
            amt = (8 - sh) if rev else sh
            keep = (sub < 8 - sh) if rev else (sub >= sh)
            a_sh = pltpu.roll(a, amt, 0)
            b_sh = pltpu.roll(b, amt, 0)
            b = jnp.where(keep, a * b_sh + b, b)
            a = jnp.where(keep, a * a_sh, a)
        hg = a * h + b
        edge = hg[0:1] if rev else hg[7:8]
        h = jnp.broadcast_to(edge, (8, D_RNN))
        if mix:
            o_ref[r0:r0 + 8, :] = ((hf_ref[r0:r0 + 8, :] + hg)
                                   * gate_ref[r0:r0 + 8, :].astype(F32)).astype(BF16)
        else:
            o_ref[r0:r0 + 8, :] = hg
    carry[...] = h


def _lru_sweep(u_all, row0, nb, seq, cw, cb, wa, wx, ba, bx, lam, h0, rev, hf=None, gate=None,
               conv="compute"):
    ns = seq // TM
    blk0 = row0 // TM
    mix = hf is not None
    sidx = (lambda s: ns - 1 - s) if rev else (lambda s: s)
    cur = lambda b, s: (blk0 + b * ns + sidx(s), 0)
    out = lambda b, s: (b * ns + sidx(s), 0)
    h8 = TM // 8
    nblk8 = u_all.shape[0] // 8
    prv = lambda b, s: (jnp.maximum((blk0 + b * ns + sidx(s)) * h8 - 1, 0), 0)
    nxt = lambda b, s: (jnp.minimum((blk0 + b * ns + sidx(s) + 1) * h8, nblk8 - 1), 0)
    vec = lambda b, s: (0, 0)
    in_specs = [pl.BlockSpec((TM, D_RNN), cur),
                pl.BlockSpec((8, D_RNN), prv),
                pl.BlockSpec((8, D_RNN), nxt),
                pl.BlockSpec((CONV_W, D_RNN), vec),
                pl.BlockSpec((1, D_RNN), vec),
                pl.BlockSpec((D_RNN, D_RNN), vec),
                pl.BlockSpec((D_RNN, D_RNN), vec),
                pl.BlockSpec((1, D_RNN), vec),
                pl.BlockSpec((1, D_RNN), vec),
                pl.BlockSpec((1, D_RNN), vec),
                pl.BlockSpec((1, 1, D_RNN), lambda b, s: (b, 0, 0))]
    args = [u_all, u_all, u_all, cw, cb, wa, wx, ba, bx, lam, h0]
    if mix:
        in_specs += [pl.BlockSpec((TM, D_RNN), out), pl.BlockSpec((TM, D_RNN), cur)]
        args += [hf, gate]
    out_specs = [pl.BlockSpec((TM, D_RNN), out)]
    out_shape = [jax.ShapeDtypeStruct((nb * seq, D_RNN), BF16 if mix else F32)]
    if conv == "emit":
        out_specs.append(pl.BlockSpec((TM, D_RNN), out))
        out_shape.append(jax.ShapeDtypeStruct((nb * seq, D_RNN), F32))
    return pl.pallas_call(
        functools.partial(_lru_sweep_kernel, rev, mix, conv, ns),
        grid=(nb, ns),
        in_specs=in_specs,
        out_specs=out_specs,
        out_shape=out_shape,
        scratch_shapes=[pltpu.VMEM((TM, D_RNN), F32), pltpu.VMEM((TM, D_RNN), F32),
                        pltpu.VMEM((8, D_RNN), F32)],
        compiler_params=_cparams(("arbitrary", "arbitrary")),
        name="lru_sweep_rev" if rev else "lru_sweep_fwd",
    )(*args)


def _block_diag(w):
    rows = jnp.tile(w.reshape(D_RNN, LRU_BLOCK_W), (1, LRU_BLOCKS))
    blk = jnp.arange(D_RNN, dtype=I32) // LRU_BLOCK_W
    return jnp.where(blk[:, None] == blk[None, :], rows, 0.0).astype(BF16)


def kernel(x, c, ctx, c_ctx, ada_w, ada_b, norm1_g, norm2_g, attn_w_qkv, attn_w_o, attn_sink,
           lru_w_in, lru_conv_w, lru_conv_b, lru_wa, lru_ba, lru_wx, lru_bx, lru_lam, lru_w_out,
           moe_router_w, moe_router_bias, moe_w_gate, moe_w_up, moe_w_down,
           shared_w_gate, shared_w_up, shared_w_down, final_g):
    nb, seq, d = x.shape
    ctx_len = ctx.shape[1]
    assert d == D_MODEL and seq % TM == 0 and ctx_len == TM and nb < 8
    tx = nb * seq
    tall = tx + nb * ctx_len
    tpb = seq // TM
    n_xt = tx // TM
    nt_all = tall // TM

    cvec = jnp.zeros((8, d), F32).at[:nb].set(c).at[nb].set(c_ctx)
    mod = _adaln(cvec, ada_w, ada_b).reshape(DEPTH, 8, 6, d)
    x2d = x.reshape(tx, d)
    c2d = ctx.reshape(nb * ctx_len, d)
    row = lambda v: v.reshape(1, -1)
    shared = lambda i: (jnp.concatenate([shared_w_gate[i], shared_w_up[i]], axis=1).astype(BF16),
                        shared_w_down[i].astype(BF16))

    def router(i):
        hi = moe_router_w[i].astype(BF16)
        lo = (moe_router_w[i] - hi.astype(F32)).astype(BF16)
        return jnp.concatenate([hi, lo], axis=1)

    qd = N_HEADS * HEAD_DIM
    kd = N_KV_HEADS * HEAD_DIM
    wq = attn_w_qkv[0][:, :qd]
    wk = attn_w_qkv[0][:, qd:qd + kd].reshape(d, N_KV_HEADS, 1, HEAD_DIM)
    wv = attn_w_qkv[0][:, qd + kd:].reshape(d, N_KV_HEADS, 1, HEAD_DIM)
    dup = lambda w: jnp.broadcast_to(w, (d, N_KV_HEADS, 2, HEAD_DIM)).reshape(d, 2 * kd)
    wqkv = jnp.concatenate([wq, dup(wk), dup(wv)], axis=1).astype(BF16)
    cos, sin = _rope_tables(seq)
    q, k, v = _qkv(x2d, c2d, mod[0], row(norm1_g[0]), wqkv, cos, sin, n_xt, tpb, nb)
    o = _attention(q, k, v, attn_sink[0], nb, seq, ctx_len)
    x1, h2, pos_t, gate_t, cnt = _post(
        o, attn_w_o[0].astype(BF16), x2d, c2d, mod[0], row(norm2_g[0]),
        router(0), moe_router_bias[0].reshape(-1, 1), nt_all, n_xt, tpb, nb)
    sgu, sdn = shared(0)
    xall = _moe(h2, pos_t, gate_t, cnt, x1, mod[0], moe_w_gate, moe_w_up, moe_w_down, 0,
                sgu, sdn, row(final_g), nt_all, n_xt, tpb, nb, False)

    gate, u_pre = _lru_in(xall, mod[1], row(norm1_g[1]), lru_w_in[0].astype(BF16), n_xt, tpb, nb)
    cw, cb = lru_conv_w[0], row(lru_conv_b[0])
    hdir = []
    for dr, rev in ((0, False), (1, True)):
        wa = _block_diag(0.5 * lru_wa[0, dr])
        wx = _block_diag(0.5 * lru_wx[0, dr])
        prm = (cw, cb, wa, wx, row(0.5 * lru_ba[0, dr]), row(0.5 * lru_bx[0, dr]), row(lru_lam[0, dr]))
        zero = jnp.zeros((nb, 1, D_RNN), F32)
        hc = _lru_sweep(u_pre, tx, nb, ctx_len, *prm, zero, rev)[0].reshape(nb, ctx_len, D_RNN)
        h0 = hc[:, 0:1] if rev else hc[:, ctx_len - 1:ctx_len]
        if not rev:
            hdir = _lru_sweep(u_pre, 0, nb, seq, *prm, h0, rev, conv="emit")
        else:
            mixed = _lru_sweep(hdir[1], 0, nb, seq, *prm, h0, rev, hf=hdir[0], gate=gate,
                               conv="reuse")[0]
    x1, h2, pos_t, gate_t, cnt = _post(
        mixed, lru_w_out[0].astype(BF16), xall, xall, mod[1], row(norm2_g[1]),
        router(1), moe_router_bias[1].reshape(-1, 1), n_xt, n_xt, tpb, nb)
    sgu, sdn = shared(1)
    out = _moe(h2, pos_t, gate_t, cnt, x1, mod[1], moe_w_gate, moe_w_up, moe_w_down, 1,
               sgu, sdn, row(final_g), n_xt, n_xt, tpb, nb, True)
    return out.reshape(nb, seq, d)
```

```python
import functools
import math

import jax
import jax.numpy as jnp
from jax import lax
from jax.experimental import pallas as pl
from jax.experimental.pallas import tpu as pltpu

D_MODEL = 1024
DEPTH = 2
GRID_W = 64
HEAD_DIM = 64
N_HEADS = 16
N_KV_HEADS = 4
GROUP = N_HEADS // N_KV_HEADS
WINDOW = 128
ATTN_BLOCK = 128
ROPE_THETA = 10000.0
D_RNN = 1280
LRU_BLOCKS = 16
LRU_BLOCK_W = D_RNN // LRU_BLOCKS
CONV_W = 4
CONV_LEFT = 2
LRU_C = 8.0
N_EXPERTS = 64
TOP_K = 8
N_GROUPS = 8
TOPK_GROUPS = 4
D_EXPERT = 256
ROUTED_SCALE = 2.5
RMS_EPS = 1e-6

F32 = jnp.float32
BF16 = jnp.bfloat16
I32 = jnp.int32
HIGHEST = lax.Precision.HIGHEST
LOG2E = math.log2(math.e)

TM = 256
RT = 4
CHUNK = 16
HB = 256
NSUB = 6
BM = NSUB * HB
CPB = BM // CHUNK
CPH = HB // CHUNK
RCAP = ((TOP_K * TM + (CHUNK - 1) * N_EXPERTS + TM - 1) // TM) * TM
NCH_CAP = RCAP // CHUNK
NRB_CAP = RCAP // TM
MIN_RB = TOP_K
EAGER_RB = 10
SPARE = (NRB_CAP - MIN_RB) * CPH
VMEM_LIMIT = 56 * 1024 * 1024


def _cparams(sem):
    return pltpu.CompilerParams(dimension_semantics=sem, vmem_limit_bytes=VMEM_LIMIT)


def _silu(x):
    return x * jax.nn.sigmoid(x)


def _dot(a, b):
    return jnp.dot(a, b, preferred_element_type=F32)


def _dot_nt(a, b, precision=None):
    return lax.dot_general(a, b, (((1,), (1,)), ((), ())), precision=precision,
                           preferred_element_type=F32)


def _adaln_kernel(c_ref, w_ref, b_ref, o_ref):
    s = _silu(c_ref[...])
    o_ref[0] = jnp.dot(s, w_ref[0], precision=HIGHEST, preferred_element_type=F32) + b_ref[0]


def _adaln(cvec, ada_w, ada_b):
    nb = 1536
    d6 = 6 * D_MODEL
    return pl.pallas_call(
        _adaln_kernel,
        grid=(DEPTH, d6 // nb),
        in_specs=[pl.BlockSpec((8, D_MODEL), lambda l, j: (0, 0)),
                  pl.BlockSpec((1, D_MODEL, nb), lambda l, j: (l, 0, j)),
                  pl.BlockSpec((1, 1, nb), lambda l, j: (l, 0, j))],
        out_specs=pl.BlockSpec((1, 8, nb), lambda l, j: (l, 0, j)),
        out_shape=jax.ShapeDtypeStruct((DEPTH, 8, d6), F32),
        compiler_params=_cparams(("arbitrary", "arbitrary")),
        name="adaln",
    )(cvec, ada_w, ada_b.reshape(DEPTH, 1, d6))


def _norm_mod(x, g, shift, scale):
    xn = x * lax.rsqrt(jnp.mean(x * x, axis=-1, keepdims=True) + RMS_EPS) * g
    return xn * (1.0 + scale) + shift


def _qkv_kernel(n_xt, x_ref, c_ref, mod_ref, g_ref, w_ref, cos_ref, sin_ref, q_ref, k_ref, v_ref):
    m = mod_ref[0]
    x = jnp.where(pl.program_id(0) < n_xt, x_ref[...], c_ref[...])
    h = _norm_mod(x, g_ref[...], m[0:1], m[1:2])
    p = _dot(h.astype(BF16), w_ref[...])
    cos = cos_ref[...]
    sin = sin_ref[...]
    lane = lax.broadcasted_iota(I32, (TM, 128), 1)
    second = (lane & 16) != 0
    scale = HEAD_DIM ** -0.5 * LOG2E
    nq = N_HEADS * HEAD_DIM // 128
    nk = 2 * N_KV_HEADS * HEAD_DIM // 128
    for c in range(nq + nk):
        blk = p[:, 128 * c:128 * (c + 1)]
        partner = jnp.where(second, pltpu.roll(blk, 16, 1), pltpu.roll(blk, 112, 1))
        r = blk * cos + partner * sin
        if c < nq:
            q_ref[:, 128 * c:128 * (c + 1)] = (r * scale).astype(BF16)
        else:
            k_ref[:, 128 * (c - nq):128 * (c - nq + 1)] = r.astype(BF16)
    v_ref[...] = p[:, 128 * (nq + nk):].astype(BF16)


def _qkv(x2d, c2d, mod, g, w, cos, sin, n_xt, tpb, nb):
    t = x2d.shape[0] + c2d.shape[0]
    nt = t // TM
    kw = 2 * N_KV_HEADS * HEAD_DIM
    row = lambda i: jnp.where(i < n_xt, i // tpb, nb)
    pos = lambda i: jnp.where(i < n_xt, i % tpb, tpb)
    return pl.pallas_call(
        functools.partial(_qkv_kernel, n_xt),
        grid=(nt,),
        in_specs=[pl.BlockSpec((TM, D_MODEL), lambda i: (jnp.minimum(i, n_xt - 1), 0)),
                  pl.BlockSpec((TM, D_MODEL), lambda i: (jnp.maximum(i - n_xt, 0), 0)),
                  pl.BlockSpec((1, 6, D_MODEL), lambda i: (row(i), 0, 0)),
                  pl.BlockSpec((1, D_MODEL), lambda i: (0, 0)),
                  pl.BlockSpec(w.shape, lambda i: (0, 0)),
                  pl.BlockSpec((TM, 128), lambda i: (pos(i), 0)),
                  pl.BlockSpec((TM, 128), lambda i: (pos(i), 0))],
        out_specs=[pl.BlockSpec((TM, D_MODEL), lambda i: (i, 0)),
                   pl.BlockSpec((TM, kw), lambda i: (i, 0)),
                   pl.BlockSpec((TM, kw), lambda i: (i, 0))],
        out_shape=[jax.ShapeDtypeStruct((t, D_MODEL), BF16),
                   jax.ShapeDtypeStruct((t, kw), BF16),
                   jax.ShapeDtypeStruct((t, kw), BF16)],
        compiler_params=_cparams(("arbitrary",)),
        name="qkv_rope",
    )(x2d, c2d, mod, g, w, cos, sin)


def _rope_tables(seq):
    s = jnp.arange(seq)
    row = (s // GRID_W).astype(F32)
    col = (s % GRID_W).astype(F32)
    n_freq = HEAD_DIM // 4
    inv = jnp.exp(-math.log(ROPE_THETA) * jnp.arange(n_freq, dtype=F32) / n_freq)
    ar = row[:, None] * inv
    ac = col[:, None] * inv
    cos = jnp.concatenate([jnp.cos(ar), jnp.cos(ar), jnp.cos(ac), jnp.cos(ac)], axis=-1)
    sin = jnp.concatenate([-jnp.sin(ar), jnp.sin(ar), -jnp.sin(ac), jnp.sin(ac)], axis=-1)
    cos = jnp.concatenate([jnp.tile(cos, (1, 2)), jnp.ones((TM, 128), F32)], axis=0)
    sin = jnp.concatenate([jnp.tile(sin, (1, 2)), jnp.zeros((TM, 128), F32)], axis=0)
    return cos, sin


def _attn_kernel(nqb, sink_ref, q_ref, kp_ref, kc_ref, kn_ref, kx_ref,
                 vp_ref, vc_ref, vn_ref, vx_ref, o_ref):
    n = pl.program_id(1)
    qb = ATTN_BLOCK
    nloc = 3 * qb
    nkeys = nloc + kx_ref.shape[0]
    rows = GROUP * qb
    r = lax.broadcasted_iota(I32, (rows, nloc), 0) % qb
    c = lax.broadcasted_iota(I32, (rows, nloc), 1)
    d = c - r
    ok = (d >= 0) & (d <= 2 * WINDOW) & (n < nqb)
    ok = ok & ((n > 0) | (c >= qb)) & ((n < nqb - 1) | (c < 2 * qb))
    lane = lax.broadcasted_iota(I32, (qb, 128), 1)
    rsub = lax.broadcasted_iota(I32, (rows, 1), 0) // qb
    def score(j):
        ks = slice(128 * j, 128 * (j + 1))
        kall = jnp.concatenate([kp_ref[:, ks], kc_ref[:, ks], kn_ref[:, ks], kx_ref[:, ks]], axis=0)
        qs = []
        for g in range(GROUP):
            pair, half = divmod(g, 2)
            qp = q_ref[:, 256 * j + 128 * pair:256 * j + 128 * (pair + 1)]
            keep = (lane >= 64) if half else (lane < 64)
            qs.append(jnp.where(keep, qp, jnp.zeros_like(qp)))
        qst = jnp.concatenate(qs, axis=0)
        return _dot_nt(qst, kall)

    def softmax(j, s):
        sink = jnp.zeros((rows, 1), F32)
        for g in range(GROUP):
            sink = jnp.where(rsub == g, sink_ref[GROUP * j + g] * LOG2E, sink)
        s = jnp.concatenate([jnp.where(ok, s[:, :nloc], -jnp.inf), s[:, nloc:]], axis=1)
        m = jnp.maximum(jnp.max(s, axis=-1, keepdims=True), sink)
        p = jnp.exp2(s - m)
        den = jnp.sum(p, axis=-1, keepdims=True) + jnp.exp2(sink - m)
        return p.astype(BF16), 1.0 / den

    def values(j, pb, rden):
        ks = slice(128 * j, 128 * (j + 1))
        vall = jnp.concatenate([vp_ref[:, ks], vc_ref[:, ks], vn_ref[:, ks], vx_ref[:, ks]], axis=0)
        o = _dot(pb, vall) * rden
        for pair in range(GROUP // 2):
            o0 = o[(2 * pair) * qb:(2 * pair + 1) * qb]
            o1 = o[(2 * pair + 1) * qb:(2 * pair + 2) * qb]
            o_ref[:, 256 * j + 128 * pair:256 * j + 128 * (pair + 1)] = (
                jnp.where(lane < 64, o0, o1).astype(BF16))

    nj = N_KV_HEADS
    sc = {0: score(0)}
    pr = {}
    for j in range(nj):
        if j + 1 < nj:
            sc[j + 1] = score(j + 1)
        pr[j] = softmax(j, sc.pop(j))
        if j >= 1:
            values(j - 1, *pr.pop(j - 1))
    values(nj - 1, *pr.pop(nj - 1))


def _attention(q, k, v, sink, nb, seq, ctx_len):
    t = q.shape[0]
    qb = ATTN_BLOCK
    nqb = seq // qb
    ncb = ctx_len // qb
    xq = nb * nqb
    kw = k.shape[1]
    qrow = lambda b, n: jnp.where(n < nqb, b * nqb + n, xq + b * ncb + (n - nqb))
    kprev = lambda b, n, s: (b * nqb + jnp.clip(n - 1, 0, nqb - 1), 0)
    kcur = lambda b, n, s: (b * nqb + jnp.minimum(n, nqb - 1), 0)
    knext = lambda b, n, s: (b * nqb + jnp.minimum(n + 1, nqb - 1), 0)
    kctx = lambda b, n, s: (nb * seq // ctx_len + b, 0)
    kspec = lambda f: pl.BlockSpec((qb, kw), f)
    xspec = pl.BlockSpec((ctx_len, kw), kctx)
    return pl.pallas_call(
        functools.partial(_attn_kernel, nqb),
        grid_spec=pltpu.PrefetchScalarGridSpec(
            num_scalar_prefetch=1,
            grid=(nb, nqb + ncb),
            in_specs=[pl.BlockSpec((qb, D_MODEL), lambda b, n, s: (qrow(b, n), 0)),
                      kspec(kprev), kspec(kcur), kspec(knext), xspec,
                      kspec(kprev), kspec(kcur), kspec(knext), xspec],
            out_specs=pl.BlockSpec((qb, D_MODEL), lambda b, n, s: (qrow(b, n), 0)),
        ),
        out_shape=jax.ShapeDtypeStruct((t, D_MODEL), BF16),
        compiler_params=_cparams(("arbitrary", "arbitrary")),
        name="window_attn",
    )(sink, q, k, k, k, k, v, v, v, v)


def _post_kernel(n_xt, m_ref, w_ref, x_ref, c_ref, mod_ref, g_ref, rw_ref, rb_ref,
                 x1_ref, h2_ref, pos_ref, gate_ref, cnt_ref):
    latent = pl.program_id(0) * RT < n_xt
    md = mod_ref[0]
    ne = N_EXPERTS
    per = ne // N_GROUPS
    tw = RT * TM
    ys = [_dot(m_ref[s * TM:(s + 1) * TM, :], w_ref[...]) for s in range(RT)]
    logits = []
    for s in range(RT):
        rows = slice(s * TM, (s + 1) * TM)
        x = jnp.where(latent, x_ref[rows, :], c_ref[rows, :])
        x1 = x + md[2:3] * ys[s]
        x1_ref[rows, :] = x1
        h2 = _norm_mod(x1, g_ref[...], md[3:4], md[4:5])
        h2b = h2.astype(BF16)
        h2_ref[rows, :] = h2b
        h2lo = (h2 - h2b.astype(F32)).astype(BF16)
        lg = _dot(h2b, rw_ref[...]) + _dot(h2lo, rw_ref[...])
        lg = lg + pltpu.roll(lg, ne, 1)
        logits.append(lg.T[0:ne])
    logit = jnp.concatenate(logits, axis=1)
    sc = jax.nn.sigmoid(logit)
    sel = sc + rb_ref[...]
    sub8 = lax.broadcasted_iota(I32, (per, tw), 0)
    gs = jnp.zeros((N_GROUPS, tw), F32)
    gi = lax.broadcasted_iota(I32, (N_GROUPS, tw), 0)
    for g in range(N_GROUPS):
        blk = sel[per * g:per * (g + 1)]
        m1 = jnp.max(blk, axis=0, keepdims=True)
        i1 = jnp.min(jnp.where(blk == m1, sub8, per), axis=0, keepdims=True)
        m2 = jnp.max(jnp.where(sub8 == i1, -jnp.inf, blk), axis=0, keepdims=True)
        gs = jnp.where(gi == g, m1 + m2, gs)
    grank = jnp.zeros((N_GROUPS, tw), F32)
    for g in range(N_GROUPS):
        v = gs[g:g + 1]
        grank = grank + jnp.where(gi > g, jnp.where(v >= gs, 1.0, 0.0), jnp.where(v > gs, 1.0, 0.0))
    ei = lax.broadcasted_iota(I32, (ne, tw), 0)
    gsel = jnp.zeros((ne, tw), F32)
    for g in range(N_GROUPS):
        gsel = jnp.where(ei // per == g, grank[g:g + 1], gsel)
    selm = jnp.where(gsel < TOPK_GROUPS, sel, -jnp.inf)
    eif = ei.astype(F32)
    s8 = jnp.zeros((ne, tw), F32)
    for _ in range(TOP_K):
        best = jnp.max(selm, axis=0, keepdims=True)
        first = jnp.min(jnp.where(selm == best, eif, float(ne)), axis=0, keepdims=True)
        hit = eif == first
        s8 = jnp.where(hit, 1.0, s8)
        selm = jnp.where(hit, -jnp.inf, selm)
    ws = s8 * sc
    gate = ws / jnp.sum(ws, axis=0, keepdims=True) * ROUTED_SCALE

    s8b = s8.astype(BF16)
    er = lax.broadcasted_iota(I32, (ne, ne), 0)
    ec = lax.broadcasted_iota(I32, (ne, ne), 1)
    lower = jnp.where(ec < er, 1.0, 0.0).astype(BF16)
    tr = lax.broadcasted_iota(I32, (TM, TM), 0)
    tc = lax.broadcasted_iota(I32, (TM, TM), 1)
    upper = jnp.where(tr < tc, 1.0, 0.0).astype(BF16)
    slot = _dot(lower, s8b)
    prows = []
    for s in range(RT):
        cols = slice(s * TM, (s + 1) * TM)
        rank = _dot(s8b[:, cols], upper)
        cnt = jnp.sum(s8[:, cols], axis=1, keepdims=True)
        pc = jnp.floor((cnt + (CHUNK - 1)) * (1.0 / CHUNK))
        pcb = jnp.broadcast_to(pc, (ne, 128))
        cnt_ref[s] = pcb
        loc = _dot(lower, pcb.astype(BF16))[:, 0:1] * CHUNK
        prows.append(loc + rank)
    prow = jnp.concatenate(prows, axis=1)
    k8 = lax.broadcasted_iota(I32, (TOP_K, tw), 0)
    pos_t = jnp.zeros((TOP_K, tw), F32)
    gate_t = jnp.zeros((TOP_K, tw), F32)
    for k in range(TOP_K):
        mk = jnp.where(slot == k, s8, 0.0)
        pos_t = jnp.where(k8 == k, jnp.sum(mk * prow, axis=0, keepdims=True), pos_t)
        gate_t = jnp.where(k8 == k, jnp.sum(mk * gate, axis=0, keepdims=True), gate_t)
    pos_ref[...] = pos_t.astype(I32)
    gate_ref[...] = gate_t


def _post(m, w, x2d, c2d, mod, g, rw_t, rb, nt, n_xt, tpb, nb):
    assert nt % RT == 0 and n_xt % RT == 0 and tpb % RT == 0
    t = nt * TM
    kd = m.shape[1]
    rm = RT * TM
    nxs = n_xt // RT
    row = lambda i: jnp.where(i < nxs, i * RT // tpb, nb)
    return pl.pallas_call(
        functools.partial(_post_kernel, n_xt),
        grid=(nt // RT,),
        in_specs=[pl.BlockSpec((rm, kd), lambda i: (i, 0)),
                  pl.BlockSpec(w.shape, lambda i: (0, 0)),
                  pl.BlockSpec((rm, D_MODEL), lambda i: (jnp.minimum(i, nxs - 1), 0)),
                  pl.BlockSpec((rm, D_MODEL), lambda i: (jnp.maximum(i - nxs, 0), 0)),
                  pl.BlockSpec((1, 6, D_MODEL), lambda i: (row(i), 0, 0)),
                  pl.BlockSpec((1, D_MODEL), lambda i: (0, 0)),
                  pl.BlockSpec((D_MODEL, 2 * N_EXPERTS), lambda i: (0, 0)),
                  pl.BlockSpec((N_EXPERTS, 1), lambda i: (0, 0))],
        out_specs=[pl.BlockSpec((rm, D_MODEL), lambda i: (i, 0)),
                   pl.BlockSpec((rm, D_MODEL), lambda i: (i, 0)),
                   pl.BlockSpec((TOP_K, rm), lambda i: (0, i)),
                   pl.BlockSpec((TOP_K, rm), lambda i: (0, i)),
                   pl.BlockSpec((RT, N_EXPERTS, 128), lambda i: (i, 0, 0))],
        out_shape=[jax.ShapeDtypeStruct((t, D_MODEL), F32),
                   jax.ShapeDtypeStruct((t, D_MODEL), BF16),
                   jax.ShapeDtypeStruct((TOP_K, t), I32),
                   jax.ShapeDtypeStruct((TOP_K, t), F32),
                   jax.ShapeDtypeStruct((nt, N_EXPERTS, 128), F32)],
        compiler_params=_cparams(("arbitrary",)),
        name="mix_out_route",
    )(m, w, x2d, c2d, mod, g, rw_t, rb)


def _moe_tables(pc, nblk_cap):
    nt = pc.shape[0]
    seg = jnp.sum(pc, axis=0)
    segb = (seg + CPB - 1) // CPB
    seg_end_b = jnp.cumsum(segb)
    seg_start = (seg_end_b - segb) * CPB
    goff = seg_start[None, :] + jnp.cumsum(pc, axis=0) - pc
    loc_end = jnp.cumsum(pc, axis=1)
    loc = loc_end - pc
    nch = loc_end[:, -1]
    nrb = (nch + CPH - 1) // CPH
    j = jnp.arange(NCH_CAP, dtype=I32)
    inrun = (j[None, :, None] >= loc[:, None, :]) & (j[None, :, None] < loc_end[:, None, :])
    shift = jnp.sum(jnp.where(inrun, (goff - loc)[:, None, :], 0), axis=-1)
    live = j[None, :] < nch[:, None]
    trash = nblk_cap * CPB + jnp.arange(nt, dtype=I32)[:, None] * SPARE + j[None, :] - MIN_RB * CPH
    dst = jnp.where(live, shift + j[None, :], trash).astype(I32)
    src = jnp.where(live, shift + j[None, :], 0).astype(I32)
    subs = (seg + CPH - 1) // CPH
    tl = jnp.arange(CPH - 1, dtype=I32)
    tail_len = subs * CPH - seg
    tails = jnp.where(tl[None, :] < tail_len[:, None], (seg_start + seg)[:, None] + tl[None, :], -1)
    nused = seg_end_b[-1]
    blk = jnp.arange(nblk_cap, dtype=I32)
    bexp = jnp.sum((blk[:, None] >= seg_end_b[None, :]).astype(I32), axis=-1)
    last = jnp.sum((nused - 1 >= seg_end_b).astype(I32))
    bexp = jnp.where(blk < nused, bexp, last).astype(I32)
    onehot = (bexp[:, None] == jnp.arange(N_EXPERTS, dtype=I32)[None, :]).astype(I32)
    first_blk = seg_end_b - segb
    bsub = jnp.sum(onehot * (subs - NSUB * (blk[:, None] - first_blk[None, :])), axis=-1)
    bsub = jnp.where(blk < nused, jnp.clip(bsub, 1, NSUB), 0).astype(I32)
    eid = jnp.arange(N_EXPERTS, dtype=I32)
    owns = segb > 0
    ordinal = jnp.cumsum(owns.astype(I32)) - 1
    later = jnp.where(owns[None, :] & (eid[None, :] > eid[:, None]), eid[None, :], N_EXPERTS)
    nxt_e = jnp.min(later, axis=1)
    nxt_e = jnp.where(nxt_e < N_EXPERTS, nxt_e, -1)
    bpar = jnp.sum(onehot * (ordinal % 2)[None, :], axis=-1).astype(I32)
    bnext = jnp.sum(onehot * nxt_e[None, :], axis=-1).astype(I32)
    return dict(nrb=nrb.astype(I32), dst=dst.reshape(-1), src=src.reshape(-1),
                tails=tails.reshape(-1).astype(I32),
                nused=nused.reshape(1).astype(I32), bexp=bexp, bsub=bsub, bpar=bpar, bnext=bnext)


def _rows_copy(src, dst, sem):
    return pltpu.make_async_copy(src, dst, sem)


def _row_onehot(riota, pos, base, value):
    p = jnp.zeros((TM, TM), BF16)
    for k in range(TOP_K):
        d = jnp.clip(pos[k:k + 1] - base, -1, TM).astype(F32).astype(BF16)
        v = 1.0 if value is None else value[k:k + 1]
        p = jnp.where(riota == d, v, p)
    return p


def _dispatch_kernel(nt, nrb_ref, dst_ref, tails_ref, h2_ref, pos_ref, xs_hbm, buf, sem):
    i = pl.program_id(0)
    slot = i % 2

    def drain(tile, sl):
        def body(rb, carry):
            _rows_copy(buf.at[sl, pl.ds(0, TM)], xs_hbm.at[pl.ds(0, TM)], sem.at[sl]).wait()
            return carry
        lax.fori_loop(0, jnp.maximum(nrb_ref[tile], EAGER_RB), body, 0)

    @pl.when((i >= 2) & (i < nt))
    def _():
        drain(i - 2, slot)

    @pl.when(i < nt)
    def _():
        h2 = h2_ref[...]
        pos = pos_ref[...]
        riota = lax.broadcasted_iota(I32, (TM, TM), 0).astype(F32).astype(BF16)
        nrb = nrb_ref[i]

        def rows_of(b):
            return _dot(_row_onehot(riota, pos, b * TM, None), h2).astype(BF16)

        def send(b, xb):
            buf[slot, b * TM:(b + 1) * TM, :] = xb
            for c in range(CPH):
                d = dst_ref[i * NCH_CAP + b * CPH + c]
                _rows_copy(buf.at[slot, pl.ds(b * TM + c * CHUNK, CHUNK)],
                           xs_hbm.at[pl.ds(pl.multiple_of(d * CHUNK, CHUNK), CHUNK)],
                           sem.at[slot]).start()

        def straight(count):
            prev = rows_of(0)
            for b in range(1, count):
                cur = rows_of(b)
                send(b - 1, prev)
                prev = cur
            send(count - 1, prev)

        pl.when(nrb <= EAGER_RB)(functools.partial(straight, EAGER_RB))
        pl.when(nrb > EAGER_RB)(functools.partial(straight, EAGER_RB + 1))
        for b in range(EAGER_RB + 1, NRB_CAP):
            pl.when(b < nrb)(lambda b=b: send(b, rows_of(b)))

    @pl.when(i == nt)
    def _():
        drain(nt - 2, nt % 2)
        drain(nt - 1, (nt - 1) % 2)
        buf[0, pl.ds(0, CHUNK), :] = jnp.zeros((CHUNK, D_MODEL), BF16)
        ntail = N_EXPERTS * (CPH - 1)

        def tail(j, carry):
            d = tails_ref[j]

            @pl.when(d >= 0)
            def _():
                _rows_copy(buf.at[0, pl.ds(0, CHUNK)],
                           xs_hbm.at[pl.ds(pl.multiple_of(d * CHUNK, CHUNK), CHUNK)], sem.at[0]).start()
            return carry

        lax.fori_loop(0, ntail, tail, 0)

        def tail_wait(j, carry):
            @pl.when(tails_ref[j] >= 0)
            def _():
                _rows_copy(buf.at[0, pl.ds(0, CHUNK)], xs_hbm.at[pl.ds(0, CHUNK)], sem.at[0]).wait()
            return carry

        lax.fori_loop(0, ntail, tail_wait, 0)


def _dispatch(tb, h2, pos_t, nt, nblk_cap):
    last = nt - 1
    return pl.pallas_call(
        functools.partial(_dispatch_kernel, nt),
        grid_spec=pltpu.PrefetchScalarGridSpec(
            num_scalar_prefetch=3,
            grid=(nt + 1,),
            in_specs=[pl.BlockSpec((TM, D_MODEL), lambda i, *_: (jnp.minimum(i, last), 0)),
                      pl.BlockSpec((TOP_K, TM), lambda i, *_: (0, jnp.minimum(i, last)))],
            out_specs=pl.BlockSpec(memory_space=pl.ANY),
            scratch_shapes=[pltpu.VMEM((2, RCAP, D_MODEL), BF16), pltpu.SemaphoreType.DMA((2,))],
        ),
        out_shape=jax.ShapeDtypeStruct((nblk_cap * BM + nt * SPARE * CHUNK, D_MODEL), BF16),
        compiler_params=_cparams(("arbitrary",)),
        name="moe_dispatch",
    )(tb["nrb"], tb["dst"], tb["tails"], h2, pos_t)


RING = 3
CSLOT = 3


def _expert_kernel(layer, bexp_ref, bsub_ref, bpar_ref, bnext_ref, nused_ref,
                   xs_hbm, wg_hbm, wu_hbm, wd_hbm, ys_hbm,
                   xb, yb, stg_g, stg_u, stg_d, wgu, wdn, xsem, ysem, wsem):
    i = pl.program_id(0)
    nused = nused_ref[0]

    def x_copy(blk, sl, n):
        return _rows_copy(xs_hbm.at[pl.ds(pl.multiple_of(blk * BM, BM), n * HB)],
                          xb.at[sl, pl.ds(0, n * HB)], xsem.at[sl])

    def y_copy(blk, sl, n):
        return _rows_copy(yb.at[sl, pl.ds(0, n * HB)],
                          ys_hbm.at[pl.ds(pl.multiple_of(blk * BM, BM), n * HB)], ysem.at[sl])

    def sized(blk, fn):
        live = bsub_ref[blk]
        for n in range(1, NSUB + 1):
            pl.when(live == n)(functools.partial(fn, n))

    def w_copies(ex, sl):
        return (_rows_copy(wg_hbm.at[layer, ex], stg_g.at[sl], wsem.at[sl]),
                _rows_copy(wu_hbm.at[layer, ex], stg_u.at[sl], wsem.at[sl]),
                _rows_copy(wd_hbm.at[layer, ex], stg_d.at[sl], wsem.at[sl]))

    @pl.when(i == 0)
    def _():
        sized(0, lambda n: x_copy(0, 0, n).start())

        @pl.when(nused > 1)
        def _():
            sized(1, lambda n: x_copy(1, 1, n).start())
        for c in w_copies(bexp_ref[0], bpar_ref[0]):
            c.start()

    @pl.when(i < nused)
    def _():
        sl = i % RING

        @pl.when(i + 2 < nused)
        def _():
            sized(i + 2, lambda n: x_copy(i + 2, (i + 2) % RING, n).start())

        e = bexp_ref[i]
        par = bpar_ref[i]

        @pl.when((i == 0) | (e != bexp_ref[jnp.maximum(i - 1, 0)]))
        def _():
            for c in w_copies(e, par):
                c.wait()
            wgu[:, :D_EXPERT] = stg_g[par].astype(BF16)
            wgu[:, D_EXPERT:] = stg_u[par].astype(BF16)
            wdn[...] = stg_d[par].astype(BF16)
            nxt = bnext_ref[i]

            @pl.when(nxt >= 0)
            def _():
                for c in w_copies(nxt, 1 - par):
                    c.start()

        @pl.when(i >= RING)
        def _():
            sized(i - RING, lambda n: y_copy(i - RING, sl, n).wait())

        def step(n):
            x_copy(i, sl, n).wait()
            def up(c):
                return _dot(xb[sl, pl.ds(c * HB, HB), :], wgu[...])

            def act(gu):
                return (_silu(gu[:, :D_EXPERT]) * gu[:, D_EXPERT:]).astype(BF16)

            def down(c, h):
                yb[sl, pl.ds(c * HB, HB), :] = _dot(h, wdn[...]).astype(BF16)

            gus = {0: up(0)}
            hs = {}
            for c in range(n):
                if c + 1 < n:
                    gus[c + 1] = up(c + 1)
                hs[c] = act(gus.pop(c))
                if c >= 1:
                    down(c - 1, hs.pop(c - 1))
            down(n - 1, hs.pop(n - 1))
            y_copy(i, sl, n).start()

        sized(i, step)

        @pl.when(i == nused - 1)
        def _():
            for back in range(RING):
                @pl.when(i - back >= 0)
                def _():
                    sized(i - back, lambda n, back=back: y_copy(i - back, (i - back) % RING, n).wait())


def _experts(tb, xs, w_gate, w_up, w_down, layer, nblk_cap):
    hbm = pl.BlockSpec(memory_space=pl.ANY)
    return pl.pallas_call(
        functools.partial(_expert_kernel, layer),
        grid_spec=pltpu.PrefetchScalarGridSpec(
            num_scalar_prefetch=5,
            grid=(nblk_cap,),
            in_specs=[hbm, hbm, hbm, hbm],
            out_specs=hbm,
            scratch_shapes=[pltpu.VMEM((RING, BM, D_MODEL), BF16),
                            pltpu.VMEM((RING, BM, D_MODEL), BF16),
                            pltpu.VMEM((2, D_MODEL, D_EXPERT), F32),
                            pltpu.VMEM((2, D_MODEL, D_EXPERT), F32),
                            pltpu.VMEM((2, D_EXPERT, D_MODEL), F32),
                            pltpu.VMEM((D_MODEL, 2 * D_EXPERT), BF16),
                            pltpu.VMEM((D_EXPERT, D_MODEL), BF16),
                            pltpu.SemaphoreType.DMA((RING,)),
                            pltpu.SemaphoreType.DMA((RING,)),
                            pltpu.SemaphoreType.DMA((2,))],
        ),
        out_shape=jax.ShapeDtypeStruct((nblk_cap * BM, D_MODEL), BF16),
        compiler_params=_cparams(("arbitrary",)),
        name="moe_experts",
    )(tb["bexp"], tb["bsub"], tb["bpar"], tb["bnext"], tb["nused"], xs, w_gate, w_up, w_down)


def _combine_kernel(final, nt, nrb_ref, src_ref, ys_hbm, pos_ref, gate_ref, h2_ref, x1_ref,
                    mod_ref, sgu_ref, sdn_ref, fg_ref, o_ref, buf, acc, sem):
    i = pl.program_id(0)
    slot = i % CSLOT
    nxt = jnp.minimum(i + CSLOT - 1, nt - 1)
    nrb = nrb_ref[i]
    nrb_next = nrb_ref[nxt]
    nslot = (i + CSLOT - 1) % CSLOT

    def fetch(tile, sl, b):
        for c in range(CPH):
            s = src_ref[tile * NCH_CAP + b * CPH + c]
            _rows_copy(ys_hbm.at[pl.ds(pl.multiple_of(s * CHUNK, CHUNK), CHUNK)],
                       buf.at[sl, pl.ds(b * TM + c * CHUNK, CHUNK)], sem.at[sl]).start()

    def wait_blocks(sl, n):
        def body(rb, carry):
            _rows_copy(ys_hbm.at[pl.ds(0, TM)], buf.at[sl, pl.ds(0, TM)], sem.at[sl]).wait()
            return carry
        lax.fori_loop(0, jnp.maximum(n, EAGER_RB), body, 0)

    @pl.when(i == 0)
    def _():
        for tile in range(min(CSLOT - 1, nt)):
            for b in range(EAGER_RB):
                fetch(tile, tile, b)
            for b in range(EAGER_RB, NRB_CAP):
                pl.when(b < nrb_ref[tile])(functools.partial(fetch, tile, tile, b))

    wait_blocks(slot, nrb)

    pos = pos_ref[...]
    gate = gate_ref[...].astype(BF16)
    riota = lax.broadcasted_iota(I32, (TM, TM), 0).astype(F32).astype(BF16)
    tlhs = (((0,), (0,)), ((), ()))

    def straight(count):
        gu = _dot(h2_ref[...], sgu_ref[...])
        hs = _silu(gu[:, :D_EXPERT]) * gu[:, D_EXPERT:]
        tot = _dot(hs.astype(BF16), sdn_ref[...])
        onehots = [_row_onehot(riota, pos, b * TM, gate) for b in range(count)]
        for b in range(count):
            if b < EAGER_RB:
                fetch(nxt, nslot, b)
            tot = tot + lax.dot_general(onehots[b], buf[slot, b * TM:(b + 1) * TM, :], tlhs,
                                        preferred_element_type=F32)
        acc[...] = tot

    pl.when(nrb <= EAGER_RB)(functools.partial(straight, EAGER_RB))
    pl.when(nrb > EAGER_RB)(functools.partial(straight, EAGER_RB + 1))
    for b in range(EAGER_RB, NRB_CAP):
        pl.when(b < nrb_next)(functools.partial(fetch, nxt, nslot, b))
    for b in range(EAGER_RB + 1, NRB_CAP):
        @pl.when(b < nrb)
        def _():
            pb = _row_onehot(riota, pos, b * TM, gate)
            acc[...] += lax.dot_general(pb, buf[slot, b * TM:(b + 1) * TM, :], tlhs,
                                        preferred_element_type=F32)

    @pl.when(i == nt - 1)
    def _():
        for back in range(min(CSLOT - 1, nt)):
            wait_blocks((i - back + CSLOT - 1) % CSLOT, nrb_next)

    x2 = x1_ref[...] + mod_ref[0][5:6] * acc[...]
    if final:
        x2 = x2 * lax.rsqrt(jnp.mean(x2 * x2, axis=-1, keepdims=True) + RMS_EPS) * fg_ref[...]
    o_ref[...] = x2


def _combine(tb, ys, pos, gate, h2, x1, mod, sgu, sdn, fg, nt, n_xt, tpb, nb, final):
    t = nt * TM
    row = lambda i, *_: (jnp.where(i < n_xt, i // tpb, nb), 0, 0)
    tile = lambda i, *_: (i, 0)
    slots = lambda i, *_: (0, i)
    const = lambda i, *_: (0, 0)
    return pl.pallas_call(
        functools.partial(_combine_kernel, final, nt),
        grid_spec=pltpu.PrefetchScalarGridSpec(
            num_scalar_prefetch=2,
            grid=(nt,),
            in_specs=[pl.BlockSpec(memory_space=pl.ANY),
                      pl.BlockSpec((TOP_K, TM), slots),
                      pl.BlockSpec((TOP_K, TM), slots),
                      pl.BlockSpec((TM, D_MODEL), tile),
                      pl.BlockSpec((TM, D_MODEL), tile),
                      pl.BlockSpec((1, 6, D_MODEL), row),
                      pl.BlockSpec(sgu.shape, const),
                      pl.BlockSpec(sdn.shape, const),
                      pl.BlockSpec((1, D_MODEL), const)],
            out_specs=pl.BlockSpec((TM, D_MODEL), tile),
            scratch_shapes=[pltpu.VMEM((CSLOT, RCAP, D_MODEL), BF16),
                            pltpu.VMEM((TM, D_MODEL), F32),
                            pltpu.SemaphoreType.DMA((CSLOT,))],
        ),
        out_shape=jax.ShapeDtypeStruct((t, D_MODEL), F32),
        compiler_params=_cparams(("arbitrary",)),
        name="moe_combine",
    )(tb["nrb"], tb["src"], ys, pos, gate, h2, x1, mod, sgu, sdn, fg)


def _moe(h2, pos_t, gate_t, cnt, x1, mod, w_gate, w_up, w_down, layer, sgu, sdn, fg,
         nt, n_xt, tpb, nb, final):
    nblk_cap = (TOP_K * nt * TM + (CHUNK - 1) * N_EXPERTS * nt) // BM + N_EXPERTS
    tb = _moe_tables(cnt[:, :, 0].astype(I32), nblk_cap)
    xs = _dispatch(tb, h2, pos_t, nt, nblk_cap)
    ys = _experts(tb, xs, w_gate, w_up, w_down, layer, nblk_cap)
    return _combine(tb, ys, pos_t, gate_t, h2, x1, mod, sgu, sdn, fg, nt, n_xt, tpb, nb, final)


def _lru_in_kernel(x_ref, mod_ref, g_ref, w_ref, gate_ref, u_ref):
    m = mod_ref[0]
    h = _norm_mod(x_ref[...], g_ref[...], m[0:1], m[1:2])
    p = _dot(h.astype(BF16), w_ref[...])
    z = p[:, :D_RNN]
    cdf = 0.5 * (1.0 + jnp.tanh(math.sqrt(2.0 / math.pi) * (z + 0.044715 * (z * z * z))))
    gate_ref[...] = (z * cdf).astype(BF16)
    u_ref[...] = p[:, D_RNN:]


def _lru_in(xall, mod, g, w, n_xt, tpb, nb):
    t = xall.shape[0]
    nt = t // TM
    row = lambda i: jnp.where(i < n_xt, i // tpb, nb)
    return pl.pallas_call(
        _lru_in_kernel,
        grid=(nt,),
        in_specs=[pl.BlockSpec((TM, D_MODEL), lambda i: (i, 0)),
                  pl.BlockSpec((1, 6, D_MODEL), lambda i: (row(i), 0, 0)),
                  pl.BlockSpec((1, D_MODEL), lambda i: (0, 0)),
                  pl.BlockSpec(w.shape, lambda i: (0, 0))],
        out_specs=[pl.BlockSpec((TM, D_RNN), lambda i: (i, 0)),
                   pl.BlockSpec((TM, D_RNN), lambda i: (i, 0))],
        out_shape=[jax.ShapeDtypeStruct((t, D_RNN), BF16),
                   jax.ShapeDtypeStruct((t, D_RNN), F32)],
        compiler_params=_cparams(("arbitrary",)),
        name="lru_in",
    )(xall, mod, g, w)


def _gate_windows():
    wins = []
    for c0 in range(0, D_RNN, 256):
        c1 = min(c0 + 256, D_RNN)
        k0 = (c0 // LRU_BLOCK_W) * LRU_BLOCK_W
        k1 = ((c1 - 1) // LRU_BLOCK_W + 1) * LRU_BLOCK_W
        wins.append((c0, c1, (k0 // 128) * 128, min(-(-k1 // 128) * 128, D_RNN)))
    return wins


def _block_diag_dot(ub, w_ref):
    return jnp.concatenate([_dot(ub[:, k0:k1], w_ref[k0:k1, c0:c1]) for c0, c1, k0, k1 in _gate_windows()],
                           axis=1)


def _lru_sweep_kernel(rev, mix, conv, ns, u_ref, up_ref, un_ref, cw_ref, cb_ref, wa_ref, wx_ref,
                      ba_ref, bx_ref, lam_ref, h0_ref, *rest):
    rest = list(rest)
    hf_ref, gate_ref = (rest.pop(0), rest.pop(0)) if mix else (None, None)
    o_ref = rest.pop(0)
    uc_ref = rest.pop(0) if conv == "emit" else None
    a_s, b_s, carry = rest
    s = pl.program_id(1)
    ss = (ns - 1 - s) if rev else s

    @pl.when(s == 0)
    def _():
        carry[...] = jnp.broadcast_to(h0_ref[0], (8, D_RNN))

    if conv == "reuse":
        u = u_ref[...]
    else:
        prev = jnp.where(ss > 0, up_ref[...], 0.0)
        nxt = jnp.where(ss < ns - 1, un_ref[...], 0.0)
        ext = jnp.concatenate([prev, u_ref[...], nxt], axis=0)
        n_ext = TM + 16
        cw = cw_ref[...]
        u = cb_ref[...]
        for j in range(CONV_W):
            shift = (CONV_LEFT - j) % n_ext
            tap = ext if shift == 0 else pltpu.roll(ext, shift, 0)
            u = u + cw[j:j + 1] * tap[8:8 + TM]
        if uc_ref is not None:
            uc_ref[...] = u

    ub = u.astype(BF16)
    ta = jnp.tanh(_block_diag_dot(ub, wa_ref) + ba_ref[...])
    tx = jnp.tanh(_block_diag_dot(ub, wx_ref) + bx_ref[...])
    nl = -lam_ref[...]
    c0 = (0.5 * LRU_C) * (jnp.maximum(nl, 0.0) + jnp.log1p(jnp.exp(-jnp.abs(nl))))
    y = c0 + c0 * ta
    a = jnp.exp(-y)
    a_s[...] = a
    hu = 0.5 * u
    b_s[...] = jnp.sqrt(jnp.tanh(y) * (1.0 + a * a)) * (hu + hu * tx)

    sub = lax.broadcasted_iota(I32, (8, D_RNN), 0)
    h = carry[...]
    ng = TM // 8
    for gi in range(ng):
        r0 = 8 * ((ng - 1 - gi) if rev else gi)
        a = a_s[r0:r0 + 8, :]
        b = b_s[r0:r0 + 8, :]
        for sh in (1, 2, 4):
            amt = (8 - sh) if rev else sh
            keep = (sub < 8 - sh) if rev else (sub >= sh)
            a_sh = pltpu.roll(a, amt, 0)
            b_sh = pltpu.roll(b, amt, 0)
            b = jnp.where(keep, a * b_sh + b, b)
            a = jnp.where(keep, a * a_sh, a)
        hg = a * h + b
        edge = hg[0:1] if rev else hg[7:8]
        h = jnp.broadcast_to(edge, (8, D_RNN))
        if mix:
            o_ref[r0:r0 + 8, :] = ((hf_ref[r0:r0 + 8, :] + hg)
                                   * gate_ref[r0:r0 + 8, :].astype(F32)).astype(BF16)
        else:
            o_ref[r0:r0 + 8, :] = hg
    carry[...] = h


def _lru_sweep(u_all, row0, nb, seq, cw, cb, wa, wx, ba, bx, lam, h0, rev, hf=None, gate=None,
               conv="compute"):
    ns = seq // TM
    blk0 = row0 // TM
    mix = hf is not None
    sidx = (lambda s: ns - 1 - s) if rev else (lambda s: s)
    cur = lambda b, s: (blk0 + b * ns + sidx(s), 0)
    out = lambda b, s: (b * ns + sidx(s), 0)
    h8 = TM // 8
    nblk8 = u_all.shape[0] // 8
    prv = lambda b, s: (jnp.maximum((blk0 + b * ns + sidx(s)) * h8 - 1, 0), 0)
    nxt = lambda b, s: (jnp.minimum((blk0 + b * ns + sidx(s) + 1) * h8, nblk8 - 1), 0)
    vec = lambda b, s: (0, 0)
    in_specs = [pl.BlockSpec((TM, D_RNN), cur),
                pl.BlockSpec((8, D_RNN), prv),
                pl.BlockSpec((8, D_RNN), nxt),
                pl.BlockSpec((CONV_W, D_RNN), vec),
                pl.BlockSpec((1, D_RNN), vec),
                pl.BlockSpec((D_RNN, D_RNN), vec),
                pl.BlockSpec((D_RNN, D_RNN), vec),
                pl.BlockSpec((1, D_RNN), vec),
                pl.BlockSpec((1, D_RNN), vec),
                pl.BlockSpec((1, D_RNN), vec),
                pl.BlockSpec((1, 1, D_RNN), lambda b, s: (b, 0, 0))]
    args = [u_all, u_all, u_all, cw, cb, wa, wx, ba, bx, lam, h0]
    if mix:
        in_specs += [pl.BlockSpec((TM, D_RNN), out), pl.BlockSpec((TM, D_RNN), cur)]
        args += [hf, gate]
    out_specs = [pl.BlockSpec((TM, D_RNN), out)]
    out_shape = [jax.ShapeDtypeStruct((nb * seq, D_RNN), BF16 if mix else F32)]
    if conv == "emit":
        out_specs.append(pl.BlockSpec((TM, D_RNN), out))
        out_shape.append(jax.ShapeDtypeStruct((nb * seq, D_RNN), F32))
    return pl.pallas_call(
        functools.partial(_lru_sweep_kernel, rev, mix, conv, ns),
        grid=(nb, ns),
        in_specs=in_specs,
        out_specs=out_specs,
        out_shape=out_shape,
        scratch_shapes=[pltpu.VMEM((TM, D_RNN), F32), pltpu.VMEM((TM, D_RNN), F32),
                        pltpu.VMEM((8, D_RNN), F32)],
        compiler_params=_cparams(("arbitrary", "arbitrary")),
        name="lru_sweep_rev" if rev else "lru_sweep_fwd",
    )(*args)


def _block_diag(w):
    rows = jnp.tile(w.reshape(D_RNN, LRU_BLOCK_W), (1, LRU_BLOCKS))
    blk = jnp.arange(D_RNN, dtype=I32) // LRU_BLOCK_W
    return jnp.where(blk[:, None] == blk[None, :], rows, 0.0).astype(BF16)


def kernel(x, c, ctx, c_ctx, ada_w, ada_b, norm1_g, norm2_g, attn_w_qkv, attn_w_o, attn_sink,
           lru_w_in, lru_conv_w, lru_conv_b, lru_wa, lru_ba, lru_wx, lru_bx, lru_lam, lru_w_out,
           moe_router_w, moe_router_bias, moe_w_gate, moe_w_up, moe_w_down,
           shared_w_gate, shared_w_up, shared_w_down, final_g):
    nb, seq, d = x.shape
    ctx_len = ctx.shape[1]
    assert d == D_MODEL and seq % TM == 0 and ctx_len == TM and nb < 8
    tx = nb * seq
    tall = tx + nb * ctx_len
    tpb = seq // TM
    n_xt = tx // TM
    nt_all = tall // TM

    cvec = jnp.zeros((8, d), F32).at[:nb].set(c).at[nb].set(c_ctx)
    mod = _adaln(cvec, ada_w, ada_b).reshape(DEPTH, 8, 6, d)
    x2d = x.reshape(tx, d)
    c2d = ctx.reshape(nb * ctx_len, d)
    row = lambda v: v.reshape(1, -1)
    shared = lambda i: (jnp.concatenate([shared_w_gate[i], shared_w_up[i]], axis=1).astype(BF16),
                        shared_w_down[i].astype(BF16))

    def router(i):
        hi = moe_router_w[i].astype(BF16)
        lo = (moe_router_w[i] - hi.astype(F32)).astype(BF16)
        return jnp.concatenate([hi, lo], axis=1)

    qd = N_HEADS * HEAD_DIM
    kd = N_KV_HEADS * HEAD_DIM
    wq = attn_w_qkv[0][:, :qd]
    wk = attn_w_qkv[0][:, qd:qd + kd].reshape(d, N_KV_HEADS, 1, HEAD_DIM)
    wv = attn_w_qkv[0][:, qd + kd:].reshape(d, N_KV_HEADS, 1, HEAD_DIM)
    dup = lambda w: jnp.broadcast_to(w, (d, N_KV_HEADS, 2, HEAD_DIM)).reshape(d, 2 * kd)
    wqkv = jnp.concatenate([wq, dup(wk), dup(wv)], axis=1).astype(BF16)
    cos, sin = _rope_tables(seq)
    q, k, v = _qkv(x2d, c2d, mod[0], row(norm1_g[0]), wqkv, cos, sin, n_xt, tpb, nb)
    o = _attention(q, k, v, attn_sink[0], nb, seq, ctx_len)
    x1, h2, pos_t, gate_t, cnt = _post(
        o, attn_w_o[0].astype(BF16), x2d, c2d, mod[0], row(norm2_g[0]),
        router(0), moe_router_bias[0].reshape(-1, 1), nt_all, n_xt, tpb, nb)
    sgu, sdn = shared(0)
    xall = _moe(h2, pos_t, gate_t, cnt, x1, mod[0], moe_w_gate, moe_w_up, moe_w_down, 0,
                sgu, sdn, row(final_g), nt_all, n_xt, tpb, nb, False)

    gate, u_pre = _lru_in(xall, mod[1], row(norm1_g[1]), lru_w_in[0].astype(BF16), n_xt, tpb, nb)
    cw, cb = lru_conv_w[0], row(lru_conv_b[0])
    hdir = []
    for dr, rev in ((0, False), (1, True)):
        wa = _block_diag(0.5 * lru_wa[0, dr])
        wx = _block_diag(0.5 * lru_wx[0, dr])
        prm = (cw, cb, wa, wx, row(0.5 * lru_ba[0, dr]), row(0.5 * lru_bx[0, dr]), row(lru_lam[0, dr]))
        zero = jnp.zeros((nb, 1, D_RNN), F32)
        hc = _lru_sweep(u_pre, tx, nb, ctx_len, *prm, zero, rev)[0].reshape(nb, ctx_len, D_RNN)
        h0 = hc[:, 0:1] if rev else hc[:, ctx_len - 1:ctx_len]
        if not rev:
            hdir = _lru_sweep(u_pre, 0, nb, seq, *prm, h0, rev, conv="emit")
        else:
            mixed = _lru_sweep(hdir[1], 0, nb, seq, *prm, h0, rev, hf=hdir[0], gate=gate,
                               conv="reuse")[0]
    x1, h2, pos_t, gate_t, cnt = _post(
        mixed, lru_w_out[0].astype(BF16), xall, xall, mod[1], row(norm2_g[1]),
        router(1), moe_router_bias[1].reshape(-1, 1), n_xt, n_xt, tpb, nb)
    sgu, sdn = shared(1)
    out = _moe(h2, pos_t, gate_t, cnt, x1, mod[1], moe_w_gate, moe_w_up, moe_w_down, 1,
               sgu, sdn, row(final_g), n_xt, n_xt, tpb, nb, True)
    return out.reshape(nb, seq, d)
```
